```python
import math
import jax, jax.numpy as jnp
from jax import lax
import numpy as np

D_MODEL = 1024
BATCH = 8
SEQ = 8192
DEPTH = 4

MIX_WIDTH = D_MODEL
GROUP_WIDTH = MIX_WIDTH // 4
DA_HEADS = 4
DA_QK_DIM = 32
DA_V_DIM = GROUP_WIDTH // DA_HEADS
Q_BLOCK = 128
GLA_HEADS = 4
GLA_K_DIM = 32
GLA_V_DIM = GROUP_WIDTH // GLA_HEADS
GLA_GATE_RANK = 16
GLA_GATE_NORMALIZER = 16.0
GLA_CHUNK = 64
HG_HEADS = 4
HG_K_DIM = 64
HG_V_DIM = GROUP_WIDTH // HG_HEADS
HG_CHUNK = 64
RW_HEADS = 4
RW_HEAD = GROUP_WIDTH // RW_HEADS
RW_DECAY_RANK = 32
RW_AAA_RANK = 32
RW_GATE_RANK = 64
RW_DECAY_SCALE = 0.606531
RW_LN_EPS = 64e-5
D_FF = 4 * D_MODEL
RMS_EPS = 1e-6
MASK_VALUE = -1e30
TINY = 1e-30

A_IN = 3 * GROUP_WIDTH
B_IN = 2 * GLA_HEADS * GLA_K_DIM + GROUP_WIDTH + GLA_GATE_RANK + GROUP_WIDTH
C_IN = HG_HEADS * HG_K_DIM * 2 + GROUP_WIDTH * 2
D_IN = 3 * GROUP_WIDTH + RW_DECAY_RANK + RW_AAA_RANK + RW_GATE_RANK
P_IN = A_IN + B_IN + C_IN + D_IN

kernel_name = "hybrid_parallel_head_group_decoder"


def split_cols(p, sizes):
    idx = [int(s) for s in np.cumsum(sizes)[:-1]]
    return jnp.split(p, idx, axis=-1)


def rms_norm(x, g, eps=RMS_EPS):
    xf = x.astype(jnp.float32)
    y = xf * lax.rsqrt(jnp.mean(xf * xf, axis=-1, keepdims=True) + eps)
    return (y * g.astype(jnp.float32)).astype(x.dtype)


def alibi_slopes(n):
    return 2.0 ** (-8.0 * jnp.arange(1, n + 1, dtype=jnp.float32) / n)


def diff_attention(p, q_norm_g, k_norm_g, lq1, lk1, lq2, lk2, out_norm_g, lam_init):
    B, S, _ = p.shape
    q, k, v = split_cols(p, (GROUP_WIDTH, GROUP_WIDTH, GROUP_WIDTH))
    q = rms_norm(q.reshape(B, S, DA_HEADS, 2, DA_QK_DIM), q_norm_g) * (DA_QK_DIM ** -0.5)
    k = rms_norm(k.reshape(B, S, DA_HEADS, 2, DA_QK_DIM), k_norm_g)
    v = v.reshape(B, S, DA_HEADS, DA_V_DIM).transpose(0, 2, 1, 3)
    lam = (jnp.exp(jnp.sum(lq1 * lk1)) - jnp.exp(jnp.sum(lq2 * lk2)) + lam_init).astype(jnp.float32)
    n_blk = S // Q_BLOCK
    qb = q.reshape(B, n_blk, Q_BLOCK, DA_HEADS, 2, DA_QK_DIM).transpose(1, 0, 3, 4, 2, 5)
    kt = k.transpose(0, 2, 3, 1, 4)
    slopes = alibi_slopes(DA_HEADS)
    k_pos = jnp.arange(S)

    def attend_block(args):
        blk, q_blk = args
        q_pos = blk * Q_BLOCK + jnp.arange(Q_BLOCK)
        dist = (q_pos[:, None] - k_pos[None, :]).astype(jnp.float32)
        bias = jnp.where(dist[None] >= 0, -slopes[:, None, None] * dist[None], MASK_VALUE)
        s = jnp.einsum('bhcqd,bhckd->bhcqk', q_blk, kt).astype(jnp.float32) + bias[None, :, None]
        prob = jax.nn.softmax(s, axis=-1)
        weights = prob[:, :, 0] - lam * prob[:, :, 1]
        return jnp.einsum('bhqk,bhkv->bhqv', weights.astype(v.dtype), v)

    o = lax.map(attend_block, (jnp.arange(n_blk), qb))
    o = o.transpose(1, 0, 3, 2, 4).reshape(B, S, DA_HEADS, DA_V_DIM)
    o = rms_norm(o, out_norm_g) * (1.0 - lam_init)
    return o.reshape(B, S, GROUP_WIDTH)


def chunk_gla(q, k, v, log_f, chunk):
    B, H, T, K = q.shape
    V = v.shape[-1]
    n = T // chunk

    def to_chunks(z):
        return z.astype(jnp.float32).reshape(B, H, n, chunk, z.shape[-1]).transpose(2, 0, 1, 3, 4)

    qc, kc, vc = to_chunks(q), to_chunks(k), to_chunks(v)
    bc = jnp.cumsum(to_chunks(log_f), axis=-2)
    causal = jnp.tril(jnp.ones((chunk, chunk), dtype=bool))[:, :, None]

    def step(state, inp):
        q_i, k_i, v_i, b_i = inp
        o_inter = jnp.einsum('bhck,bhkv->bhcv', q_i * jnp.exp(b_i), state)
        rel = b_i[:, :, :, None, :] - b_i[:, :, None, :, :]
        decay = jnp.exp(jnp.where(causal, rel, MASK_VALUE))
        scores = jnp.einsum('bhik,bhijk->bhij', q_i, decay * k_i[:, :, None, :, :])
        o_intra = jnp.einsum('bhij,bhjv->bhiv', scores, v_i)
        b_last = b_i[:, :, -1:, :]
        state = jnp.exp(b_last[:, :, 0, :, None]) * state + jnp.einsum(
            'bhck,bhcv->bhkv', k_i * jnp.exp(b_last - b_i), v_i)
        return state, o_inter + o_intra

    state0 = jnp.zeros((B, H, K, V), jnp.float32)
    _, o = lax.scan(step, state0, (qc, kc, vc, bc))
    return o.transpose(1, 2, 0, 3, 4).reshape(B, H, T, V)


def gla_mixer(p, gate_up, gate_b, out_norm_g):
    B, S, _ = p.shape
    qk = GLA_HEADS * GLA_K_DIM
    q, k, v, gd, og = split_cols(p, (qk, qk, GROUP_WIDTH, GLA_GATE_RANK, GROUP_WIDTH))
    heads = lambda z, d: z.reshape(B, S, -1, d).transpose(0, 2, 1, 3)
    log_a = jax.nn.log_sigmoid((gd @ gate_up + gate_b).astype(jnp.float32)) / GLA_GATE_NORMALIZER
    o = chunk_gla(heads(q, GLA_K_DIM) * (GLA_K_DIM ** -0.5), heads(k, GLA_K_DIM),
                  heads(v, GLA_V_DIM), heads(log_a, GLA_K_DIM), GLA_CHUNK)
    o = o.transpose(0, 2, 1, 3).astype(p.dtype)
    return rms_norm(o, out_norm_g).reshape(B, S, GROUP_WIDTH) * jax.nn.silu(og)


def hgrn2_mixer(p, lower_bound, out_norm_g):
    B, S, _ = p.shape
    fk = HG_HEADS * HG_K_DIM
    q, f, i, og = split_cols(p, (fk, fk, GROUP_WIDTH, GROUP_WIDTH))
    lb = lower_bound.astype(jnp.float32)
    z = f.astype(jnp.float32)
    forget = lb + (1.0 - lb) * jax.nn.sigmoid(z)
    log_f = jnp.log(jnp.maximum(forget, TINY))
    k = (1.0 - lb) * jax.nn.sigmoid(-z)
    heads = lambda t, d: t.reshape(B, S, -1, d).transpose(0, 2, 1, 3)
    o = chunk_gla(heads(jax.nn.silu(q), HG_K_DIM), heads(k, HG_K_DIM),
                  heads(i, HG_V_DIM), heads(log_f, HG_K_DIM), HG_CHUNK)
    o = o.transpose(0, 2, 1, 3).astype(p.dtype)
    return rms_norm(o, out_norm_g).reshape(B, S, GROUP_WIDTH) * jax.nn.silu(og)


def rwkv7_recurrence(r, w, k, v, kk, a):
    B, S, H, N = r.shape
    tm = lambda z: z.astype(jnp.float32).transpose(1, 0, 2, 3)

    def step(state, inp):
        r_t, w_t, k_t, v_t, kk_t, a_t = inp
        sa = jnp.einsum('bhvk,bhk->bhv', state, -kk_t)
        state = (state * w_t[:, :, None, :] + sa[..., None] * (kk_t * a_t)[:, :, None, :]
                 + v_t[..., None] * k_t[:, :, None, :])
        return state, jnp.einsum('bhvk,bhk->bhv', state, r_t)

    state0 = jnp.zeros((B, H, N, N), jnp.float32)
    _, y = lax.scan(step, state0, (tm(r), tm(w), tm(k), tm(v), tm(kk), tm(a)))
    return y.transpose(1, 0, 2, 3)


def rwkv7_mixer(p, shift_mu, w0, w_up, a0, a_up, g_up, k_k, k_a, r_k, ln_g, ln_b):
    B, S, _ = p.shape
    prev = jnp.pad(p, ((0, 0), (1, 0), (0, 0)))[:, :-1]
    p = p + shift_mu * (prev - p)
    r, k, v, wd, ad, gd = split_cols(p, (GROUP_WIDTH, GROUP_WIDTH, GROUP_WIDTH,
                                         RW_DECAY_RANK, RW_AAA_RANK, RW_GATE_RANK))
    log_w = -RW_DECAY_SCALE * jax.nn.sigmoid(w0 + jnp.tanh(wd) @ w_up)
    a = jax.nn.sigmoid(a0 + ad @ a_up)
    g = jax.nn.sigmoid(gd) @ g_up
    heads = lambda z: z.reshape(B, S, RW_HEADS, RW_HEAD)
    kk = heads(k * k_k).astype(jnp.float32)
    kk = kk * lax.rsqrt(jnp.maximum(jnp.sum(kk * kk, axis=-1, keepdims=True), 1e-24))
    k = k * (1.0 + (a - 1.0) * k_a)
    y = rwkv7_recurrence(heads(r), jnp.exp(heads(log_w).astype(jnp.float32)), heads(k), heads(v), kk, heads(a))
    mu = jnp.mean(y, axis=-1, keepdims=True)
    var = jnp.mean(jnp.square(y - mu), axis=-1, keepdims=True)
    yn = ((y - mu) * lax.rsqrt(var + RW_LN_EPS)).reshape(B, S, GROUP_WIDTH)
    yn = yn * ln_g.astype(jnp.float32) + ln_b.astype(jnp.float32)
    bonus = jnp.sum(heads(r) * heads(k) * r_k, axis=-1, keepdims=True) * heads(v)
    out = (yn + bonus.reshape(B, S, GROUP_WIDTH).astype(jnp.float32)) * g.astype(jnp.float32)
    return out.astype(p.dtype)


def setup_inputs(seed: int = 0) -> dict:
    key = jax.random.key(seed)
    ks = list(jax.random.split(key, 32))
    nrm = lambda shape, scale: scale * jax.random.normal(ks.pop(), shape, jnp.float32)
    gain = lambda shape: 1.0 + nrm(shape, 0.02)
    L = DEPTH
    return {
        "x": nrm((BATCH, SEQ, D_MODEL), 1.0),
        "norm_mix_g": gain((L, D_MODEL)),
        "w_in": nrm((L, D_MODEL, P_IN), D_MODEL ** -0.5),
        "da_q_norm_g": gain((L, DA_QK_DIM)),
        "da_k_norm_g": gain((L, DA_QK_DIM)),
        "da_lambda_q1": nrm((L, DA_QK_DIM), 0.1),
        "da_lambda_k1": nrm((L, DA_QK_DIM), 0.1),
        "da_lambda_q2": nrm((L, DA_QK_DIM), 0.1),
        "da_lambda_k2": nrm((L, DA_QK_DIM), 0.1),
        "da_out_norm_g": gain((L, DA_V_DIM)),
        "gla_gate_up": nrm((L, GLA_GATE_RANK, GLA_HEADS * GLA_K_DIM), GLA_GATE_RANK ** -0.5),
        "gla_gate_b": nrm((L, GLA_HEADS * GLA_K_DIM), 0.1),
        "gla_out_norm_g": gain((L, GLA_V_DIM)),
        "hgrn_lb_logits": nrm((L, HG_HEADS * HG_K_DIM), 0.5),
        "hgrn_out_norm_g": gain((L, HG_V_DIM)),
        "rw_shift_mu": jax.random.uniform(ks.pop(), (L, D_IN), jnp.float32, 0.0, 1.0),
        "rw_w0": -0.5 + nrm((L, GROUP_WIDTH), 0.5),
        "rw_w_up": nrm((L, RW_DECAY_RANK, GROUP_WIDTH), 0.5 * RW_DECAY_RANK ** -0.5),
        "rw_a0": nrm((L, GROUP_WIDTH), 0.1),
        "rw_a_up": nrm((L, RW_AAA_RANK, GROUP_WIDTH), 0.5 * RW_AAA_RANK ** -0.5),
        "rw_g_up": nrm((L, RW_GATE_RANK, GROUP_WIDTH), RW_GATE_RANK ** -0.5),
        "rw_k_k": 0.85 + nrm((L, GROUP_WIDTH), 0.05),
        "rw_k_a": 1.0 + nrm((L, GROUP_WIDTH), 0.05),
        "rw_r_k": nrm((L, RW_HEADS, RW_HEAD), 0.1),
        "rw_ln_g": gain((L, GROUP_WIDTH)),
        "rw_ln_b": nrm((L, GROUP_WIDTH), 0.02),
        "w_out": nrm((L, MIX_WIDTH, D_MODEL), MIX_WIDTH ** -0.5),
        "norm_mlp_g": gain((L, D_MODEL)),
        "w_mlp_up": nrm((L, D_MODEL, D_FF), D_MODEL ** -0.5),
        "w_mlp_down": nrm((L, D_FF, D_MODEL), D_FF ** -0.5),
    }


def reference(x, norm_mix_g, w_in, da_q_norm_g, da_k_norm_g, da_lambda_q1, da_lambda_k1,
              da_lambda_q2, da_lambda_k2, da_out_norm_g, gla_gate_up, gla_gate_b, gla_out_norm_g,
              hgrn_lb_logits, hgrn_out_norm_g, rw_shift_mu, rw_w0, rw_w_up, rw_a0, rw_a_up, rw_g_up,
              rw_k_k, rw_k_a, rw_r_k, rw_ln_g, rw_ln_b, w_out, norm_mlp_g, w_mlp_up, w_mlp_down):
    probs = jax.nn.softmax(hgrn_lb_logits.astype(jnp.float32), axis=0)
    lower_bounds = jnp.cumsum(probs, axis=0) - probs[0]
    for l in range(DEPTH):
        lam_init = 0.8 - 0.6 * math.exp(-0.3 * l)
        h = rms_norm(x, norm_mix_g[l])
        proj = h @ w_in[l]
        pa, pb, pc, pd = split_cols(proj, (A_IN, B_IN, C_IN, D_IN))
        o_a = diff_attention(pa, da_q_norm_g[l], da_k_norm_g[l], da_lambda_q1[l], da_lambda_k1[l],
                             da_lambda_q2[l], da_lambda_k2[l], da_out_norm_g[l], lam_init)
        o_b = gla_mixer(pb, gla_gate_up[l], gla_gate_b[l], gla_out_norm_g[l])
        o_c = hgrn2_mixer(pc, lower_bounds[l], hgrn_out_norm_g[l])
        o_d = rwkv7_mixer(pd, rw_shift_mu[l], rw_w0[l], rw_w_up[l], rw_a0[l], rw_a_up[l], rw_g_up[l],
                          rw_k_k[l], rw_k_a[l], rw_r_k[l], rw_ln_g[l], rw_ln_b[l])
        mixed = jnp.concatenate([o_a, o_b.astype(x.dtype), o_c.astype(x.dtype), o_d], axis=-1)
        x = x + mixed @ w_out[l]
        h = rms_norm(x, norm_mlp_g[l])
        x = x + jnp.square(jax.nn.relu(h @ w_mlp_up[l])) @ w_mlp_down[l]
    return x
```

```python
import functools
import math

import jax
import jax.numpy as jnp
from jax import lax
from jax.experimental import pallas as pl
from jax.experimental.pallas import tpu as pltpu

F32, BF16 = jnp.float32, jnp.bfloat16

D_MODEL = 1024
GROUP = 256
HEADS = 4
HEAD_V = 64
DA_QK = 32
GLA_K = 32
GLA_RANK = 16
GLA_NORMALIZER = 16.0
HG_K = 64
RW_DECAY_SCALE = 0.606531
RW_LN_EPS = 64e-5
D_FF = 4 * D_MODEL
RMS_EPS = 1e-6
MASK_VALUE = -1e30
TINY = 1e-30
DEPTH = 4

LANE = 128
A_W, B_W, C_W, D_W = 768, 896, 1024, 896
P_W = A_W + B_W + C_W + D_W

ROW_TILE = 512
TIME_BLOCK = 256
ATT_BLOCK = 256
GLA_CHUNK = 32
RW_CHUNK = 64
VMEM_LIMIT = 56 * 1024 * 1024


def _dot(a, b):
    return jnp.dot(a, b, preferred_element_type=F32)


def _dot_nt(a, b):
    return lax.dot_general(a, b, (((1,), (1,)), ((), ())), preferred_element_type=F32)


def _dot_tn(a, b):
    return lax.dot_general(a, b, (((0,), (0,)), ((), ())), preferred_element_type=F32)


def _split(x, parts):
    out, r = [], x
    for i in range(parts):
        h = r.astype(BF16)
        out.append(h)
        if i + 1 < parts:
            r = r - h.astype(F32)
    return out


def _dot_x_exact(x, m, parts):
    acc = None
    for h in _split(x, parts):
        d = _dot(h, m)
        acc = d if acc is None else acc + d
    return acc


def _dot_exact_x(m, x, parts):
    acc = None
    for h in _split(x, parts):
        d = _dot(m, h)
        acc = d if acc is None else acc + d
    return acc


def _mm(a, b, parts):
    sa, sb = _split(a, parts), _split(b, parts)
    acc = None
    for i, x in enumerate(sa):
        for j, y in enumerate(sb):
            if i + j < parts:
                d = _dot(x, y)
                acc = d if acc is None else acc + d
    return acc


def _sigmoid(x):
    return 1.0 / (1.0 + jnp.exp(-x))


def _silu(x):
    return x * _sigmoid(x)


def _const_spec(shape):
    zeros = (0,) * len(shape)
    return pl.BlockSpec(shape, lambda *_: zeros)


def _chunk_row_bcast(ref, idx, row, chunk):
    n_rows, width = ref.shape[1], ref.shape[2]
    pieces = []
    for c in range(n_rows // chunk):
        r = c * chunk + row
        pieces.append(jnp.broadcast_to(ref[idx, r:r + 1, :], (chunk, width)))
    return jnp.concatenate(pieces, axis=0)


def _params(*sem):
    return pltpu.CompilerParams(dimension_semantics=sem, vmem_limit_bytes=VMEM_LIMIT)


def _in_proj_body(x_ref, g_ref, w_ref, pa_ref, pb_ref, pc_ref, pd_ref):
    x = x_ref[...]
    h = (x * lax.rsqrt(jnp.mean(x * x, axis=-1, keepdims=True) + RMS_EPS) * g_ref[...]).astype(BF16)
    off = 0
    for o_ref in (pa_ref, pb_ref, pc_ref, pd_ref):
        n = o_ref.shape[1]
        o_ref[...] = _dot(h, w_ref[:, off:off + n])
        off += n


def _in_proj(x, g, w):
    n = x.shape[0]
    tm = min(ROW_TILE, n)
    row = lambda width: pl.BlockSpec((tm, width), lambda i: (i, 0))
    return pl.pallas_call(
        _in_proj_body,
        grid=(n // tm,),
        in_specs=[row(D_MODEL), _const_spec((1, D_MODEL)), _const_spec((D_MODEL, P_W))],
        out_specs=[row(A_W), row(B_W), row(C_W), row(D_W)],
        out_shape=[jax.ShapeDtypeStruct((n, wd), F32) for wd in (A_W, B_W, C_W, D_W)],
        compiler_params=_params("parallel"),
        name="in_proj",
    )(x, g.reshape(1, D_MODEL), w)


def _post_body(x_ref, oa_ref, ob_ref, oc_ref, od_ref, wo_ref, g_ref, wu_ref, wd_ref, out_ref):
    mixed = jnp.concatenate([r[...] for r in (oa_ref, ob_ref, oc_ref, od_ref)], axis=1)
    x1 = x_ref[...] + _dot(mixed, wo_ref[...])
    h = (x1 * lax.rsqrt(jnp.mean(x1 * x1, axis=-1, keepdims=True) + RMS_EPS) * g_ref[...]).astype(BF16)
    y = x1
    for f in range(D_FF // D_MODEL):
        cols = slice(f * D_MODEL, (f + 1) * D_MODEL)
        u = jnp.square(jnp.maximum(_dot(h, wu_ref[:, cols]), 0.0)).astype(BF16)
        y = y + _dot(u, wd_ref[cols, :])
    out_ref[...] = y


def _post(x, outs, w_out, g, w_up, w_down):
    n = x.shape[0]
    tm = min(ROW_TILE, n)
    row = lambda width: pl.BlockSpec((tm, width), lambda i: (i, 0))
    return pl.pallas_call(
        _post_body,
        grid=(n // tm,),
        in_specs=[row(D_MODEL)] + [row(GROUP)] * 4 + [
            _const_spec((D_MODEL, D_MODEL)), _const_spec((1, D_MODEL)),
            _const_spec((D_MODEL, D_FF)), _const_spec((D_FF, D_MODEL))],
        out_specs=row(D_MODEL),
        out_shape=jax.ShapeDtypeStruct((n, D_MODEL), F32),
        compiler_params=_params("parallel"),
        name="post",
    )(x, *outs, w_out, g.reshape(1, D_MODEL), w_up, w_down)


def _group_mean_matrix(width, group):
    i = jnp.arange(width) // group
    return ((i[:, None] == i[None, :]).astype(F32) / group).astype(BF16)


def _group_sum_matrix(width, group):
    i = jnp.arange(width) // group
    return (i[:, None] == i[None, :]).astype(BF16)


def _da_prep_body(pa_ref, gq_ref, gk_ref, g32_ref, q_ref, k_ref, v_ref):
    q = pa_ref[:, 0:GROUP]
    k = pa_ref[:, GROUP:2 * GROUP]
    qm = _dot_x_exact(q * q, g32_ref[...], 2)
    km = _dot_x_exact(k * k, g32_ref[...], 2)
    q_ref[...] = (q * lax.rsqrt(qm + RMS_EPS) * gq_ref[...] * (DA_QK ** -0.5)).astype(BF16)
    k_ref[...] = (k * lax.rsqrt(km + RMS_EPS) * gk_ref[...]).astype(BF16)
    v_ref[...] = pa_ref[:, 2 * GROUP:3 * GROUP].astype(BF16)


def _da_prep(pa, gq, gk):
    n = pa.shape[0]
    tm = min(ROW_TILE, n)
    row = lambda width: pl.BlockSpec((tm, width), lambda i: (i, 0))
    tile = lambda g: jnp.tile(g, GROUP // DA_QK).reshape(1, GROUP)
    return pl.pallas_call(
        _da_prep_body,
        grid=(n // tm,),
        in_specs=[row(A_W), _const_spec((1, GROUP)), _const_spec((1, GROUP)), _const_spec((GROUP, GROUP))],
        out_specs=[row(GROUP)] * 3,
        out_shape=[jax.ShapeDtypeStruct((n, GROUP), BF16)] * 3,
        compiler_params=_params("parallel"),
        name="da_prep",
    )(pa, tile(gq), tile(gk), _group_mean_matrix(GROUP, DA_QK))


def _attn_body(q_ref, k_ref, v_ref, tb_ref, td_ref, lp_ref, go_ref, o_ref, *, lam_init, slopes, blk):
    i = pl.program_id(1)
    lp = lp_ref[...]
    lam = (jnp.exp(jnp.sum(lp[0:1] * lp[1:2], axis=-1, keepdims=True))
           - jnp.exp(jnp.sum(lp[2:3] * lp[3:4], axis=-1, keepdims=True)) + lam_init)
    for h in range(HEADS):
        v_d = v_ref[0, h, i]
        carry = []
        for c in range(2):
            s = _dot(q_ref[0, h, c], k_ref[0, h, c, i]) + td_ref[h]
            m = jnp.max(s, axis=-1, keepdims=True)
            p = jnp.exp(s - m)
            carry += [m, jnp.sum(p, axis=-1, keepdims=True), _dot(p.astype(BF16), v_d)]

        def body(j, carry, h=h):
            off = (i - j).astype(F32) * (slopes[h] * blk)
            v_j = v_ref[0, h, j]
            new = []
            for c in range(2):
                m, l, acc = carry[3 * c:3 * c + 3]
                s = _dot(q_ref[0, h, c], k_ref[0, h, c, j]) + tb_ref[h]
                mx = jnp.maximum(m, jnp.max(s, axis=-1, keepdims=True) - off)
                p = jnp.exp(s - (mx + off))
                alpha = jnp.exp(m - mx)
                new += [mx, alpha * l + jnp.sum(p, axis=-1, keepdims=True),
                        alpha * acc + _dot(p.astype(BF16), v_j)]
            return tuple(new)

        m1, l1, a1, m2, l2, a2 = lax.fori_loop(0, i, body, tuple(carry))
        o = a1 / l1 - lam * (a2 / l2)
        on = o * lax.rsqrt(jnp.mean(o * o, axis=-1, keepdims=True) + RMS_EPS) * go_ref[...] * (1.0 - lam_init)
        o_ref[:, h * HEAD_V:(h + 1) * HEAD_V] = on.astype(BF16)


def _diff_attention(pa, batch, seq, gq, gk, lq1, lk1, lq2, lk2, g_out, lam_init):
    q, k, v = _da_prep(pa, gq, gk)
    blk = min(ATT_BLOCK, seq)
    nb = seq // blk
    q = q.reshape(batch, seq, HEADS, 2, DA_QK).transpose(0, 2, 3, 1, 4)
    k = k.reshape(batch, nb, blk, HEADS, 2, DA_QK).transpose(0, 3, 4, 1, 5, 2)
    v = v.reshape(batch, nb, blk, HEADS, HEAD_V).transpose(0, 3, 1, 2, 4)
    slopes = tuple(2.0 ** (-8.0 * (h + 1) / HEADS) for h in range(HEADS))
    rel = (jnp.arange(blk)[:, None] - jnp.arange(blk)[None, :]).astype(F32)
    tb = jnp.stack([-s * rel for s in slopes])
    td = jnp.stack([jnp.where(rel >= 0, -s * rel, MASK_VALUE) for s in slopes])
    lp = jnp.stack([lq1, lk1, lq2, lk2])
    body = functools.partial(_attn_body, lam_init=lam_init, slopes=slopes, blk=blk)
    return pl.pallas_call(
        body,
        grid=(batch, nb),
        in_specs=[
            pl.BlockSpec((1, HEADS, 2, blk, DA_QK), lambda b, i: (b, 0, 0, i, 0)),
            pl.BlockSpec((1, HEADS, 2, nb, DA_QK, blk), lambda b, i: (b, 0, 0, 0, 0, 0)),
            pl.BlockSpec((1, HEADS, nb, blk, HEAD_V), lambda b, i: (b, 0, 0, 0, 0)),
            _const_spec((HEADS, blk, blk)), _const_spec((HEADS, blk, blk)),
            _const_spec((4, DA_QK)), _const_spec((1, HEAD_V))],
        out_specs=pl.BlockSpec((blk, GROUP), lambda b, i: (b * nb + i, 0)),
        out_shape=jax.ShapeDtypeStruct((batch * seq, GROUP), BF16),
        compiler_params=_params("parallel", "arbitrary"),
        name="diff_attention",
    )(q, k, v, tb, td, lp, g_out.reshape(1, HEAD_V))


def _gla_consts(tb, hk):
    c, nc, kh = GLA_CHUNK, tb // GLA_CHUNK, hk // HEADS
    t = jnp.arange(tb)
    lseg = ((t[:, None] // c == t[None, :] // c) & (t[None, :] <= t[:, None])).astype(BF16)
    r = jnp.arange(c * hk)
    col = jnp.arange(HEADS * c)
    ind = ((r[:, None] // hk == col[None, :] % c)
           & ((r[:, None] % hk) // kh == col[None, :] // c)).astype(BF16)
    cmask_a = (t[:, None] // c == jnp.arange(nc * HEADS * c)[None, :] // (HEADS * c)).astype(F32)
    cmask_q = (t[:, None] // c == jnp.arange(nc * hk)[None, :] // hk).astype(F32)
    vr = jnp.arange(nc * HEADS * c)
    vmask = ((vr[:, None] % (HEADS * c)) // c == jnp.arange(GROUP)[None, :] // HEAD_V).astype(BF16)
    bdmask = (jnp.arange(GROUP)[:, None] // HEAD_V == jnp.arange(hk)[None, :] // kh).astype(F32)
    return lseg, ind, cmask_a, cmask_q, vmask, bdmask


def _gla_core(q, k, lf, v, consts, st_ref, tcat_ref, bk_ref, stack_ref):
    lseg_ref, ind_ref, cmask_a_ref, cmask_q_ref, vmask_ref, bdmask_ref = consts
    tb, hk = q.shape
    c = GLA_CHUNK
    nc = tb // c
    b = _dot_exact_x(lseg_ref[...], lf, 3)
    bk_ref[0] = b
    bk_ref[1] = k
    rowmod = lax.broadcasted_iota(jnp.int32, (tb, hk), 0) % c
    for jj in range(c):
        bj = _chunk_row_bcast(bk_ref, 0, jj, c)
        kj = _chunk_row_bcast(bk_ref, 1, jj, c)
        t = q * kj * jnp.exp(b - bj)
        tcat_ref[:, jj * hk:(jj + 1) * hk] = jnp.where(rowmod >= jj, t, 0.0).astype(BF16)
    a = _dot(tcat_ref[...], ind_ref[...])
    a_exp = (jnp.concatenate([a] * nc, axis=1) * cmask_a_ref[...]).astype(BF16)
    vb = v.astype(BF16)
    v_bd = jnp.concatenate([vb[ci * c:(ci + 1) * c] for ci in range(nc) for _ in range(HEADS)],
                           axis=0) * vmask_ref[...]
    o = _dot(a_exp, v_bd)
    blast = _chunk_row_bcast(bk_ref, 0, c - 1, c)
    qe = q * jnp.exp(b)
    ke = (k * jnp.exp(blast - b)).astype(BF16)
    q_exp = (jnp.concatenate([qe] * nc, axis=1) * cmask_q_ref[...]).astype(BF16)
    st = st_ref[...]
    for ci in range(nc):
        stack_ref[:, ci * hk:(ci + 1) * hk] = st.astype(BF16)
        pt = _dot_tn(vb[ci * c:(ci + 1) * c], ke[ci * c:(ci + 1) * c]) * bdmask_ref[...]
        gam = jnp.exp(bk_ref[0, ci * c + c - 1:ci * c + c, :])
        st = st * gam + pt
    st_ref[...] = st
    return o + _dot_nt(q_exp, stack_ref[...])


def _gla_scratch(tb, hk):
    nc = tb // GLA_CHUNK
    return [pltpu.VMEM((GROUP, hk), F32), pltpu.VMEM((tb, GLA_CHUNK * hk), BF16),
            pltpu.VMEM((2, tb, hk), F32), pltpu.VMEM((GROUP, nc * hk), BF16)]


def _gla_body(pb_ref, gup_ref, gb_ref, gout_ref, g64_ref, *rest):
    consts, (o_ref, st_ref, tcat_ref, bk_ref, stack_ref) = rest[:6], rest[6:]

    @pl.when(pl.program_id(1) == 0)
    def _():
        st_ref[...] = jnp.zeros_like(st_ref)

    hk = HEADS * GLA_K
    q = pb_ref[:, 0:hk] * (GLA_K ** -0.5)
    k = pb_ref[:, hk:2 * hk]
    v = pb_ref[:, 2 * hk:2 * hk + GROUP]
    logit = _dot(pb_ref[:, 512:640].astype(BF16), gup_ref[...]) + gb_ref[...]
    lf = (jnp.minimum(logit, 0.0) - jnp.log(1.0 + jnp.exp(-jnp.abs(logit)))) * (1.0 / GLA_NORMALIZER)
    o = _gla_core(q, k, lf, v, consts, st_ref, tcat_ref, bk_ref, stack_ref)
    ms = _dot_x_exact(o * o, g64_ref[...], 2)
    og = pb_ref[:, 640:896]
    o_ref[...] = (o * lax.rsqrt(ms + RMS_EPS) * gout_ref[...] * _silu(og)).astype(BF16)


def _hgrn_body(pc_ref, lg_ref, gout_ref, g64_ref, *rest, layer):
    consts, (o_ref, st_ref, tcat_ref, bk_ref, stack_ref) = rest[:6], rest[6:]

    @pl.when(pl.program_id(1) == 0)
    def _():
        st_ref[...] = jnp.zeros_like(st_ref)

    lg = lg_ref[...]
    e = jnp.exp(lg - jnp.max(lg, axis=0, keepdims=True))
    p = e / jnp.sum(e, axis=0, keepdims=True)
    cs = p[0:1]
    for i in range(1, layer + 1):
        cs = cs + p[i:i + 1]
    lb = cs - p[0:1]
    q = _silu(pc_ref[:, 0:GROUP])
    z = pc_ref[:, GROUP:2 * GROUP]
    forget = lb + (1.0 - lb) * _sigmoid(z)
    lf = jnp.log(jnp.maximum(forget, TINY))
    k = (1.0 - lb) * _sigmoid(-z)
    v = pc_ref[:, 2 * GROUP:3 * GROUP]
    o = _gla_core(q, k, lf, v, consts, st_ref, tcat_ref, bk_ref, stack_ref)
    ms = _dot_x_exact(o * o, g64_ref[...], 2)
    og = pc_ref[:, 3 * GROUP:4 * GROUP]
    o_ref[...] = (o * lax.rsqrt(ms + RMS_EPS) * gout_ref[...] * _silu(og)).astype(BF16)


def _time_grid_call(body, name, p, batch, seq, extra, scratch):
    tb = min(TIME_BLOCK, seq)
    nt = seq // tb
    width = p.shape[1]
    return pl.pallas_call(
        body,
        grid=(batch, nt),
        in_specs=[pl.BlockSpec((tb, width), lambda b, t: (b * nt + t, 0))]
                 + [_const_spec(e.shape) for e in extra],
        out_specs=pl.BlockSpec((tb, GROUP), lambda b, t: (b * nt + t, 0)),
        out_shape=jax.ShapeDtypeStruct((batch * seq, GROUP), BF16),
        scratch_shapes=scratch,
        compiler_params=_params("parallel", "arbitrary"),
        name=name,
    )(p, *extra)


def _gla(pb, batch, seq, gate_up, gate_b, g_out):
    tb = min(TIME_BLOCK, seq)
    hk = HEADS * GLA_K
    gup = jnp.zeros((LANE, hk), F32).at[:GLA_RANK].set(gate_up).astype(BF16)
    extra = [gup, gate_b.reshape(1, hk), jnp.tile(g_out, HEADS).reshape(1, GROUP),
             _group_mean_matrix(GROUP, HEAD_V), *_gla_consts(tb, hk)]
    return _time_grid_call(_gla_body, "gla", pb, batch, seq, extra, _gla_scratch(tb, hk))


def _hgrn(pc, batch, seq, lb_logits, g_out, layer):
    tb = min(TIME_BLOCK, seq)
    hk = HEADS * HG_K
    extra = [lb_logits, jnp.tile(g_out, HEADS).reshape(1, GROUP),
             _group_mean_matrix(GROUP, HEAD_V), *_gla_consts(tb, hk)]
    return _time_grid_call(functools.partial(_hgrn_body, layer=layer), "hgrn2", pc, batch, seq, extra,
                           _gla_scratch(tb, hk))


def _rwkv_consts(tb):
    c, nch = RW_CHUNK, tb // RW_CHUNK
    t = jnp.arange(tb)
    same = t[:, None] // c == t[None, :] // c
    lseg = (same & (t[None, :] <= t[:, None])).astype(BF16)
    strict = (same & (t[None, :] < t[:, None])).astype(F32)
    incl = (same & (t[None, :] <= t[:, None])).astype(F32)
    eye = jnp.eye(tb, dtype=F32)
    hmask = (jnp.arange(HEADS)[:, None] == jnp.arange(GROUP)[None, :] // HEAD_V).astype(F32)
    bd = (jnp.arange(GROUP)[:, None] // HEAD_V == jnp.arange(GROUP)[None, :] // HEAD_V).astype(F32)
    cmask = (t[:, None] // c == jnp.arange(nch * GROUP)[None, :] // GROUP).astype(F32)
    eye_g = jnp.eye(GROUP, dtype=F32)
    return lseg, strict, incl, eye, hmask, bd, cmask, eye_g


def _rwkv_core(r, logw, k2, v, kk, a, consts, mt_ref, gam_ref, stack_ref):
    lseg_ref, strict_ref, incl_ref, eye_ref, hmask_ref, bd_ref, cmask_ref, eyeg_ref = consts
    tb = r.shape[0]
    c = RW_CHUNK
    nch = tb // c
    gam = _dot_exact_x(lseg_ref[...], logw, 3)
    gam_ref[0] = gam
    glast = _chunk_row_bcast(gam_ref, 0, c - 1, c)
    eng = jnp.exp(-gam)
    ecl = jnp.exp(glast - gam)
    beta = a * kk
    abar = kk * jnp.exp(gam - logw)
    rbar = r * jnp.exp(gam)
    yb = jnp.concatenate([beta * eng, k2 * eng], axis=0).astype(BF16)
    bhat = beta * ecl
    khat = k2 * ecl
    strict, incl = strict_ref[...] > 0.0, incl_ref[...] > 0.0
    eye = eye_ref[...]
    wu = None
    om = None
    arkv = None
    for h in range(HEADS):
        hm = hmask_ref[h:h + 1, :]
        xh = jnp.concatenate([abar * hm, rbar * hm], axis=0).astype(BF16)
        s = _dot_nt(xh, yb)
        a_ab = jnp.where(strict, s[:tb, :tb], 0.0)
        a_ak = jnp.where(strict, s[:tb, tb:], 0.0)
        a_rb = jnp.where(incl, s[tb:, :tb], 0.0)
        a_rk = jnp.where(incl, s[tb:, tb:], 0.0)
        vh = (v * hm).astype(BF16)
        pw = -a_ab
        tinv = eye + pw
        for _ in range(int(math.log2(c)) - 1):
            pw = _mm(pw, pw, 2)
            tinv = tinv + _mm(tinv, pw, 2)
        akv = _dot(a_ak.astype(BF16), vh)
        z = jnp.concatenate([abar * hm, akv], axis=1)
        wu_h = _mm(tinv, z, 2)
        om_h = _dot(a_rb.astype(BF16), wu_h.astype(BF16))
        arkv_h = _dot(a_rk.astype(BF16), vh)
        wu = wu_h if wu is None else wu + wu_h
        om = om_h if om is None else om + om_h
        arkv = arkv_h if arkv is None else arkv + arkv_h
    wbar, u0 = wu[:, :GROUP], wu[:, GROUP:]
    omega = rbar - om[:, :GROUP]
    y0 = arkv - om[:, GROUP:]
    wb, bhb = wbar.astype(BF16), bhat.astype(BF16)
    lhs = jnp.concatenate([v, -u0], axis=1).astype(BF16)
    khb = khat.astype(BF16)
    mt = mt_ref[...]
    for ci in range(nch):
        rows = slice(ci * c, (ci + 1) * c)
        stack_ref[:, ci * GROUP:(ci + 1) * GROUP] = mt.astype(BF16)
        x_c = _dot_tn(wb[rows], bhb[rows]) * bd_ref[...]
        psi = (_dot_tn(lhs[rows, :GROUP], khb[rows]) + _dot_tn(lhs[rows, GROUP:], bhb[rows])) * bd_ref[...]
        gl = jnp.exp(gam_ref[0, ci * c + c - 1:ci * c + c, :])
        mt = mt * gl - _dot(mt.astype(BF16), x_c.astype(BF16)) + psi
    mt_ref[...] = mt
    om_exp = (jnp.concatenate([omega] * nch, axis=1) * cmask_ref[...]).astype(BF16)
    return y0 + _dot_nt(om_exp, stack_ref[...])


def _rwkv_body(pd_ref, mu_ref, w0_ref, wup_ref, a0_ref, aup_ref, gup_ref, kk_ref, ka_ref, rk_ref,
               lng_ref, lnb_ref, g64_ref, s64_ref, *rest):
    consts, (o_ref, mt_ref, carry_ref, gam_ref, stack_ref) = rest[:8], rest[8:]
    tb = pd_ref.shape[0]

    @pl.when(pl.program_id(1) == 0)
    def _():
        mt_ref[...] = jnp.zeros_like(mt_ref)
        carry_ref[...] = jnp.zeros_like(carry_ref)

    p = pd_ref[...]
    rowid = lax.broadcasted_iota(jnp.int32, p.shape, 0)
    prev = jnp.where(rowid == 0, carry_ref[...], pltpu.roll(p, 1, axis=0))
    carry_ref[...] = p[tb - 1:tb, :]
    p = p + mu_ref[...] * (prev - p)
    r, k, v = p[:, 0:GROUP], p[:, GROUP:2 * GROUP], p[:, 2 * GROUP:3 * GROUP]
    low = p[:, 3 * GROUP:3 * GROUP + LANE]
    logw = -RW_DECAY_SCALE * _sigmoid(w0_ref[...] + _dot(jnp.tanh(low).astype(BF16), wup_ref[...]))
    a = _sigmoid(a0_ref[...] + _dot(low.astype(BF16), aup_ref[...]))
    g = _dot(_sigmoid(low).astype(BF16), gup_ref[...])
    kk = k * kk_ref[...]
    kk = kk * lax.rsqrt(jnp.maximum(_dot_x_exact(kk * kk, s64_ref[...], 2), 1e-24))
    k2 = k * (1.0 + (a - 1.0) * ka_ref[...])
    y = _rwkv_core(r, logw, k2, v, kk, a, consts, mt_ref, gam_ref, stack_ref)
    mean = _dot_x_exact(y, g64_ref[...], 2)
    d = y - mean
    var = _dot_x_exact(d * d, g64_ref[...], 2)
    yn = d * lax.rsqrt(var + RW_LN_EPS) * lng_ref[...] + lnb_ref[...]
    bonus = _dot_x_exact(r * k2 * rk_ref[...], s64_ref[...], 2) * v
    o_ref[...] = ((yn + bonus) * g).astype(BF16)


def _rwkv(pd, batch, seq, mu, w0, w_up, a0, a_up, g_up, k_k, k_a, r_k, ln_g, ln_b):
    tb = min(TIME_BLOCK, seq)
    nch = tb // RW_CHUNK
    row = lambda z: z.reshape(1, -1)
    low = lambda w, start: jnp.zeros((LANE, GROUP), F32).at[start:start + w.shape[0]].set(w).astype(BF16)
    extra = [row(mu), row(w0), low(w_up, 0), row(a0), low(a_up, 32), low(g_up, 64), row(k_k), row(k_a),
             row(r_k), row(ln_g), row(ln_b), _group_mean_matrix(GROUP, HEAD_V),
             _group_sum_matrix(GROUP, HEAD_V), *_rwkv_consts(tb)]
    scratch = [pltpu.VMEM((GROUP, GROUP), F32), pltpu.VMEM((1, D_W), F32),
               pltpu.VMEM((1, tb, GROUP), F32), pltpu.VMEM((GROUP, nch * GROUP), BF16)]
    return _time_grid_call(_rwkv_body, "rwkv7", pd, batch, seq, extra, scratch)


def _relayout_w_in(w_in):
    a_end, b_end, c_end = 768, 768 + 784, 768 + 784 + 1024
    wb = w_in[:, a_end:b_end]
    pad = jnp.zeros((w_in.shape[0], LANE - GLA_RANK), w_in.dtype)
    wb = jnp.concatenate([wb[:, :528], pad, wb[:, 528:]], axis=1)
    return jnp.concatenate([w_in[:, :a_end], wb, w_in[:, b_end:c_end], w_in[:, c_end:]], axis=1).astype(BF16)


def kernel(x, norm_mix_g, w_in, da_q_norm_g, da_k_norm_g, da_lambda_q1, da_lambda_k1, da_lambda_q2, da_lambda_k2, da_out_norm_g, gla_gate_up, gla_gate_b, gla_out_norm_g, hgrn_lb_logits, hgrn_out_norm_g, rw_shift_mu, rw_w0, rw_w_up, rw_a0, rw_a_up, rw_g_up, rw_k_k, rw_k_a, rw_r_k, rw_ln_g, rw_ln_b, w_out, norm_mlp_g, w_mlp_up, w_mlp_down):
    batch, seq, _ = x.shape
    xf = x.reshape(batch * seq, D_MODEL)
    for l in range(DEPTH):
        lam_init = 0.8 - 0.6 * math.exp(-0.3 * l)
        pa, pb, pc, pd = _in_proj(xf, norm_mix_g[l], _relayout_w_in(w_in[l]))
        o_a = _diff_attention(pa, batch, seq, da_q_norm_g[l], da_k_norm_g[l], da_lambda_q1[l],
                              da_lambda_k1[l], da_lambda_q2[l], da_lambda_k2[l], da_out_norm_g[l], lam_init)
        o_b = _gla(pb, batch, seq, gla_gate_up[l], gla_gate_b[l], gla_out_norm_g[l])
        o_c = _hgrn(pc, batch, seq, hgrn_lb_logits, hgrn_out_norm_g[l], l)
        o_d = _rwkv(pd, batch, seq, rw_shift_mu[l], rw_w0[l], rw_w_up[l], rw_a0[l], rw_a_up[l], rw_g_up[l],
                    rw_k_k[l], rw_k_a[l], rw_r_k[l].reshape(-1), rw_ln_g[l], rw_ln_b[l])
        xf = _post(xf, (o_a, o_b, o_c, o_d), w_out[l].astype(BF16), norm_mlp_g[l],
                   w_mlp_up[l].astype(BF16), w_mlp_down[l].astype(BF16))
    return xf.reshape(batch, seq, D_MODEL)
```

```python
import functools
import math

import jax
import jax.numpy as jnp
from jax import lax
from jax.experimental import pallas as pl
from jax.experimental.pallas import tpu as pltpu

F32, BF16 = jnp.float32, jnp.bfloat16

D_MODEL = 1024
GROUP = 256
HEADS = 4
HEAD_V = 64
DA_QK = 32
GLA_K = 32
GLA_RANK = 16
GLA_NORMALIZER = 16.0
HG_K = 64
RW_DECAY_SCALE = 0.606531
RW_LN_EPS = 64e-5
D_FF = 4 * D_MODEL
RMS_EPS = 1e-6
MASK_VALUE = -1e30
TINY = 1e-30
DEPTH = 4

LANE = 128
A_W, B_W, C_W, D_W = 768, 896, 1024, 896
P_W = A_W + B_W + C_W + D_W

ROW_TILE = 512
TIME_BLOCK = 256
ATT_BLOCK = 256
ATT_VA = 80
LOG2E = 1.4426950408889634
GLA_CHUNK = 32
RW_CHUNK = 64
VMEM_LIMIT = 56 * 1024 * 1024


def _dot(a, b):
    return jnp.dot(a, b, preferred_element_type=F32)


def _dot_nt(a, b):
    return lax.dot_general(a, b, (((1,), (1,)), ((), ())), preferred_element_type=F32)


def _dot_tn(a, b):
    return lax.dot_general(a, b, (((0,), (0,)), ((), ())), preferred_element_type=F32)


def _split(x, parts):
    out, r = [], x
    for i in range(parts):
        h = r.astype(BF16)
        out.append(h)
        if i + 1 < parts:
            r = r - h.astype(F32)
    return out


def _dot_x_exact(x, m, parts):
    acc = None
    for h in _split(x, parts):
        d = _dot(h, m)
        acc = d if acc is None else acc + d
    return acc


def _dot_exact_x(m, x, parts):
    acc = None
    for h in _split(x, parts):
        d = _dot(m, h)
        acc = d if acc is None else acc + d
    return acc


def _sigmoid(x):
    return 1.0 / (1.0 + jnp.exp(-x))


def _silu(x):
    return x * _sigmoid(x)


def _const_spec(shape):
    zeros = (0,) * len(shape)
    return pl.BlockSpec(shape, lambda *_: zeros)


def _chunk_row_bcast(ref, idx, row, chunk):
    n_rows, width = ref.shape[1], ref.shape[2]
    pieces = []
    for c in range(n_rows // chunk):
        r = c * chunk + row
        pieces.append(jnp.broadcast_to(ref[idx, r:r + 1, :], (chunk, width)))
    return jnp.concatenate(pieces, axis=0)


def _params(*sem):
    return pltpu.CompilerParams(dimension_semantics=sem, vmem_limit_bytes=VMEM_LIMIT)


def _in_proj_body(x_ref, g_ref, w_ref, pa_ref, pb_ref, pc_ref, pd_ref):
    x = x_ref[...]
    h = (x * lax.rsqrt(jnp.mean(x * x, axis=-1, keepdims=True) + RMS_EPS) * g_ref[...]).astype(BF16)
    off = 0
    for o_ref in (pa_ref, pb_ref, pc_ref, pd_ref):
        n = o_ref.shape[1]
        o_ref[...] = _dot(h, w_ref[:, off:off + n])
        off += n


def _in_proj(x, g, w):
    n = x.shape[0]
    tm = min(ROW_TILE, n)
    row = lambda width: pl.BlockSpec((tm, width), lambda i: (i, 0))
    return pl.pallas_call(
        _in_proj_body,
        grid=(n // tm,),
        in_specs=[row(D_MODEL), _const_spec((1, D_MODEL)), _const_spec((D_MODEL, P_W))],
        out_specs=[row(A_W), row(B_W), row(C_W), row(D_W)],
        out_shape=[jax.ShapeDtypeStruct((n, wd), F32) for wd in (A_W, B_W, C_W, D_W)],
        compiler_params=_params("parallel"),
        name="in_proj",
    )(x, g.reshape(1, D_MODEL), w)


def _post_body(x_ref, oa_ref, ob_ref, oc_ref, od_ref, wo_ref, g_ref, wu_ref, wd_ref, out_ref):
    mixed = jnp.concatenate([r[...] for r in (oa_ref, ob_ref, oc_ref, od_ref)], axis=1)
    x1 = x_ref[...] + _dot(mixed, wo_ref[...])
    h = (x1 * lax.rsqrt(jnp.mean(x1 * x1, axis=-1, keepdims=True) + RMS_EPS) * g_ref[...]).astype(BF16)
    y = x1
    for f in range(D_FF // D_MODEL):
        cols = slice(f * D_MODEL, (f + 1) * D_MODEL)
        u = jnp.square(jnp.maximum(_dot(h, wu_ref[:, cols]), 0.0)).astype(BF16)
        y = y + _dot(u, wd_ref[cols, :])
    out_ref[...] = y


def _post(x, outs, w_out, g, w_up, w_down):
    n = x.shape[0]
    tm = min(ROW_TILE, n)
    row = lambda width: pl.BlockSpec((tm, width), lambda i: (i, 0))
    return pl.pallas_call(
        _post_body,
        grid=(n // tm,),
        in_specs=[row(D_MODEL)] + [row(GROUP)] * 4 + [
            _const_spec((D_MODEL, D_MODEL)), _const_spec((1, D_MODEL)),
            _const_spec((D_MODEL, D_FF)), _const_spec((D_FF, D_MODEL))],
        out_specs=row(D_MODEL),
        out_shape=jax.ShapeDtypeStruct((n, D_MODEL), F32),
        compiler_params=_params("parallel"),
        name="post",
    )(x, *outs, w_out, g.reshape(1, D_MODEL), w_up, w_down)


def _group_mean_matrix(width, group):
    i = jnp.arange(width) // group
    return ((i[:, None] == i[None, :]).astype(F32) / group).astype(BF16)


def _group_sum_matrix(width, group):
    i = jnp.arange(width) // group
    return (i[:, None] == i[None, :]).astype(BF16)


def _da_prep_body(pa_ref, gq_ref, gk_ref, g32_ref, q_ref, k_ref, v_ref):
    q = pa_ref[:, 0:GROUP]
    k = pa_ref[:, GROUP:2 * GROUP]
    qm = _dot_x_exact(q * q, g32_ref[...], 2)
    km = _dot_x_exact(k * k, g32_ref[...], 2)
    q_ref[...] = (q * lax.rsqrt(qm + RMS_EPS) * gq_ref[...] * (DA_QK ** -0.5) * LOG2E).astype(BF16)
    k_ref[...] = (k * lax.rsqrt(km + RMS_EPS) * gk_ref[...]).astype(BF16)
    v_ref[...] = pa_ref[:, 2 * GROUP:3 * GROUP].astype(BF16)


def _da_prep(pa, gq, gk):
    n = pa.shape[0]
    tm = min(ROW_TILE, n)
    row = lambda width: pl.BlockSpec((tm, width), lambda i: (i, 0))
    tile = lambda g: jnp.tile(g, GROUP // DA_QK).reshape(1, GROUP)
    return pl.pallas_call(
        _da_prep_body,
        grid=(n // tm,),
        in_specs=[row(A_W), _const_spec((1, GROUP)), _const_spec((1, GROUP)), _const_spec((GROUP, GROUP))],
        out_specs=[row(GROUP)] * 3,
        out_shape=[jax.ShapeDtypeStruct((n, GROUP), BF16)] * 3,
        compiler_params=_params("parallel"),
        name="da_prep",
    )(pa, tile(gq), tile(gk), _group_mean_matrix(GROUP, DA_QK))


def _attn_body(qt_ref, k_ref, vt_ref, tb_ref, td_ref, lp_ref, go_ref, o_ref, m_ref, acc_ref, *,
               lam_init, slopes, blk):
    i = pl.program_id(1)
    chains = [(h, c) for h in range(HEADS) for c in range(2)]

    def step(j, bias_ref, first):
        k_j = [k_ref[0, h, j] for h in range(HEADS)]
        s_all = [_dot(k_j[h], qt_ref[0, h, c, 0]) for h, c in chains]
        p_all, alpha_all = [], []
        for n, (h, c) in enumerate(chains):
            off = (i - j).astype(F32) * (slopes[h] * LOG2E * blk)
            s = s_all[n] + bias_ref[h]
            smax = jnp.max(s, axis=0, keepdims=True) - off
            if first:
                mx = smax
            else:
                m = m_ref[n]
                mx = jnp.maximum(m, smax)
                alpha_all.append(jnp.exp2(m - mx))
            m_ref[n] = mx
            p_all.append(jnp.exp2(s - (mx + off)).astype(BF16))
        for n, (h, c) in enumerate(chains):
            pv = _dot(vt_ref[0, h, j], p_all[n])
            acc_ref[n] = pv if first else alpha_all[n] * acc_ref[n] + pv

    step(i, td_ref, True)

    def body(j, carry):
        step(j, tb_ref, False)
        return carry

    lax.fori_loop(0, i, body, 0)
    lp = lp_ref[...]
    lam = (jnp.exp(jnp.sum(lp[0:1] * lp[1:2], axis=-1, keepdims=True))
           - jnp.exp(jnp.sum(lp[2:3] * lp[3:4], axis=-1, keepdims=True)) + lam_init)
    for h in range(HEADS):
        a1, a2 = acc_ref[2 * h], acc_ref[2 * h + 1]
        o = a1[0:HEAD_V] / a1[HEAD_V:HEAD_V + 1] - lam * (a2[0:HEAD_V] / a2[HEAD_V:HEAD_V + 1])
        on = o * lax.rsqrt(jnp.mean(o * o, axis=0, keepdims=True) + RMS_EPS) * go_ref[...] * (1.0 - lam_init)
        o_ref[0, h * HEAD_V:(h + 1) * HEAD_V, :] = on.astype(BF16)


def _diff_attention(pa, batch, seq, gq, gk, lq1, lk1, lq2, lk2, g_out, lam_init):
    q, k, v = _da_prep(pa, gq, gk)
    blk = min(ATT_BLOCK, seq)
    nb = seq // blk
    slopes = tuple(2.0 ** (-8.0 * (h + 1) / HEADS) for h in range(HEADS))
    kh = k.reshape(batch, nb, blk, HEADS, 2 * DA_QK).transpose(0, 3, 1, 2, 4)
    qh = q.reshape(batch, nb, blk, HEADS, 2, DA_QK).transpose(0, 3, 4, 1, 5, 2)
    zq = jnp.zeros_like(qh[:, :, 0])
    qt = jnp.stack([jnp.concatenate([qh[:, :, 0], zq], -2),
                    jnp.concatenate([zq, qh[:, :, 1]], -2)], axis=2)
    vh = v.reshape(batch, nb, blk, HEADS, HEAD_V).transpose(0, 3, 1, 4, 2)
    vt = jnp.concatenate([vh, jnp.ones((batch, HEADS, nb, 1, blk), BF16),
                          jnp.zeros((batch, HEADS, nb, ATT_VA - HEAD_V - 1, blk), BF16)], -2)
    rel = (jnp.arange(blk)[None, :] - jnp.arange(blk)[:, None]).astype(F32)
    tb = jnp.stack([-(s * LOG2E) * rel for s in slopes])
    td = jnp.where(rel[None] >= 0, tb, MASK_VALUE)
    lp = jnp.stack([lq1, lk1, lq2, lk2])
    go = jnp.broadcast_to(g_out[:, None], (HEAD_V, blk))
    body = functools.partial(_attn_body, lam_init=lam_init, slopes=slopes, blk=blk)
    o_t = pl.pallas_call(
        body,
        grid=(batch, nb),
        in_specs=[
            pl.BlockSpec((1, HEADS, 2, 1, 2 * DA_QK, blk), lambda b, i: (b, 0, 0, i, 0, 0)),
            pl.BlockSpec((1, HEADS, nb, blk, 2 * DA_QK), lambda b, i: (b, 0, 0, 0, 0)),
            pl.BlockSpec((1, HEADS, nb, ATT_VA, blk), lambda b, i: (b, 0, 0, 0, 0)),
            _const_spec((HEADS, blk, blk)), _const_spec((HEADS, blk, blk)),
            _const_spec((4, DA_QK)), _const_spec((HEAD_V, blk))],
        out_specs=pl.BlockSpec((1, GROUP, blk), lambda b, i: (b, 0, i)),
        out_shape=jax.ShapeDtypeStruct((batch, GROUP, seq), BF16),
        scratch_shapes=[pltpu.VMEM((2 * HEADS, 1, blk), F32), pltpu.VMEM((2 * HEADS, ATT_VA, blk), F32)],
        compiler_params=_params("parallel", "arbitrary"),
        name="diff_attention",
    )(qt, kh, vt, tb, td, lp, go)
    return o_t.transpose(0, 2, 1).reshape(batch * seq, GROUP)


def _gla_consts(tb, hk):
    c, nc, kh = GLA_CHUNK, tb // GLA_CHUNK, hk // HEADS
    t = jnp.arange(tb)
    lseg = ((t[:, None] // c == t[None, :] // c) & (t[None, :] <= t[:, None])).astype(BF16)
    r = jnp.arange(c * hk)
    col = jnp.arange(HEADS * c)
    ind = ((r[:, None] // hk == col[None, :] % c)
           & ((r[:, None] % hk) // kh == col[None, :] // c)).astype(BF16)
    cmask_a = (t[:, None] // c == jnp.arange(nc * HEADS * c)[None, :] // (HEADS * c)).astype(F32)
    cmask_q = (t[:, None] // c == jnp.arange(nc * hk)[None, :] // hk).astype(F32)
    vr = jnp.arange(nc * HEADS * c)
    vmask = ((vr[:, None] % (HEADS * c)) // c == jnp.arange(GROUP)[None, :] // HEAD_V).astype(BF16)
    bdmask = (jnp.arange(GROUP)[:, None] // HEAD_V == jnp.arange(hk)[None, :] // kh).astype(F32)
    return lseg, ind, cmask_a, cmask_q, vmask, bdmask


def _gla_core(q, k, lf, v, consts, st_ref, tcat_ref, bk_ref, stack_ref):
    lseg_ref, ind_ref, cmask_a_ref, cmask_q_ref, vmask_ref, bdmask_ref = consts
    tb, hk = q.shape
    c = GLA_CHUNK
    nc = tb // c
    b = _dot_exact_x(lseg_ref[...], lf, 3)
    bk_ref[0] = b
    bk_ref[1] = k
    rowmod = lax.broadcasted_iota(jnp.int32, (tb, hk), 0) % c
    for jj in range(c):
        bj = _chunk_row_bcast(bk_ref, 0, jj, c)
        kj = _chunk_row_bcast(bk_ref, 1, jj, c)
        t = q * kj * jnp.exp(b - bj)
        tcat_ref[:, jj * hk:(jj + 1) * hk] = jnp.where(rowmod >= jj, t, 0.0).astype(BF16)
    a = _dot(tcat_ref[...], ind_ref[...])
    a_exp = (jnp.concatenate([a] * nc, axis=1) * cmask_a_ref[...]).astype(BF16)
    vb = v.astype(BF16)
    v_bd = jnp.concatenate([vb[ci * c:(ci + 1) * c] for ci in range(nc) for _ in range(HEADS)],
                           axis=0) * vmask_ref[...]
    o = _dot(a_exp, v_bd)
    blast = _chunk_row_bcast(bk_ref, 0, c - 1, c)
    qe = q * jnp.exp(b)
    ke = (k * jnp.exp(blast - b)).astype(BF16)
    q_exp = (jnp.concatenate([qe] * nc, axis=1) * cmask_q_ref[...]).astype(BF16)
    st = st_ref[...]
    for ci in range(nc):
        stack_ref[:, ci * hk:(ci + 1) * hk] = st.astype(BF16)
        pt = _dot_tn(vb[ci * c:(ci + 1) * c], ke[ci * c:(ci + 1) * c]) * bdmask_ref[...]
        gam = jnp.exp(bk_ref[0, ci * c + c - 1:ci * c + c, :])
        st = st * gam + pt
    st_ref[...] = st
    return o + _dot_nt(q_exp, stack_ref[...])


def _gla_scratch(tb, hk):
    nc = tb // GLA_CHUNK
    return [pltpu.VMEM((GROUP, hk), F32), pltpu.VMEM((tb, GLA_CHUNK * hk), BF16),
            pltpu.VMEM((2, tb, hk), F32), pltpu.VMEM((GROUP, nc * hk), BF16)]


def _gla_body(pb_ref, gup_ref, gb_ref, gout_ref, g64_ref, *rest):
    consts, (o_ref, st_ref, tcat_ref, bk_ref, stack_ref) = rest[:6], rest[6:]

    @pl.when(pl.program_id(1) == 0)
    def _():
        st_ref[...] = jnp.zeros_like(st_ref)

    hk = HEADS * GLA_K
    q = pb_ref[:, 0:hk] * (GLA_K ** -0.5)
    k = pb_ref[:, hk:2 * hk]
    v = pb_ref[:, 2 * hk:2 * hk + GROUP]
    logit = _dot(pb_ref[:, 512:640].astype(BF16), gup_ref[...]) + gb_ref[...]
    lf = (jnp.minimum(logit, 0.0) - jnp.log(1.0 + jnp.exp(-jnp.abs(logit)))) * (1.0 / GLA_NORMALIZER)
    o = _gla_core(q, k, lf, v, consts, st_ref, tcat_ref, bk_ref, stack_ref)
    ms = _dot_x_exact(o * o, g64_ref[...], 2)
    og = pb_ref[:, 640:896]
    o_ref[...] = (o * lax.rsqrt(ms + RMS_EPS) * gout_ref[...] * _silu(og)).astype(BF16)


def _hgrn_body(pc_ref, lg_ref, gout_ref, g64_ref, *rest, layer):
    consts, (o_ref, st_ref, tcat_ref, bk_ref, stack_ref) = rest[:6], rest[6:]

    @pl.when(pl.program_id(1) == 0)
    def _():
        st_ref[...] = jnp.zeros_like(st_ref)

    lg = lg_ref[...]
    e = jnp.exp(lg - jnp.max(lg, axis=0, keepdims=True))
    p = e / jnp.sum(e, axis=0, keepdims=True)
    cs = p[0:1]
    for i in range(1, layer + 1):
        cs = cs + p[i:i + 1]
    lb = cs - p[0:1]
    q = _silu(pc_ref[:, 0:GROUP])
    z = pc_ref[:, GROUP:2 * GROUP]
    forget = lb + (1.0 - lb) * _sigmoid(z)
    lf = jnp.log(jnp.maximum(forget, TINY))
    k = (1.0 - lb) * _sigmoid(-z)
    v = pc_ref[:, 2 * GROUP:3 * GROUP]
    o = _gla_core(q, k, lf, v, consts, st_ref, tcat_ref, bk_ref, stack_ref)
    ms = _dot_x_exact(o * o, g64_ref[...], 2)
    og = pc_ref[:, 3 * GROUP:4 * GROUP]
    o_ref[...] = (o * lax.rsqrt(ms + RMS_EPS) * gout_ref[...] * _silu(og)).astype(BF16)


def _time_grid_call(body, name, p, batch, seq, extra, scratch):
    tb = min(TIME_BLOCK, seq)
    nt = seq // tb
    width = p.shape[1]
    return pl.pallas_call(
        body,
        grid=(batch, nt),
        in_specs=[pl.BlockSpec((tb, width), lambda b, t: (b * nt + t, 0))]
                 + [_const_spec(e.shape) for e in extra],
        out_specs=pl.BlockSpec((tb, GROUP), lambda b, t: (b * nt + t, 0)),
        out_shape=jax.ShapeDtypeStruct((batch * seq, GROUP), BF16),
        scratch_shapes=scratch,
        compiler_params=_params("parallel", "arbitrary"),
        name=name,
    )(p, *extra)


def _gla(pb, batch, seq, gate_up, gate_b, g_out):
    tb = min(TIME_BLOCK, seq)
    hk = HEADS * GLA_K
    gup = jnp.zeros((LANE, hk), F32).at[:GLA_RANK].set(gate_up).astype(BF16)
    extra = [gup, gate_b.reshape(1, hk), jnp.tile(g_out, HEADS).reshape(1, GROUP),
             _group_mean_matrix(GROUP, HEAD_V), *_gla_consts(tb, hk)]
    return _time_grid_call(_gla_body, "gla", pb, batch, seq, extra, _gla_scratch(tb, hk))


def _hgrn(pc, batch, seq, lb_logits, g_out, layer):
    tb = min(TIME_BLOCK, seq)
    hk = HEADS * HG_K
    extra = [lb_logits, jnp.tile(g_out, HEADS).reshape(1, GROUP),
             _group_mean_matrix(GROUP, HEAD_V), *_gla_consts(tb, hk)]
    return _time_grid_call(functools.partial(_hgrn_body, layer=layer), "hgrn2", pc, batch, seq, extra,
                           _gla_scratch(tb, hk))


def _rwkv_consts(tb):
    c, nch = RW_CHUNK, tb // RW_CHUNK
    t = jnp.arange(tb)
    same = t[:, None] // c == t[None, :] // c
    lseg = (same & (t[None, :] <= t[:, None])).astype(BF16)
    strict = (same & (t[None, :] < t[:, None])).astype(F32)
    incl = (same & (t[None, :] <= t[:, None])).astype(F32)
    eye = jnp.eye(tb, dtype=F32)
    hmask = (jnp.arange(HEADS)[:, None] == jnp.arange(GROUP)[None, :] // HEAD_V).astype(F32)
    bd = (jnp.arange(GROUP)[:, None] // HEAD_V == jnp.arange(GROUP)[None, :] // HEAD_V).astype(F32)
    cmask = (t[:, None] // c == jnp.arange(nch * GROUP)[None, :] // GROUP).astype(F32)
    eye_g = jnp.eye(GROUP, dtype=F32)
    return lseg, strict, incl, eye, hmask, bd, cmask, eye_g


def _rwkv_core(r, logw, k2, v, kk, a, consts, mt_ref, gam_ref, stack_ref):
    lseg_ref, strict_ref, incl_ref, eye_ref, hmask_ref, bd_ref, cmask_ref, eyeg_ref = consts
    tb = r.shape[0]
    c = RW_CHUNK
    nch = tb // c
    gam = _dot_exact_x(lseg_ref[...], logw, 3)
    gam_ref[0] = gam
    glast = _chunk_row_bcast(gam_ref, 0, c - 1, c)
    eng = jnp.exp(-gam)
    ecl = jnp.exp(glast - gam)
    beta = a * kk
    abar = kk * jnp.exp(gam - logw)
    rbar = r * jnp.exp(gam)
    yb = jnp.concatenate([beta * eng, k2 * eng], axis=0).astype(BF16)
    bhat = beta * ecl
    khat = k2 * ecl
    strict, incl = strict_ref[...] > 0.0, incl_ref[...] > 0.0
    eye = eye_ref[...]
    wu = None
    om = None
    arkv = None
    for h in range(HEADS):
        hm = hmask_ref[h:h + 1, :]
        xh = jnp.concatenate([abar * hm, rbar * hm], axis=0).astype(BF16)
        s = _dot_nt(xh, yb)
        a_ab = jnp.where(strict, s[:tb, :tb], 0.0)
        a_ak = jnp.where(strict, s[:tb, tb:], 0.0)
        a_rb = jnp.where(incl, s[tb:, :tb], 0.0)
        a_rk = jnp.where(incl, s[tb:, tb:], 0.0)
        vh = (v * hm).astype(BF16)
        pw = -a_ab
        tinv = eye + pw
        for _ in range(int(math.log2(c)) - 1):
            pwb = pw.astype(BF16)
            pw = _dot(pwb, pwb)
            tinv = tinv + _dot(tinv.astype(BF16), pw.astype(BF16))
        akv = _dot(a_ak.astype(BF16), vh)
        z = jnp.concatenate([abar * hm, akv], axis=1)
        wu_h = _dot(tinv.astype(BF16), z.astype(BF16))
        om_h = _dot(a_rb.astype(BF16), wu_h.astype(BF16))
        arkv_h = _dot(a_rk.astype(BF16), vh)
        wu = wu_h if wu is None else wu + wu_h
        om = om_h if om is None else om + om_h
        arkv = arkv_h if arkv is None else arkv + arkv_h
    wbar, u0 = wu[:, :GROUP], wu[:, GROUP:]
    omega = rbar - om[:, :GROUP]
    y0 = arkv - om[:, GROUP:]
    wb, bhb = wbar.astype(BF16), bhat.astype(BF16)
    vb, nub, khb = v.astype(BF16), (-u0).astype(BF16), khat.astype(BF16)
    mt = mt_ref[...]
    for ci in range(nch):
        rows = slice(ci * c, (ci + 1) * c)
        stack_ref[:, ci * GROUP:(ci + 1) * GROUP] = mt.astype(BF16)
        x_c = _dot_tn(wb[rows], bhb[rows]) * bd_ref[...]
        psi = _dot_tn(jnp.concatenate([vb[rows], nub[rows]], axis=0),
                      jnp.concatenate([khb[rows], bhb[rows]], axis=0)) * bd_ref[...]
        gl = jnp.exp(gam_ref[0, ci * c + c - 1:ci * c + c, :])
        mt = mt * gl - _dot(mt.astype(BF16), x_c.astype(BF16)) + psi
    mt_ref[...] = mt
    om_exp = (jnp.concatenate([omega] * nch, axis=1) * cmask_ref[...]).astype(BF16)
    return y0 + _dot_nt(om_exp, stack_ref[...])


def _rwkv_body(pd_ref, mu_ref, w0_ref, wup_ref, a0_ref, aup_ref, gup_ref, kk_ref, ka_ref, rk_ref,
               lng_ref, lnb_ref, g64_ref, s64_ref, *rest):
    consts, (o_ref, mt_ref, carry_ref, gam_ref, stack_ref) = rest[:8], rest[8:]
    tb = pd_ref.shape[0]

    @pl.when(pl.program_id(1) == 0)
    def _():
        mt_ref[...] = jnp.zeros_like(mt_ref)
        carry_ref[...] = jnp.zeros_like(carry_ref)

    p = pd_ref[...]
    rowid = lax.broadcasted_iota(jnp.int32, p.shape, 0)
    prev = jnp.where(rowid == 0, carry_ref[...], pltpu.roll(p, 1, axis=0))
    carry_ref[...] = p[tb - 1:tb, :]
    p = p + mu_ref[...] * (prev - p)
    r, k, v = p[:, 0:GROUP], p[:, GROUP:2 * GROUP], p[:, 2 * GROUP:3 * GROUP]
    low = p[:, 3 * GROUP:3 * GROUP + LANE]
    logw = -RW_DECAY_SCALE * _sigmoid(w0_ref[...] + _dot(jnp.tanh(low).astype(BF16), wup_ref[...]))
    a = _sigmoid(a0_ref[...] + _dot(low.astype(BF16), aup_ref[...]))
    g = _dot(_sigmoid(low).astype(BF16), gup_ref[...])
    kk = k * kk_ref[...]
    kk = kk * lax.rsqrt(jnp.maximum(_dot_x_exact(kk * kk, s64_ref[...], 2), 1e-24))
    k2 = k * (1.0 + (a - 1.0) * ka_ref[...])
    y = _rwkv_core(r, logw, k2, v, kk, a, consts, mt_ref, gam_ref, stack_ref)
    mean = _dot_x_exact(y, g64_ref[...], 2)
    d = y - mean
    var = _dot_x_exact(d * d, g64_ref[...], 2)
    yn = d * lax.rsqrt(var + RW_LN_EPS) * lng_ref[...] + lnb_ref[...]
    bonus = _dot_x_exact(r * k2 * rk_ref[...], s64_ref[...], 2) * v
    o_ref[...] = ((yn + bonus) * g).astype(BF16)


def _rwkv(pd, batch, seq, mu, w0, w_up, a0, a_up, g_up, k_k, k_a, r_k, ln_g, ln_b):
    tb = min(TIME_BLOCK, seq)
    nch = tb // RW_CHUNK
    row = lambda z: z.reshape(1, -1)
    low = lambda w, start: jnp.zeros((LANE, GROUP), F32).at[start:start + w.shape[0]].set(w).astype(BF16)
    extra = [row(mu), row(w0), low(w_up, 0), row(a0), low(a_up, 32), low(g_up, 64), row(k_k), row(k_a),
             row(r_k), row(ln_g), row(ln_b), _group_mean_matrix(GROUP, HEAD_V),
             _group_sum_matrix(GROUP, HEAD_V), *_rwkv_consts(tb)]
    scratch = [pltpu.VMEM((GROUP, GROUP), F32), pltpu.VMEM((1, D_W), F32),
               pltpu.VMEM((1, tb, GROUP), F32), pltpu.VMEM((GROUP, nch * GROUP), BF16)]
    return _time_grid_call(_rwkv_body, "rwkv7", pd, batch, seq, extra, scratch)


def _relayout_w_in(w_in):
    a_end, b_end, c_end = 768, 768 + 784, 768 + 784 + 1024
    wb = w_in[:, a_end:b_end]
    pad = jnp.zeros((w_in.shape[0], LANE - GLA_RANK), w_in.dtype)
    wb = jnp.concatenate([wb[:, :528], pad, wb[:, 528:]], axis=1)
    return jnp.concatenate([w_in[:, :a_end], wb, w_in[:, b_end:c_end], w_in[:, c_end:]], axis=1).astype(BF16)


def kernel(x, norm_mix_g, w_in, da_q_norm_g, da_k_norm_g, da_lambda_q1, da_lambda_k1, da_lambda_q2, da_lambda_k2, da_out_norm_g, gla_gate_up, gla_gate_b, gla_out_norm_g, hgrn_lb_logits, hgrn_out_norm_g, rw_shift_mu, rw_w0, rw_w_up, rw_a0, rw_a_up, rw_g_up, rw_k_k, rw_k_a, rw_r_k, rw_ln_g, rw_ln_b, w_out, norm_mlp_g, w_mlp_up, w_mlp_down):
    batch, seq, _ = x.shape
    xf = x.reshape(batch * seq, D_MODEL)
    for l in range(DEPTH):
        lam_init = 0.8 - 0.6 * math.exp(-0.3 * l)
        pa, pb, pc, pd = _in_proj(xf, norm_mix_g[l], _relayout_w_in(w_in[l]))
        o_a = _diff_attention(pa, batch, seq, da_q_norm_g[l], da_k_norm_g[l], da_lambda_q1[l],
                              da_lambda_k1[l], da_lambda_q2[l], da_lambda_k2[l], da_out_norm_g[l], lam_init)
        o_b = _gla(pb, batch, seq, gla_gate_up[l], gla_gate_b[l], gla_out_norm_g[l])
        o_c = _hgrn(pc, batch, seq, hgrn_lb_logits, hgrn_out_norm_g[l], l)
        o_d = _rwkv(pd, batch, seq, rw_shift_mu[l], rw_w0[l], rw_w_up[l], rw_a0[l], rw_a_up[l], rw_g_up[l],
                    rw_k_k[l], rw_k_a[l], rw_r_k[l].reshape(-1), rw_ln_g[l], rw_ln_b[l])
        xf = _post(xf, (o_a, o_b, o_c, o_d), w_out[l].astype(BF16), norm_mlp_g[l],
                   w_mlp_up[l].astype(BF16), w_mlp_down[l].astype(BF16))
    return xf.reshape(batch, seq, D_MODEL)
```

```python
import functools
import math

import jax
import jax.numpy as jnp
from jax import lax
from jax.experimental import pallas as pl
from jax.experimental.pallas import tpu as pltpu

F32, BF16 = jnp.float32, jnp.bfloat16

D_MODEL = 1024
GROUP = 256
HEADS = 4
HEAD_V = 64
DA_QK = 32
GLA_K = 32
GLA_RANK = 16
GLA_NORMALIZER = 16.0
HG_K = 64
RW_DECAY_SCALE = 0.606531
RW_LN_EPS = 64e-5
D_FF = 4 * D_MODEL
RMS_EPS = 1e-6
MASK_VALUE = -1e30
TINY = 1e-30
DEPTH = 4

LANE = 128
A_W, B_W, C_W, D_W = 768, 896, 1024, 896
P_W = A_W + B_W + C_W + D_W

ROW_TILE = 512
TIME_BLOCK = 256
ATT_BLOCK = 256
ATT_VA = 80
LOG2E = 1.4426950408889634
GLA_CHUNK = 32
RW_CHUNK = 64
VMEM_LIMIT = 56 * 1024 * 1024


def _dot(a, b):
    return jnp.dot(a, b, preferred_element_type=F32)


def _dot_nt(a, b):
    return lax.dot_general(a, b, (((1,), (1,)), ((), ())), preferred_element_type=F32)


def _dot_tn(a, b):
    return lax.dot_general(a, b, (((0,), (0,)), ((), ())), preferred_element_type=F32)


def _split(x, parts):
    out, r = [], x
    for i in range(parts):
        h = r.astype(BF16)
        out.append(h)
        if i + 1 < parts:
            r = r - h.astype(F32)
    return out


def _dot_x_exact(x, m, parts):
    acc = None
    for h in _split(x, parts):
        d = _dot(h, m)
        acc = d if acc is None else acc + d
    return acc


def _dot_exact_x(m, x, parts):
    acc = None
    for h in _split(x, parts):
        d = _dot(m, h)
        acc = d if acc is None else acc + d
    return acc


def _sigmoid(x):
    return 1.0 / (1.0 + jnp.exp(-x))


def _silu(x):
    return x * _sigmoid(x)


def _const_spec(shape):
    zeros = (0,) * len(shape)
    return pl.BlockSpec(shape, lambda *_: zeros)


def _chunk_row_bcast(ref, idx, row, chunk):
    n_rows, width = ref.shape[1], ref.shape[2]
    pieces = []
    for c in range(n_rows // chunk):
        r = c * chunk + row
        pieces.append(jnp.broadcast_to(ref[idx, r:r + 1, :], (chunk, width)))
    return jnp.concatenate(pieces, axis=0)


def _params(*sem):
    return pltpu.CompilerParams(dimension_semantics=sem, vmem_limit_bytes=VMEM_LIMIT)


def _in_proj_body(x_ref, g_ref, w_ref, pa_ref, pb_ref, pc_ref, pd_ref):
    x = x_ref[...]
    h = (x * lax.rsqrt(jnp.mean(x * x, axis=-1, keepdims=True) + RMS_EPS) * g_ref[...]).astype(BF16)
    off = 0
    for o_ref in (pa_ref, pb_ref, pc_ref, pd_ref):
        n = o_ref.shape[1]
        o_ref[...] = _dot(h, w_ref[:, off:off + n])
        off += n


def _in_proj(x, g, w):
    n = x.shape[0]
    tm = min(ROW_TILE, n)
    row = lambda width: pl.BlockSpec((tm, width), lambda i: (i, 0))
    return pl.pallas_call(
        _in_proj_body,
        grid=(n // tm,),
        in_specs=[row(D_MODEL), _const_spec((1, D_MODEL)), _const_spec((D_MODEL, P_W))],
        out_specs=[row(A_W), row(B_W), row(C_W), row(D_W)],
        out_shape=[jax.ShapeDtypeStruct((n, wd), F32) for wd in (A_W, B_W, C_W, D_W)],
        compiler_params=_params("parallel"),
        name="in_proj",
    )(x, g.reshape(1, D_MODEL), w)


def _post_body(x_ref, oa_ref, ob_ref, oc_ref, od_ref, wo_ref, g_ref, wu_ref, wd_ref, out_ref):
    mixed = jnp.concatenate([r[...] for r in (oa_ref, ob_ref, oc_ref, od_ref)], axis=1)
    x1 = x_ref[...] + _dot(mixed, wo_ref[...])
    h = (x1 * lax.rsqrt(jnp.mean(x1 * x1, axis=-1, keepdims=True) + RMS_EPS) * g_ref[...]).astype(BF16)
    y = x1
    for f in range(D_FF // D_MODEL):
        cols = slice(f * D_MODEL, (f + 1) * D_MODEL)
        u = jnp.square(jnp.maximum(_dot(h, wu_ref[:, cols]), 0.0)).astype(BF16)
        y = y + _dot(u, wd_ref[cols, :])
    out_ref[...] = y


def _post(x, outs, w_out, g, w_up, w_down):
    n = x.shape[0]
    tm = min(ROW_TILE, n)
    row = lambda width: pl.BlockSpec((tm, width), lambda i: (i, 0))
    return pl.pallas_call(
        _post_body,
        grid=(n // tm,),
        in_specs=[row(D_MODEL)] + [row(GROUP)] * 4 + [
            _const_spec((D_MODEL, D_MODEL)), _const_spec((1, D_MODEL)),
            _const_spec((D_MODEL, D_FF)), _const_spec((D_FF, D_MODEL))],
        out_specs=row(D_MODEL),
        out_shape=jax.ShapeDtypeStruct((n, D_MODEL), F32),
        compiler_params=_params("parallel"),
        name="post",
    )(x, *outs, w_out, g.reshape(1, D_MODEL), w_up, w_down)


def _group_mean_matrix(width, group):
    i = jnp.arange(width) // group
    return ((i[:, None] == i[None, :]).astype(F32) / group).astype(BF16)


def _group_sum_matrix(width, group):
    i = jnp.arange(width) // group
    return (i[:, None] == i[None, :]).astype(BF16)


def _da_prep_body(pa_ref, gq_ref, gk_ref, g32_ref, q_ref, k_ref, v_ref):
    q = pa_ref[:, 0:GROUP]
    k = pa_ref[:, GROUP:2 * GROUP]
    qm = _dot_x_exact(q * q, g32_ref[...], 2)
    km = _dot_x_exact(k * k, g32_ref[...], 2)
    q_ref[...] = (q * lax.rsqrt(qm + RMS_EPS) * gq_ref[...] * (DA_QK ** -0.5) * LOG2E).astype(BF16)
    k_ref[...] = (k * lax.rsqrt(km + RMS_EPS) * gk_ref[...]).astype(BF16)
    v_ref[...] = pa_ref[:, 2 * GROUP:3 * GROUP].astype(BF16)


def _da_prep(pa, gq, gk):
    n = pa.shape[0]
    tm = min(ROW_TILE, n)
    row = lambda width: pl.BlockSpec((tm, width), lambda i: (i, 0))
    tile = lambda g: jnp.tile(g, GROUP // DA_QK).reshape(1, GROUP)
    return pl.pallas_call(
        _da_prep_body,
        grid=(n // tm,),
        in_specs=[row(A_W), _const_spec((1, GROUP)), _const_spec((1, GROUP)), _const_spec((GROUP, GROUP))],
        out_specs=[row(GROUP)] * 3,
        out_shape=[jax.ShapeDtypeStruct((n, GROUP), BF16)] * 3,
        compiler_params=_params("parallel"),
        name="da_prep",
    )(pa, tile(gq), tile(gk), _group_mean_matrix(GROUP, DA_QK))


def _attn_body(qt_ref, k_ref, vt_ref, tb_ref, td_ref, lp_ref, go_ref, o_ref, m_ref, acc_ref, s_ref, *,
               lam_init, slopes, blk):
    i = pl.program_id(1)
    chains = [(h, c) for h in range(HEADS) for c in range(2)]

    def stage(cur, cur_slot, bias_ref, first, nxt=None, nxt_slot=None):
        for n, (h, c) in enumerate(chains):
            if nxt is not None:
                s_ref[nxt_slot, n] = _dot(k_ref[0, h, nxt], qt_ref[0, h, c, 0])
            if cur is None:
                continue
            off = (i - cur).astype(F32) * (slopes[h] * LOG2E * blk)
            s = s_ref[cur_slot, n] + bias_ref[h]
            smax = jnp.max(s, axis=0, keepdims=True) - off
            if first:
                mx = smax
            else:
                m = m_ref[n]
                mx = jnp.maximum(m, smax)
            m_ref[n] = mx
            p = jnp.exp2(s - (mx + off)).astype(BF16)
            pv = _dot(vt_ref[0, h, cur], p)
            acc_ref[n] = pv if first else jnp.exp2(m - mx) * acc_ref[n] + pv

    stage(None, None, None, False, nxt=i, nxt_slot=0)
    stage(i, 0, td_ref, True, nxt=0, nxt_slot=1)

    def body(u, carry):
        stage(2 * u, 1, tb_ref, False, nxt=2 * u + 1, nxt_slot=0)
        stage(2 * u + 1, 0, tb_ref, False, nxt=2 * u + 2, nxt_slot=1)
        return carry

    lax.fori_loop(0, (i - 1) // 2, body, 0)
    odd = lax.rem(i, 2) == 1

    @pl.when(jnp.logical_and(jnp.logical_not(odd), i > 0))
    def _():
        stage(i - 2, 1, tb_ref, False, nxt=i - 1, nxt_slot=0)
        stage(i - 1, 0, tb_ref, False)

    @pl.when(odd)
    def _():
        stage(i - 1, 1, tb_ref, False)

    lp = lp_ref[...]
    lam = (jnp.exp(jnp.sum(lp[0:1] * lp[1:2], axis=-1, keepdims=True))
           - jnp.exp(jnp.sum(lp[2:3] * lp[3:4], axis=-1, keepdims=True)) + lam_init)
    for h in range(HEADS):
        a1, a2 = acc_ref[2 * h], acc_ref[2 * h + 1]
        o = a1[0:HEAD_V] / a1[HEAD_V:HEAD_V + 1] - lam * (a2[0:HEAD_V] / a2[HEAD_V:HEAD_V + 1])
        on = o * lax.rsqrt(jnp.mean(o * o, axis=0, keepdims=True) + RMS_EPS) * go_ref[...] * (1.0 - lam_init)
        o_ref[0, h * HEAD_V:(h + 1) * HEAD_V, :] = on.astype(BF16)


def _diff_attention(pa, batch, seq, gq, gk, lq1, lk1, lq2, lk2, g_out, lam_init):
    q, k, v = _da_prep(pa, gq, gk)
    blk = min(ATT_BLOCK, seq)
    nb = seq // blk
    slopes = tuple(2.0 ** (-8.0 * (h + 1) / HEADS) for h in range(HEADS))
    kh = k.reshape(batch, nb, blk, HEADS, 2 * DA_QK).transpose(0, 3, 1, 2, 4)
    qh = q.reshape(batch, nb, blk, HEADS, 2, DA_QK).transpose(0, 3, 4, 1, 5, 2)
    zq = jnp.zeros_like(qh[:, :, 0])
    qt = jnp.stack([jnp.concatenate([qh[:, :, 0], zq], -2),
                    jnp.concatenate([zq, qh[:, :, 1]], -2)], axis=2)
    vh = v.reshape(batch, nb, blk, HEADS, HEAD_V).transpose(0, 3, 1, 4, 2)
    vt = jnp.concatenate([vh, jnp.ones((batch, HEADS, nb, 1, blk), BF16),
                          jnp.zeros((batch, HEADS, nb, ATT_VA - HEAD_V - 1, blk), BF16)], -2)
    rel = (jnp.arange(blk)[None, :] - jnp.arange(blk)[:, None]).astype(F32)
    tb = jnp.stack([-(s * LOG2E) * rel for s in slopes])
    td = jnp.where(rel[None] >= 0, tb, MASK_VALUE)
    lp = jnp.stack([lq1, lk1, lq2, lk2])
    go = jnp.broadcast_to(g_out[:, None], (HEAD_V, blk))
    body = functools.partial(_attn_body, lam_init=lam_init, slopes=slopes, blk=blk)
    o_t = pl.pallas_call(
        body,
        grid=(batch, nb),
        in_specs=[
            pl.BlockSpec((1, HEADS, 2, 1, 2 * DA_QK, blk), lambda b, i: (b, 0, 0, i, 0, 0)),
            pl.BlockSpec((1, HEADS, nb, blk, 2 * DA_QK), lambda b, i: (b, 0, 0, 0, 0)),
            pl.BlockSpec((1, HEADS, nb, ATT_VA, blk), lambda b, i: (b, 0, 0, 0, 0)),
            _const_spec((HEADS, blk, blk)), _const_spec((HEADS, blk, blk)),
            _const_spec((4, DA_QK)), _const_spec((HEAD_V, blk))],
        out_specs=pl.BlockSpec((1, GROUP, blk), lambda b, i: (b, 0, i)),
        out_shape=jax.ShapeDtypeStruct((batch, GROUP, seq), BF16),
        scratch_shapes=[pltpu.VMEM((2 * HEADS, 1, blk), F32), pltpu.VMEM((2 * HEADS, ATT_VA, blk), F32),
                        pltpu.VMEM((2, 2 * HEADS, blk, blk), F32)],
        compiler_params=_params("parallel", "arbitrary"),
        name="diff_attention",
    )(qt, kh, vt, tb, td, lp, go)
    return o_t.transpose(0, 2, 1).reshape(batch * seq, GROUP)


def _gla_consts(tb, hk):
    c, nc, kh = GLA_CHUNK, tb // GLA_CHUNK, hk // HEADS
    t = jnp.arange(tb)
    lseg = ((t[:, None] // c == t[None, :] // c) & (t[None, :] <= t[:, None])).astype(BF16)
    r = jnp.arange(c * hk)
    col = jnp.arange(HEADS * c)
    ind = ((r[:, None] // hk == col[None, :] % c)
           & ((r[:, None] % hk) // kh == col[None, :] // c)).astype(BF16)
    cmask_a = (t[:, None] // c == jnp.arange(nc * HEADS * c)[None, :] // (HEADS * c)).astype(F32)
    cmask_q = (t[:, None] // c == jnp.arange(nc * hk)[None, :] // hk).astype(F32)
    vr = jnp.arange(nc * HEADS * c)
    vmask = ((vr[:, None] % (HEADS * c)) // c == jnp.arange(GROUP)[None, :] // HEAD_V).astype(BF16)
    bdmask = (jnp.arange(GROUP)[:, None] // HEAD_V == jnp.arange(hk)[None, :] // kh).astype(F32)
    return lseg, ind, cmask_a, cmask_q, vmask, bdmask


def _gla_core(q, k, lf, v, consts, st_ref, tcat_ref, bk_ref, stack_ref):
    lseg_ref, ind_ref, cmask_a_ref, cmask_q_ref, vmask_ref, bdmask_ref = consts
    tb, hk = q.shape
    c = GLA_CHUNK
    nc = tb // c
    b = _dot_exact_x(lseg_ref[...], lf, 3)
    bk_ref[0] = b
    bk_ref[1] = k
    rowmod = lax.broadcasted_iota(jnp.int32, (tb, hk), 0) % c
    for jj in range(c):
        bj = _chunk_row_bcast(bk_ref, 0, jj, c)
        kj = _chunk_row_bcast(bk_ref, 1, jj, c)
        t = q * kj * jnp.exp(b - bj)
        tcat_ref[:, jj * hk:(jj + 1) * hk] = jnp.where(rowmod >= jj, t, 0.0).astype(BF16)
    a = _dot(tcat_ref[...], ind_ref[...])
    a_exp = (jnp.concatenate([a] * nc, axis=1) * cmask_a_ref[...]).astype(BF16)
    vb = v.astype(BF16)
    v_bd = jnp.concatenate([vb[ci * c:(ci + 1) * c] for ci in range(nc) for _ in range(HEADS)],
                           axis=0) * vmask_ref[...]
    o = _dot(a_exp, v_bd)
    blast = _chunk_row_bcast(bk_ref, 0, c - 1, c)
    qe = q * jnp.exp(b)
    ke = (k * jnp.exp(blast - b)).astype(BF16)
    q_exp = (jnp.concatenate([qe] * nc, axis=1) * cmask_q_ref[...]).astype(BF16)
    st = st_ref[...]
    for ci in range(nc):
        stack_ref[:, ci * hk:(ci + 1) * hk] = st.astype(BF16)
        pt = _dot_tn(vb[ci * c:(ci + 1) * c], ke[ci * c:(ci + 1) * c]) * bdmask_ref[...]
        gam = jnp.exp(bk_ref[0, ci * c + c - 1:ci * c + c, :])
        st = st * gam + pt
    st_ref[...] = st
    return o + _dot_nt(q_exp, stack_ref[...])


def _gla_scratch(tb, hk):
    nc = tb // GLA_CHUNK
    return [pltpu.VMEM((GROUP, hk), F32), pltpu.VMEM((tb, GLA_CHUNK * hk), BF16),
            pltpu.VMEM((2, tb, hk), F32), pltpu.VMEM((GROUP, nc * hk), BF16)]


def _gla_body(pb_ref, gup_ref, gb_ref, gout_ref, g64_ref, *rest):
    consts, (o_ref, st_ref, tcat_ref, bk_ref, stack_ref) = rest[:6], rest[6:]

    @pl.when(pl.program_id(1) == 0)
    def _():
        st_ref[...] = jnp.zeros_like(st_ref)

    hk = HEADS * GLA_K
    q = pb_ref[:, 0:hk] * (GLA_K ** -0.5)
    k = pb_ref[:, hk:2 * hk]
    v = pb_ref[:, 2 * hk:2 * hk + GROUP]
    logit = _dot(pb_ref[:, 512:640].astype(BF16), gup_ref[...]) + gb_ref[...]
    lf = (jnp.minimum(logit, 0.0) - jnp.log(1.0 + jnp.exp(-jnp.abs(logit)))) * (1.0 / GLA_NORMALIZER)
    o = _gla_core(q, k, lf, v, consts, st_ref, tcat_ref, bk_ref, stack_ref)
    ms = _dot_x_exact(o * o, g64_ref[...], 2)
    og = pb_ref[:, 640:896]
    o_ref[...] = (o * lax.rsqrt(ms + RMS_EPS) * gout_ref[...] * _silu(og)).astype(BF16)


def _hgrn_body(pc_ref, lg_ref, gout_ref, g64_ref, *rest, layer):
    consts, (o_ref, st_ref, tcat_ref, bk_ref, stack_ref) = rest[:6], rest[6:]

    @pl.when(pl.program_id(1) == 0)
    def _():
        st_ref[...] = jnp.zeros_like(st_ref)

    lg = lg_ref[...]
    e = jnp.exp(lg - jnp.max(lg, axis=0, keepdims=True))
    p = e / jnp.sum(e, axis=0, keepdims=True)
    cs = p[0:1]
    for i in range(1, layer + 1):
        cs = cs + p[i:i + 1]
    lb = cs - p[0:1]
    q = _silu(pc_ref[:, 0:GROUP])
    z = pc_ref[:, GROUP:2 * GROUP]
    forget = lb + (1.0 - lb) * _sigmoid(z)
    lf = jnp.log(jnp.maximum(forget, TINY))
    k = (1.0 - lb) * _sigmoid(-z)
    v = pc_ref[:, 2 * GROUP:3 * GROUP]
    o = _gla_core(q, k, lf, v, consts, st_ref, tcat_ref, bk_ref, stack_ref)
    ms = _dot_x_exact(o * o, g64_ref[...], 2)
    og = pc_ref[:, 3 * GROUP:4 * GROUP]
    o_ref[...] = (o * lax.rsqrt(ms + RMS_EPS) * gout_ref[...] * _silu(og)).astype(BF16)


def _time_grid_call(body, name, p, batch, seq, extra, scratch):
    tb = min(TIME_BLOCK, seq)
    nt = seq // tb
    width = p.shape[1]
    return pl.pallas_call(
        body,
        grid=(batch, nt),
        in_specs=[pl.BlockSpec((tb, width), lambda b, t: (b * nt + t, 0))]
                 + [_const_spec(e.shape) for e in extra],
        out_specs=pl.BlockSpec((tb, GROUP), lambda b, t: (b * nt + t, 0)),
        out_shape=jax.ShapeDtypeStruct((batch * seq, GROUP), BF16),
        scratch_shapes=scratch,
        compiler_params=_params("parallel", "arbitrary"),
        name=name,
    )(p, *extra)


def _gla(pb, batch, seq, gate_up, gate_b, g_out):
    tb = min(TIME_BLOCK, seq)
    hk = HEADS * GLA_K
    gup = jnp.zeros((LANE, hk), F32).at[:GLA_RANK].set(gate_up).astype(BF16)
    extra = [gup, gate_b.reshape(1, hk), jnp.tile(g_out, HEADS).reshape(1, GROUP),
             _group_mean_matrix(GROUP, HEAD_V), *_gla_consts(tb, hk)]
    return _time_grid_call(_gla_body, "gla", pb, batch, seq, extra, _gla_scratch(tb, hk))


def _hgrn(pc, batch, seq, lb_logits, g_out, layer):
    tb = min(TIME_BLOCK, seq)
    hk = HEADS * HG_K
    extra = [lb_logits, jnp.tile(g_out, HEADS).reshape(1, GROUP),
             _group_mean_matrix(GROUP, HEAD_V), *_gla_consts(tb, hk)]
    return _time_grid_call(functools.partial(_hgrn_body, layer=layer), "hgrn2", pc, batch, seq, extra,
                           _gla_scratch(tb, hk))


def _rwkv_consts(tb):
    c, nch = RW_CHUNK, tb // RW_CHUNK
    t = jnp.arange(tb)
    same = t[:, None] // c == t[None, :] // c
    lseg = (same & (t[None, :] <= t[:, None])).astype(BF16)
    strict = (same & (t[None, :] < t[:, None])).astype(F32)
    incl = (same & (t[None, :] <= t[:, None])).astype(F32)
    eye = jnp.eye(tb, dtype=F32)
    hmask = (jnp.arange(HEADS)[:, None] == jnp.arange(GROUP)[None, :] // HEAD_V).astype(F32)
    bd = (jnp.arange(GROUP)[:, None] // HEAD_V == jnp.arange(GROUP)[None, :] // HEAD_V).astype(F32)
    cmask = (t[:, None] // c == jnp.arange(nch * GROUP)[None, :] // GROUP).astype(F32)
    eye_g = jnp.eye(GROUP, dtype=F32)
    return lseg, strict, incl, eye, hmask, bd, cmask, eye_g


def _rwkv_core(r, logw, k2, v, kk, a, consts, mt_ref, gam_ref, stack_ref):
    lseg_ref, strict_ref, incl_ref, eye_ref, hmask_ref, bd_ref, cmask_ref, eyeg_ref = consts
    tb = r.shape[0]
    c = RW_CHUNK
    nch = tb // c
    gam = _dot_exact_x(lseg_ref[...], logw, 3)
    gam_ref[0] = gam
    glast = _chunk_row_bcast(gam_ref, 0, c - 1, c)
    eng = jnp.exp(-gam)
    ecl = jnp.exp(glast - gam)
    beta = a * kk
    abar = kk * jnp.exp(gam - logw)
    rbar = r * jnp.exp(gam)
    yb = jnp.concatenate([beta * eng, k2 * eng], axis=0).astype(BF16)
    bhat = beta * ecl
    khat = k2 * ecl
    strict, incl = strict_ref[...] > 0.0, incl_ref[...] > 0.0
    eye = eye_ref[...]
    heads = range(HEADS)
    hms = [hmask_ref[h:h + 1, :] for h in heads]
    abar_h = [abar * hms[h] for h in heads]
    vh = [(v * hms[h]).astype(BF16) for h in heads]
    pw, a_rb, a_rk, z = [], [], [], []
    for h in heads:
        xh = jnp.concatenate([abar_h[h], rbar * hms[h]], axis=0).astype(BF16)
        s = _dot_nt(xh, yb)
        pw.append(jnp.where(strict, -s[:tb, :tb], 0.0))
        a_ak = jnp.where(strict, s[:tb, tb:], 0.0).astype(BF16)
        a_rb.append(jnp.where(incl, s[tb:, :tb], 0.0).astype(BF16))
        a_rk.append(jnp.where(incl, s[tb:, tb:], 0.0).astype(BF16))
        z.append(jnp.concatenate([abar_h[h], _dot(a_ak, vh[h])], axis=1).astype(BF16))
    tinv = [eye + pw[h] for h in heads]
    for _ in range(int(math.log2(c)) - 1):
        pwb = [pw[h].astype(BF16) for h in heads]
        pw = [_dot(pwb[h], pwb[h]) for h in heads]
        tinv = [tinv[h] + _dot(tinv[h].astype(BF16), pw[h].astype(BF16)) for h in heads]
    wu_h = [_dot(tinv[h].astype(BF16), z[h]) for h in heads]
    om_h = [_dot(a_rb[h], wu_h[h].astype(BF16)) for h in heads]
    arkv_h = [_dot(a_rk[h], vh[h]) for h in heads]
    wu, om, arkv = sum(wu_h[1:], wu_h[0]), sum(om_h[1:], om_h[0]), sum(arkv_h[1:], arkv_h[0])
    wbar, u0 = wu[:, :GROUP], wu[:, GROUP:]
    omega = rbar - om[:, :GROUP]
    y0 = arkv - om[:, GROUP:]
    wb, bhb = wbar.astype(BF16), bhat.astype(BF16)
    vb, nub, khb = v.astype(BF16), (-u0).astype(BF16), khat.astype(BF16)
    x_c, psi = [], []
    for ci in range(nch):
        rows = slice(ci * c, (ci + 1) * c)
        x_c.append((_dot_tn(wb[rows], bhb[rows]) * bd_ref[...]).astype(BF16))
        psi.append(_dot_tn(jnp.concatenate([vb[rows], nub[rows]], axis=0),
                           jnp.concatenate([khb[rows], bhb[rows]], axis=0)) * bd_ref[...])
    mt = mt_ref[...]
    for ci in range(nch):
        stack_ref[:, ci * GROUP:(ci + 1) * GROUP] = mt.astype(BF16)
        gl = jnp.exp(gam_ref[0, ci * c + c - 1:ci * c + c, :])
        mt = mt * gl - _dot(mt.astype(BF16), x_c[ci]) + psi[ci]
    mt_ref[...] = mt
    om_exp = (jnp.concatenate([omega] * nch, axis=1) * cmask_ref[...]).astype(BF16)
    return y0 + _dot_nt(om_exp, stack_ref[...])


def _rwkv_body(pd_ref, mu_ref, w0_ref, wup_ref, a0_ref, aup_ref, gup_ref, kk_ref, ka_ref, rk_ref,
               lng_ref, lnb_ref, g64_ref, s64_ref, *rest):
    consts, (o_ref, mt_ref, carry_ref, gam_ref, stack_ref) = rest[:8], rest[8:]
    tb = pd_ref.shape[0]

    @pl.when(pl.program_id(1) == 0)
    def _():
        mt_ref[...] = jnp.zeros_like(mt_ref)
        carry_ref[...] = jnp.zeros_like(carry_ref)

    p = pd_ref[...]
    rowid = lax.broadcasted_iota(jnp.int32, p.shape, 0)
    prev = jnp.where(rowid == 0, carry_ref[...], pltpu.roll(p, 1, axis=0))
    carry_ref[...] = p[tb - 1:tb, :]
    p = p + mu_ref[...] * (prev - p)
    r, k, v = p[:, 0:GROUP], p[:, GROUP:2 * GROUP], p[:, 2 * GROUP:3 * GROUP]
    low = p[:, 3 * GROUP:3 * GROUP + LANE]
    logw = -RW_DECAY_SCALE * _sigmoid(w0_ref[...] + _dot(jnp.tanh(low).astype(BF16), wup_ref[...]))
    a = _sigmoid(a0_ref[...] + _dot(low.astype(BF16), aup_ref[...]))
    g = _dot(_sigmoid(low).astype(BF16), gup_ref[...])
    kk = k * kk_ref[...]
    kk = kk * lax.rsqrt(jnp.maximum(_dot_x_exact(kk * kk, s64_ref[...], 2), 1e-24))
    k2 = k * (1.0 + (a - 1.0) * ka_ref[...])
    y = _rwkv_core(r, logw, k2, v, kk, a, consts, mt_ref, gam_ref, stack_ref)
    mean = _dot_x_exact(y, g64_ref[...], 2)
    d = y - mean
    var = _dot_x_exact(d * d, g64_ref[...], 2)
    yn = d * lax.rsqrt(var + RW_LN_EPS) * lng_ref[...] + lnb_ref[...]
    bonus = _dot_x_exact(r * k2 * rk_ref[...], s64_ref[...], 2) * v
    o_ref[...] = ((yn + bonus) * g).astype(BF16)


def _rwkv(pd, batch, seq, mu, w0, w_up, a0, a_up, g_up, k_k, k_a, r_k, ln_g, ln_b):
    tb = min(TIME_BLOCK, seq)
    nch = tb // RW_CHUNK
    row = lambda z: z.reshape(1, -1)
    low = lambda w, start: jnp.zeros((LANE, GROUP), F32).at[start:start + w.shape[0]].set(w).astype(BF16)
    extra = [row(mu), row(w0), low(w_up, 0), row(a0), low(a_up, 32), low(g_up, 64), row(k_k), row(k_a),
             row(r_k), row(ln_g), row(ln_b), _group_mean_matrix(GROUP, HEAD_V),
             _group_sum_matrix(GROUP, HEAD_V), *_rwkv_consts(tb)]
    scratch = [pltpu.VMEM((GROUP, GROUP), F32), pltpu.VMEM((1, D_W), F32),
               pltpu.VMEM((1, tb, GROUP), F32), pltpu.VMEM((GROUP, nch * GROUP), BF16)]
    return _time_grid_call(_rwkv_body, "rwkv7", pd, batch, seq, extra, scratch)


def _relayout_w_in(w_in):
    a_end, b_end, c_end = 768, 768 + 784, 768 + 784 + 1024
    wb = w_in[:, a_end:b_end]
    pad = jnp.zeros((w_in.shape[0], LANE - GLA_RANK), w_in.dtype)
    wb = jnp.concatenate([wb[:, :528], pad, wb[:, 528:]], axis=1)
    return jnp.concatenate([w_in[:, :a_end], wb, w_in[:, b_end:c_end], w_in[:, c_end:]], axis=1).astype(BF16)


def kernel(x, norm_mix_g, w_in, da_q_norm_g, da_k_norm_g, da_lambda_q1, da_lambda_k1, da_lambda_q2, da_lambda_k2, da_out_norm_g, gla_gate_up, gla_gate_b, gla_out_norm_g, hgrn_lb_logits, hgrn_out_norm_g, rw_shift_mu, rw_w0, rw_w_up, rw_a0, rw_a_up, rw_g_up, rw_k_k, rw_k_a, rw_r_k, rw_ln_g, rw_ln_b, w_out, norm_mlp_g, w_mlp_up, w_mlp_down):
    batch, seq, _ = x.shape
    xf = x.reshape(batch * seq, D_MODEL)
    for l in range(DEPTH):
        lam_init = 0.8 - 0.6 * math.exp(-0.3 * l)
        pa, pb, pc, pd = _in_proj(xf, norm_mix_g[l], _relayout_w_in(w_in[l]))
        o_a = _diff_attention(pa, batch, seq, da_q_norm_g[l], da_k_norm_g[l], da_lambda_q1[l],
                              da_lambda_k1[l], da_lambda_q2[l], da_lambda_k2[l], da_out_norm_g[l], lam_init)
        o_b = _gla(pb, batch, seq, gla_gate_up[l], gla_gate_b[l], gla_out_norm_g[l])
        o_c = _hgrn(pc, batch, seq, hgrn_lb_logits, hgrn_out_norm_g[l], l)
        o_d = _rwkv(pd, batch, seq, rw_shift_mu[l], rw_w0[l], rw_w_up[l], rw_a0[l], rw_a_up[l], rw_g_up[l],
                    rw_k_k[l], rw_k_a[l], rw_r_k[l].reshape(-1), rw_ln_g[l], rw_ln_b[l])
        xf = _post(xf, (o_a, o_b, o_c, o_d), w_out[l].astype(BF16), norm_mlp_g[l],
                   w_mlp_up[l].astype(BF16), w_mlp_down[l].astype(BF16))
    return xf.reshape(batch, seq, D_MODEL)
```

```python
import functools
import math

import jax
import jax.numpy as jnp
from jax import lax
from jax.experimental import pallas as pl
from jax.experimental.pallas import tpu as pltpu

F32, BF16 = jnp.float32, jnp.bfloat16

D_MODEL = 1024
GROUP = 256
HEADS = 4
HEAD_V = 64
DA_QK = 32
GLA_K = 32
GLA_RANK = 16
GLA_NORMALIZER = 16.0
HG_K = 64
RW_DECAY_SCALE = 0.606531
RW_LN_EPS = 64e-5
D_FF = 4 * D_MODEL
RMS_EPS = 1e-6
MASK_VALUE = -1e30
TINY = 1e-30
DEPTH = 4

LANE = 128
A_W, B_W, C_W, D_W = 768, 896, 1024, 896
P_W = A_W + B_W + C_W + D_W

ROW_TILE = 512
TIME_BLOCK = 256
ATT_BLOCK = 256
ATT_VA = 80
LOG2E = 1.4426950408889634
GLA_CHUNK = 32
RW_CHUNK = 64
VMEM_LIMIT = 56 * 1024 * 1024


def _dot(a, b):
    return jnp.dot(a, b, preferred_element_type=F32)


def _dot_nt(a, b):
    return lax.dot_general(a, b, (((1,), (1,)), ((), ())), preferred_element_type=F32)


def _dot_tn(a, b):
    return lax.dot_general(a, b, (((0,), (0,)), ((), ())), preferred_element_type=F32)


def _split(x, parts):
    out, r = [], x
    for i in range(parts):
        h = r.astype(BF16)
        out.append(h)
        if i + 1 < parts:
            r = r - h.astype(F32)
    return out


def _dot_x_exact(x, m, parts):
    acc = None
    for h in _split(x, parts):
        d = _dot(h, m)
        acc = d if acc is None else acc + d
    return acc


def _dot_exact_x(m, x, parts):
    acc = None
    for h in _split(x, parts):
        d = _dot(m, h)
        acc = d if acc is None else acc + d
    return acc


def _sigmoid(x):
    return 1.0 / (1.0 + jnp.exp(-x))


def _silu(x):
    return x * _sigmoid(x)


def _const_spec(shape):
    zeros = (0,) * len(shape)
    return pl.BlockSpec(shape, lambda *_: zeros)


def _chunk_row_bcast(ref, idx, row, chunk):
    n_rows, width = ref.shape[1], ref.shape[2]
    pieces = []
    for c in range(n_rows // chunk):
        r = c * chunk + row
        pieces.append(jnp.broadcast_to(ref[idx, r:r + 1, :], (chunk, width)))
    return jnp.concatenate(pieces, axis=0)


def _params(*sem):
    return pltpu.CompilerParams(dimension_semantics=sem, vmem_limit_bytes=VMEM_LIMIT)


def _in_proj_body(x_ref, g_ref, w_ref, pa_ref, pb_ref, pc_ref, pd_ref):
    x = x_ref[...]
    h = (x * lax.rsqrt(jnp.mean(x * x, axis=-1, keepdims=True) + RMS_EPS) * g_ref[...]).astype(BF16)
    off = 0
    for o_ref in (pa_ref, pb_ref, pc_ref, pd_ref):
        n = o_ref.shape[1]
        o_ref[...] = _dot(h, w_ref[:, off:off + n])
        off += n


def _in_proj(x, g, w):
    n = x.shape[0]
    tm = min(ROW_TILE, n)
    row = lambda width: pl.BlockSpec((tm, width), lambda i: (i, 0))
    return pl.pallas_call(
        _in_proj_body,
        grid=(n // tm,),
        in_specs=[row(D_MODEL), _const_spec((1, D_MODEL)), _const_spec((D_MODEL, P_W))],
        out_specs=[row(A_W), row(B_W), row(C_W), row(D_W)],
        out_shape=[jax.ShapeDtypeStruct((n, wd), F32) for wd in (A_W, B_W, C_W, D_W)],
        compiler_params=_params("parallel"),
        name="in_proj",
    )(x, g.reshape(1, D_MODEL), w)


def _post_body(x_ref, oa_ref, ob_ref, oc_ref, od_ref, wo_ref, g_ref, wu_ref, wd_ref, out_ref):
    mixed = jnp.concatenate([r[...] for r in (oa_ref, ob_ref, oc_ref, od_ref)], axis=1)
    x1 = x_ref[...] + _dot(mixed, wo_ref[...])
    h = (x1 * lax.rsqrt(jnp.mean(x1 * x1, axis=-1, keepdims=True) + RMS_EPS) * g_ref[...]).astype(BF16)
    y = x1
    for f in range(D_FF // D_MODEL):
        cols = slice(f * D_MODEL, (f + 1) * D_MODEL)
        u = jnp.square(jnp.maximum(_dot(h, wu_ref[:, cols]), 0.0)).astype(BF16)
        y = y + _dot(u, wd_ref[cols, :])
    out_ref[...] = y


def _post(x, outs, w_out, g, w_up, w_down):
    n = x.shape[0]
    tm = min(ROW_TILE, n)
    row = lambda width: pl.BlockSpec((tm, width), lambda i: (i, 0))
    return pl.pallas_call(
        _post_body,
        grid=(n // tm,),
        in_specs=[row(D_MODEL)] + [row(GROUP)] * 4 + [
            _const_spec((D_MODEL, D_MODEL)), _const_spec((1, D_MODEL)),
            _const_spec((D_MODEL, D_FF)), _const_spec((D_FF, D_MODEL))],
        out_specs=row(D_MODEL),
        out_shape=jax.ShapeDtypeStruct((n, D_MODEL), F32),
        compiler_params=_params("parallel"),
        name="post",
    )(x, *outs, w_out, g.reshape(1, D_MODEL), w_up, w_down)


def _group_mean_matrix(width, group):
    i = jnp.arange(width) // group
    return ((i[:, None] == i[None, :]).astype(F32) / group).astype(BF16)


def _group_sum_matrix(width, group):
    i = jnp.arange(width) // group
    return (i[:, None] == i[None, :]).astype(BF16)


def _da_prep_body(pa_ref, gq_ref, gk_ref, g32_ref, qt_ref, k_ref, vt_ref):
    tm = pa_ref.shape[0]
    blk = vt_ref.shape[-1]
    q = pa_ref[:, 0:GROUP]
    k = pa_ref[:, GROUP:2 * GROUP]
    qm = _dot_x_exact(q * q, g32_ref[...], 2)
    km = _dot_x_exact(k * k, g32_ref[...], 2)
    q_t = (q * lax.rsqrt(qm + RMS_EPS) * gq_ref[...] * (DA_QK ** -0.5 * LOG2E)).T
    comp = (lax.broadcasted_iota(jnp.int32, q_t.shape, 0) // DA_QK) % 2
    for c in range(2):
        qt_ref[0, c] = jnp.where(comp == c, q_t, 0.0).astype(BF16)
    kn = (k * lax.rsqrt(km + RMS_EPS) * gk_ref[...]).astype(BF16)
    v_t = pa_ref[:, 2 * GROUP:3 * GROUP].T.astype(BF16)
    ones_row = (lax.broadcasted_iota(jnp.int32, (ATT_VA - HEAD_V, blk), 0) == 0).astype(BF16)
    for h in range(HEADS):
        k_ref[0, h] = kn[:, h * HEAD_V:(h + 1) * HEAD_V]
        for sb in range(tm // blk):
            vt_ref[0, h, sb, 0:HEAD_V, :] = v_t[h * HEAD_V:(h + 1) * HEAD_V, sb * blk:(sb + 1) * blk]
            vt_ref[0, h, sb, HEAD_V:ATT_VA, :] = ones_row


def _da_prep(pa, batch, seq, blk, gq, gk):
    tm = min(ROW_TILE, seq)
    nt = seq // tm
    tile = lambda g: jnp.tile(g, GROUP // DA_QK).reshape(1, GROUP)
    return pl.pallas_call(
        _da_prep_body,
        grid=(batch, nt),
        in_specs=[pl.BlockSpec((tm, A_W), lambda b, t: (b * nt + t, 0)),
                  _const_spec((1, GROUP)), _const_spec((1, GROUP)), _const_spec((GROUP, GROUP))],
        out_specs=[pl.BlockSpec((1, 2, GROUP, tm), lambda b, t: (b, 0, 0, t)),
                   pl.BlockSpec((1, HEADS, tm, 2 * DA_QK), lambda b, t: (b, 0, t, 0)),
                   pl.BlockSpec((1, HEADS, tm // blk, ATT_VA, blk), lambda b, t: (b, 0, t, 0, 0))],
        out_shape=[jax.ShapeDtypeStruct((batch, 2, GROUP, seq), BF16),
                   jax.ShapeDtypeStruct((batch, HEADS, seq, 2 * DA_QK), BF16),
                   jax.ShapeDtypeStruct((batch, HEADS, seq // blk, ATT_VA, blk), BF16)],
        compiler_params=_params("parallel", "parallel"),
        name="da_prep",
    )(pa, tile(gq), tile(gk), _group_mean_matrix(GROUP, DA_QK))


def _attn_body(qt_ref, k_ref, vt_ref, tb_ref, td_ref, lp_ref, go_ref, o_ref, m_ref, acc_ref, s_ref, *,
               lam_init, slopes, blk):
    i = pl.program_id(1)
    chains = [(h, c) for h in range(HEADS) for c in range(2)]

    def stage(cur, cur_slot, bias_ref, first, nxt=None, nxt_slot=None):
        for n, (h, c) in enumerate(chains):
            if nxt is not None:
                q_hc = qt_ref[0, c, h * HEAD_V:(h + 1) * HEAD_V, :]
                s_ref[nxt_slot, n] = _dot(k_ref[0, h, nxt], q_hc)
            if cur is None:
                continue
            off = (i - cur).astype(F32) * (slopes[h] * LOG2E * blk)
            s = s_ref[cur_slot, n] + bias_ref[h]
            smax = jnp.max(s, axis=0, keepdims=True) - off
            if first:
                mx = smax
            else:
                m = m_ref[n]
                mx = jnp.maximum(m, smax)
            m_ref[n] = mx
            p = jnp.exp2(s - (mx + off)).astype(BF16)
            pv = _dot(vt_ref[0, h, cur], p)
            acc_ref[n] = pv if first else jnp.exp2(m - mx) * acc_ref[n] + pv

    stage(None, None, None, False, nxt=i, nxt_slot=0)
    stage(i, 0, td_ref, True, nxt=0, nxt_slot=1)

    def body(u, carry):
        stage(2 * u, 1, tb_ref, False, nxt=2 * u + 1, nxt_slot=0)
        stage(2 * u + 1, 0, tb_ref, False, nxt=2 * u + 2, nxt_slot=1)
        return carry

    lax.fori_loop(0, (i - 1) // 2, body, 0)
    odd = lax.rem(i, 2) == 1

    @pl.when(jnp.logical_and(jnp.logical_not(odd), i > 0))
    def _():
        stage(i - 2, 1, tb_ref, False, nxt=i - 1, nxt_slot=0)
        stage(i - 1, 0, tb_ref, False)

    @pl.when(odd)
    def _():
        stage(i - 1, 1, tb_ref, False)

    lp = lp_ref[...]
    lam = (jnp.exp(jnp.sum(lp[0:1] * lp[1:2], axis=-1, keepdims=True))
           - jnp.exp(jnp.sum(lp[2:3] * lp[3:4], axis=-1, keepdims=True)) + lam_init)
    outs = []
    for h in range(HEADS):
        a1, a2 = acc_ref[2 * h], acc_ref[2 * h + 1]
        o = a1[0:HEAD_V] / a1[HEAD_V:HEAD_V + 1] - lam * (a2[0:HEAD_V] / a2[HEAD_V:HEAD_V + 1])
        outs.append(o * lax.rsqrt(jnp.mean(o * o, axis=0, keepdims=True) + RMS_EPS) * go_ref[...]
                    * (1.0 - lam_init))
    o_ref[...] = jnp.concatenate(outs, axis=0).T.astype(BF16)


def _diff_attention(pa, batch, seq, gq, gk, lq1, lk1, lq2, lk2, g_out, lam_init):
    blk = min(ATT_BLOCK, seq)
    nb = seq // blk
    qt, k, vt = _da_prep(pa, batch, seq, blk, gq, gk)
    kh = k.reshape(batch, HEADS, nb, blk, 2 * DA_QK)
    slopes = tuple(2.0 ** (-8.0 * (h + 1) / HEADS) for h in range(HEADS))
    rel = (jnp.arange(blk)[None, :] - jnp.arange(blk)[:, None]).astype(F32)
    tb = jnp.stack([-(s * LOG2E) * rel for s in slopes])
    td = jnp.where(rel[None] >= 0, tb, MASK_VALUE)
    lp = jnp.stack([lq1, lk1, lq2, lk2])
    go = jnp.broadcast_to(g_out[:, None], (HEAD_V, blk))
    body = functools.partial(_attn_body, lam_init=lam_init, slopes=slopes, blk=blk)
    return pl.pallas_call(
        body,
        grid=(batch, nb),
        in_specs=[
            pl.BlockSpec((1, 2, GROUP, blk), lambda b, i: (b, 0, 0, i)),
            pl.BlockSpec((1, HEADS, nb, blk, 2 * DA_QK), lambda b, i: (b, 0, 0, 0, 0)),
            pl.BlockSpec((1, HEADS, nb, ATT_VA, blk), lambda b, i: (b, 0, 0, 0, 0)),
            _const_spec((HEADS, blk, blk)), _const_spec((HEADS, blk, blk)),
            _const_spec((4, DA_QK)), _const_spec((HEAD_V, blk))],
        out_specs=pl.BlockSpec((blk, GROUP), lambda b, i: (b * nb + i, 0)),
        out_shape=jax.ShapeDtypeStruct((batch * seq, GROUP), BF16),
        scratch_shapes=[pltpu.VMEM((2 * HEADS, 1, blk), F32), pltpu.VMEM((2 * HEADS, ATT_VA, blk), F32),
                        pltpu.VMEM((2, 2 * HEADS, blk, blk), F32)],
        compiler_params=_params("parallel", "arbitrary"),
        name="diff_attention",
    )(qt, kh, vt, tb, td, lp, go)


def _gla_consts(tb, hk):
    c, nc, kh = GLA_CHUNK, tb // GLA_CHUNK, hk // HEADS
    t = jnp.arange(tb)
    lseg = ((t[:, None] // c == t[None, :] // c) & (t[None, :] <= t[:, None])).astype(BF16)
    r = jnp.arange(c * hk)
    col = jnp.arange(HEADS * c)
    ind = ((r[:, None] // hk == col[None, :] % c)
           & ((r[:, None] % hk) // kh == col[None, :] // c)).astype(BF16)
    cmask_a = (t[:, None] // c == jnp.arange(nc * HEADS * c)[None, :] // (HEADS * c)).astype(F32)
    cmask_q = (t[:, None] // c == jnp.arange(nc * hk)[None, :] // hk).astype(F32)
    vr = jnp.arange(nc * HEADS * c)
    vmask = ((vr[:, None] % (HEADS * c)) // c == jnp.arange(GROUP)[None, :] // HEAD_V).astype(BF16)
    bdmask = (jnp.arange(GROUP)[:, None] // HEAD_V == jnp.arange(hk)[None, :] // kh).astype(F32)
    return lseg, ind, cmask_a, cmask_q, vmask, bdmask


def _gla_core(q, k, lf, v, consts, st_ref, tcat_ref, bk_ref, stack_ref):
    lseg_ref, ind_ref, cmask_a_ref, cmask_q_ref, vmask_ref, bdmask_ref = consts
    tb, hk = q.shape
    c = GLA_CHUNK
    nc = tb // c
    b = _dot_exact_x(lseg_ref[...], lf, 3) * LOG2E
    bk_ref[0] = b
    bk_ref[1] = k
    rowmod = lax.broadcasted_iota(jnp.int32, (tb, hk), 0) % c
    for jj in range(c):
        bj = _chunk_row_bcast(bk_ref, 0, jj, c)
        kj = _chunk_row_bcast(bk_ref, 1, jj, c)
        t = q * kj * jnp.exp2(b - bj)
        tcat_ref[:, jj * hk:(jj + 1) * hk] = jnp.where(rowmod >= jj, t, 0.0).astype(BF16)
    a = _dot(tcat_ref[...], ind_ref[...])
    a_exp = (jnp.concatenate([a] * nc, axis=1) * cmask_a_ref[...]).astype(BF16)
    vb = v.astype(BF16)
    v_bd = jnp.concatenate([vb[ci * c:(ci + 1) * c] for ci in range(nc) for _ in range(HEADS)],
                           axis=0) * vmask_ref[...]
    o = _dot(a_exp, v_bd)
    blast = _chunk_row_bcast(bk_ref, 0, c - 1, c)
    qe = q * jnp.exp2(b)
    ke = (k * jnp.exp2(blast - b)).astype(BF16)
    q_exp = (jnp.concatenate([qe] * nc, axis=1) * cmask_q_ref[...]).astype(BF16)
    st = st_ref[...]
    for ci in range(nc):
        stack_ref[:, ci * hk:(ci + 1) * hk] = st.astype(BF16)
        pt = _dot_tn(vb[ci * c:(ci + 1) * c], ke[ci * c:(ci + 1) * c]) * bdmask_ref[...]
        gam = jnp.exp2(bk_ref[0, ci * c + c - 1:ci * c + c, :])
        st = st * gam + pt
    st_ref[...] = st
    return o + _dot_nt(q_exp, stack_ref[...])


def _gla_scratch(tb, hk):
    nc = tb // GLA_CHUNK
    return [pltpu.VMEM((GROUP, hk), F32), pltpu.VMEM((tb, GLA_CHUNK * hk), BF16),
            pltpu.VMEM((2, tb, hk), F32), pltpu.VMEM((GROUP, nc * hk), BF16)]


def _gla_body(pb_ref, gup_ref, gb_ref, gout_ref, g64_ref, *rest):
    consts, (o_ref, st_ref, tcat_ref, bk_ref, stack_ref) = rest[:6], rest[6:]

    @pl.when(pl.program_id(1) == 0)
    def _():
        st_ref[...] = jnp.zeros_like(st_ref)

    hk = HEADS * GLA_K
    q = pb_ref[:, 0:hk] * (GLA_K ** -0.5)
    k = pb_ref[:, hk:2 * hk]
    v = pb_ref[:, 2 * hk:2 * hk + GROUP]
    logit = _dot(pb_ref[:, 512:640].astype(BF16), gup_ref[...]) + gb_ref[...]
    lf = (jnp.minimum(logit, 0.0) - jnp.log(1.0 + jnp.exp(-jnp.abs(logit)))) * (1.0 / GLA_NORMALIZER)
    o = _gla_core(q, k, lf, v, consts, st_ref, tcat_ref, bk_ref, stack_ref)
    ms = _dot_x_exact(o * o, g64_ref[...], 2)
    og = pb_ref[:, 640:896]
    o_ref[...] = (o * lax.rsqrt(ms + RMS_EPS) * gout_ref[...] * _silu(og)).astype(BF16)


def _hgrn_body(pc_ref, lg_ref, gout_ref, g64_ref, *rest, layer):
    consts, (o_ref, st_ref, tcat_ref, bk_ref, stack_ref) = rest[:6], rest[6:]

    @pl.when(pl.program_id(1) == 0)
    def _():
        st_ref[...] = jnp.zeros_like(st_ref)

    lg = lg_ref[...]
    e = jnp.exp(lg - jnp.max(lg, axis=0, keepdims=True))
    p = e / jnp.sum(e, axis=0, keepdims=True)
    cs = p[0:1]
    for i in range(1, layer + 1):
        cs = cs + p[i:i + 1]
    lb = cs - p[0:1]
    q = _silu(pc_ref[:, 0:GROUP])
    z = pc_ref[:, GROUP:2 * GROUP]
    forget = lb + (1.0 - lb) * _sigmoid(z)
    lf = jnp.log(jnp.maximum(forget, TINY))
    k = (1.0 - lb) * _sigmoid(-z)
    v = pc_ref[:, 2 * GROUP:3 * GROUP]
    o = _gla_core(q, k, lf, v, consts, st_ref, tcat_ref, bk_ref, stack_ref)
    ms = _dot_x_exact(o * o, g64_ref[...], 2)
    og = pc_ref[:, 3 * GROUP:4 * GROUP]
    o_ref[...] = (o * lax.rsqrt(ms + RMS_EPS) * gout_ref[...] * _silu(og)).astype(BF16)


def _time_grid_call(body, name, p, batch, seq, extra, scratch):
    tb = min(TIME_BLOCK, seq)
    nt = seq // tb
    width = p.shape[1]
    return pl.pallas_call(
        body,
        grid=(batch, nt),
        in_specs=[pl.BlockSpec((tb, width), lambda b, t: (b * nt + t, 0))]
                 + [_const_spec(e.shape) for e in extra],
        out_specs=pl.BlockSpec((tb, GROUP), lambda b, t: (b * nt + t, 0)),
        out_shape=jax.ShapeDtypeStruct((batch * seq, GROUP), BF16),
        scratch_shapes=scratch,
        compiler_params=_params("parallel", "arbitrary"),
        name=name,
    )(p, *extra)


def _gla(pb, batch, seq, gate_up, gate_b, g_out):
    tb = min(TIME_BLOCK, seq)
    hk = HEADS * GLA_K
    gup = jnp.zeros((LANE, hk), F32).at[:GLA_RANK].set(gate_up).astype(BF16)
    extra = [gup, gate_b.reshape(1, hk), jnp.tile(g_out, HEADS).reshape(1, GROUP),
             _group_mean_matrix(GROUP, HEAD_V), *_gla_consts(tb, hk)]
    return _time_grid_call(_gla_body, "gla", pb, batch, seq, extra, _gla_scratch(tb, hk))


def _hgrn(pc, batch, seq, lb_logits, g_out, layer):
    tb = min(TIME_BLOCK, seq)
    hk = HEADS * HG_K
    extra = [lb_logits, jnp.tile(g_out, HEADS).reshape(1, GROUP),
             _group_mean_matrix(GROUP, HEAD_V), *_gla_consts(tb, hk)]
    return _time_grid_call(functools.partial(_hgrn_body, layer=layer), "hgrn2", pc, batch, seq, extra,
                           _gla_scratch(tb, hk))


def _rwkv_consts(tb):
    c, nch = RW_CHUNK, tb // RW_CHUNK
    t = jnp.arange(tb)
    same = t[:, None] // c == t[None, :] // c
    lseg = (same & (t[None, :] <= t[:, None])).astype(BF16)
    strict = (same & (t[None, :] < t[:, None])).astype(F32)
    incl = (same & (t[None, :] <= t[:, None])).astype(F32)
    eye = jnp.eye(tb, dtype=F32)
    hmask = (jnp.arange(HEADS)[:, None] == jnp.arange(GROUP)[None, :] // HEAD_V).astype(F32)
    bd = (jnp.arange(GROUP)[:, None] // HEAD_V == jnp.arange(GROUP)[None, :] // HEAD_V).astype(F32)
    cmask = (t[:, None] // c == jnp.arange(nch * GROUP)[None, :] // GROUP).astype(F32)
    eye_g = jnp.eye(GROUP, dtype=F32)
    return lseg, strict, incl, eye, hmask, bd, cmask, eye_g


def _rwkv_core(r, logw, k2, v, kk, a, consts, mt_ref, gam_ref, stack_ref):
    lseg_ref, strict_ref, incl_ref, eye_ref, hmask_ref, bd_ref, cmask_ref, eyeg_ref = consts
    tb = r.shape[0]
    c = RW_CHUNK
    nch = tb // c
    gam = _dot_exact_x(lseg_ref[...], logw, 3)
    gam_ref[0] = gam
    glast = _chunk_row_bcast(gam_ref, 0, c - 1, c)
    eng = jnp.exp(-gam)
    ecl = jnp.exp(glast - gam)
    beta = a * kk
    abar = kk * jnp.exp(gam - logw)
    rbar = r * jnp.exp(gam)
    yb = jnp.concatenate([beta * eng, k2 * eng], axis=0).astype(BF16)
    bhat = beta * ecl
    khat = k2 * ecl
    strict, incl = strict_ref[...] > 0.0, incl_ref[...] > 0.0
    eye = eye_ref[...]
    heads = range(HEADS)
    hms = [hmask_ref[h:h + 1, :] for h in heads]
    abar_h = [abar * hms[h] for h in heads]
    vh = [(v * hms[h]).astype(BF16) for h in heads]
    pw, a_rb, a_rk, z = [], [], [], []
    for h in heads:
        xh = jnp.concatenate([abar_h[h], rbar * hms[h]], axis=0).astype(BF16)
        s = _dot_nt(xh, yb)
        pw.append(jnp.where(strict, -s[:tb, :tb], 0.0))
        a_ak = jnp.where(strict, s[:tb, tb:], 0.0).astype(BF16)
        a_rb.append(jnp.where(incl, s[tb:, :tb], 0.0).astype(BF16))
        a_rk.append(jnp.where(incl, s[tb:, tb:], 0.0).astype(BF16))
        z.append(jnp.concatenate([abar_h[h], _dot(a_ak, vh[h])], axis=1).astype(BF16))
    tinv = [eye + pw[h] for h in heads]
    for _ in range(int(math.log2(c)) - 1):
        pwb = [pw[h].astype(BF16) for h in heads]
        pw = [_dot(pwb[h], pwb[h]) for h in heads]
        tinv = [tinv[h] + _dot(tinv[h].astype(BF16), pw[h].astype(BF16)) for h in heads]
    wu_h = [_dot(tinv[h].astype(BF16), z[h]) for h in heads]
    om_h = [_dot(a_rb[h], wu_h[h].astype(BF16)) for h in heads]
    arkv_h = [_dot(a_rk[h], vh[h]) for h in heads]
    wu, om, arkv = sum(wu_h[1:], wu_h[0]), sum(om_h[1:], om_h[0]), sum(arkv_h[1:], arkv_h[0])
    wbar, u0 = wu[:, :GROUP], wu[:, GROUP:]
    omega = rbar - om[:, :GROUP]
    y0 = arkv - om[:, GROUP:]
    wb, bhb = wbar.astype(BF16), bhat.astype(BF16)
    vb, nub, khb = v.astype(BF16), (-u0).astype(BF16), khat.astype(BF16)
    x_c, psi = [], []
    for ci in range(nch):
        rows = slice(ci * c, (ci + 1) * c)
        x_c.append((_dot_tn(wb[rows], bhb[rows]) * bd_ref[...]).astype(BF16))
        psi.append(_dot_tn(jnp.concatenate([vb[rows], nub[rows]], axis=0),
                           jnp.concatenate([khb[rows], bhb[rows]], axis=0)) * bd_ref[...])
    mt = mt_ref[...]
    for ci in range(nch):
        stack_ref[:, ci * GROUP:(ci + 1) * GROUP] = mt.astype(BF16)
        gl = jnp.exp(gam_ref[0, ci * c + c - 1:ci * c + c, :])
        mt = mt * gl - _dot(mt.astype(BF16), x_c[ci]) + psi[ci]
    mt_ref[...] = mt
    om_exp = (jnp.concatenate([omega] * nch, axis=1) * cmask_ref[...]).astype(BF16)
    return y0 + _dot_nt(om_exp, stack_ref[...])


def _rwkv_body(pd_ref, mu_ref, w0_ref, wup_ref, a0_ref, aup_ref, gup_ref, kk_ref, ka_ref, rk_ref,
               lng_ref, lnb_ref, g64_ref, s64_ref, *rest):
    consts, (o_ref, mt_ref, carry_ref, gam_ref, stack_ref) = rest[:8], rest[8:]
    tb = pd_ref.shape[0]

    @pl.when(pl.program_id(1) == 0)
    def _():
        mt_ref[...] = jnp.zeros_like(mt_ref)
        carry_ref[...] = jnp.zeros_like(carry_ref)

    p = pd_ref[...]
    rowid = lax.broadcasted_iota(jnp.int32, p.shape, 0)
    prev = jnp.where(rowid == 0, carry_ref[...], pltpu.roll(p, 1, axis=0))
    carry_ref[...] = p[tb - 1:tb, :]
    p = p + mu_ref[...] * (prev - p)
    r, k, v = p[:, 0:GROUP], p[:, GROUP:2 * GROUP], p[:, 2 * GROUP:3 * GROUP]
    low = p[:, 3 * GROUP:3 * GROUP + LANE]
    logw = -RW_DECAY_SCALE * _sigmoid(w0_ref[...] + _dot(jnp.tanh(low).astype(BF16), wup_ref[...]))
    a = _sigmoid(a0_ref[...] + _dot(low.astype(BF16), aup_ref[...]))
    g = _dot(_sigmoid(low).astype(BF16), gup_ref[...])
    kk = k * kk_ref[...]
    kk = kk * lax.rsqrt(jnp.maximum(_dot_x_exact(kk * kk, s64_ref[...], 2), 1e-24))
    k2 = k * (1.0 + (a - 1.0) * ka_ref[...])
    y = _rwkv_core(r, logw, k2, v, kk, a, consts, mt_ref, gam_ref, stack_ref)
    mean = _dot_x_exact(y, g64_ref[...], 2)
    d = y - mean
    var = _dot_x_exact(d * d, g64_ref[...], 2)
    yn = d * lax.rsqrt(var + RW_LN_EPS) * lng_ref[...] + lnb_ref[...]
    bonus = _dot_x_exact(r * k2 * rk_ref[...], s64_ref[...], 2) * v
    o_ref[...] = ((yn + bonus) * g).astype(BF16)


def _rwkv(pd, batch, seq, mu, w0, w_up, a0, a_up, g_up, k_k, k_a, r_k, ln_g, ln_b):
    tb = min(TIME_BLOCK, seq)
    nch = tb // RW_CHUNK
    row = lambda z: z.reshape(1, -1)
    low = lambda w, start: jnp.zeros((LANE, GROUP), F32).at[start:start + w.shape[0]].set(w).astype(BF16)
    extra = [row(mu), row(w0), low(w_up, 0), row(a0), low(a_up, 32), low(g_up, 64), row(k_k), row(k_a),
             row(r_k), row(ln_g), row(ln_b), _group_mean_matrix(GROUP, HEAD_V),
             _group_sum_matrix(GROUP, HEAD_V), *_rwkv_consts(tb)]
    scratch = [pltpu.VMEM((GROUP, GROUP), F32), pltpu.VMEM((1, D_W), F32),
               pltpu.VMEM((1, tb, GROUP), F32), pltpu.VMEM((GROUP, nch * GROUP), BF16)]
    return _time_grid_call(_rwkv_body, "rwkv7", pd, batch, seq, extra, scratch)


def _relayout_w_in(w_in):
    a_end, b_end, c_end = 768, 768 + 784, 768 + 784 + 1024
    wb = w_in[:, a_end:b_end]
    pad = jnp.zeros((w_in.shape[0], LANE - GLA_RANK), w_in.dtype)
    wb = jnp.concatenate([wb[:, :528], pad, wb[:, 528:]], axis=1)
    return jnp.concatenate([w_in[:, :a_end], wb, w_in[:, b_end:c_end], w_in[:, c_end:]], axis=1).astype(BF16)


def kernel(x, norm_mix_g, w_in, da_q_norm_g, da_k_norm_g, da_lambda_q1, da_lambda_k1, da_lambda_q2, da_lambda_k2, da_out_norm_g, gla_gate_up, gla_gate_b, gla_out_norm_g, hgrn_lb_logits, hgrn_out_norm_g, rw_shift_mu, rw_w0, rw_w_up, rw_a0, rw_a_up, rw_g_up, rw_k_k, rw_k_a, rw_r_k, rw_ln_g, rw_ln_b, w_out, norm_mlp_g, w_mlp_up, w_mlp_down):
    batch, seq, _ = x.shape
    xf = x.reshape(batch * seq, D_MODEL)
    for l in range(DEPTH):
        lam_init = 0.8 - 0.6 * math.exp(-0.3 * l)
        pa, pb, pc, pd = _in_proj(xf, norm_mix_g[l], _relayout_w_in(w_in[l]))
        o_a = _diff_attention(pa, batch, seq, da_q_norm_g[l], da_k_norm_g[l], da_lambda_q1[l],
                              da_lambda_k1[l], da_lambda_q2[l], da_lambda_k2[l], da_out_norm_g[l], lam_init)
        o_b = _gla(pb, batch, seq, gla_gate_up[l], gla_gate_b[l], gla_out_norm_g[l])
        o_c = _hgrn(pc, batch, seq, hgrn_lb_logits, hgrn_out_norm_g[l], l)
        o_d = _rwkv(pd, batch, seq, rw_shift_mu[l], rw_w0[l], rw_w_up[l], rw_a0[l], rw_a_up[l], rw_g_up[l],
                    rw_k_k[l], rw_k_a[l], rw_r_k[l].reshape(-1), rw_ln_g[l], rw_ln_b[l])
        xf = _post(xf, (o_a, o_b, o_c, o_d), w_out[l].astype(BF16), norm_mlp_g[l],
                   w_mlp_up[l].astype(BF16), w_mlp_down[l].astype(BF16))
    return xf.reshape(batch, seq, D_MODEL)
```

```python
import functools
import math

import jax
import jax.numpy as jnp
import ml_dtypes
import numpy as np
from jax import lax
from jax.experimental import pallas as pl
from jax.experimental.pallas import tpu as pltpu

F32, BF16 = jnp.float32, jnp.bfloat16

D_MODEL = 1024
GROUP = 256
HEADS = 4
HEAD_V = 64
DA_QK = 32
GLA_K = 32
GLA_RANK = 16
GLA_NORMALIZER = 16.0
HG_K = 64
RW_DECAY_SCALE = 0.606531
RW_LN_EPS = 64e-5
D_FF = 4 * D_MODEL
RMS_EPS = 1e-6
MASK_VALUE = -1e30
TINY = 1e-30
DEPTH = 4

LANE = 128
A_W, B_W, C_W, D_W = 768, 896, 1024, 896
P_W = A_W + B_W + C_W + D_W

ROW_TILE = 512
TIME_BLOCK = 256
ATT_BLOCK = 256
ATT_VA = 80
ATT_KA = 80
LOG2E = 1.4426950408889634
GLA_CHUNK = 32
RW_CHUNK = 64
VMEM_LIMIT = 56 * 1024 * 1024


def _dot(a, b):
    return jnp.dot(a, b, preferred_element_type=F32)


def _dot_nt(a, b):
    return lax.dot_general(a, b, (((1,), (1,)), ((), ())), preferred_element_type=F32)


def _dot_tn(a, b):
    return lax.dot_general(a, b, (((0,), (0,)), ((), ())), preferred_element_type=F32)


def _split(x, parts):
    out, r = [], x
    for i in range(parts):
        h = r.astype(BF16)
        out.append(h)
        if i + 1 < parts:
            r = r - h.astype(F32)
    return out


def _dot_x_exact(x, m, parts):
    acc = None
    for h in _split(x, parts):
        d = _dot(h, m)
        acc = d if acc is None else acc + d
    return acc


def _dot_exact_x(m, x, parts):
    acc = None
    for h in _split(x, parts):
        d = _dot(m, h)
        acc = d if acc is None else acc + d
    return acc


def _sigmoid(x):
    return 1.0 / (1.0 + jnp.exp(-x))


def _silu(x):
    return x * _sigmoid(x)


def _const_spec(shape):
    zeros = (0,) * len(shape)
    return pl.BlockSpec(shape, lambda *_: zeros)


def _chunk_row_bcast(ref, idx, row, chunk):
    n_rows, width = ref.shape[1], ref.shape[2]
    pieces = []
    for c in range(n_rows // chunk):
        r = c * chunk + row
        pieces.append(jnp.broadcast_to(ref[idx, r:r + 1, :], (chunk, width)))
    return jnp.concatenate(pieces, axis=0)


def _params(*sem):
    return pltpu.CompilerParams(dimension_semantics=sem, vmem_limit_bytes=VMEM_LIMIT)


def _in_proj_body(x_ref, g_ref, w_ref, pa_ref, pb_ref, pc_ref, pd_ref):
    x = x_ref[...]
    h = (x * lax.rsqrt(jnp.mean(x * x, axis=-1, keepdims=True) + RMS_EPS) * g_ref[...]).astype(BF16)
    off = 0
    for o_ref in (pa_ref, pb_ref, pc_ref, pd_ref):
        n = o_ref.shape[1]
        o_ref[...] = _dot(h, w_ref[:, off:off + n])
        off += n


def _in_proj(x, g, w):
    n = x.shape[0]
    tm = min(ROW_TILE, n)
    row = lambda width: pl.BlockSpec((tm, width), lambda i: (i, 0))
    return pl.pallas_call(
        _in_proj_body,
        grid=(n // tm,),
        in_specs=[row(D_MODEL), _const_spec((1, D_MODEL)), _const_spec((D_MODEL, P_W))],
        out_specs=[row(A_W), row(B_W), row(C_W), row(D_W)],
        out_shape=[jax.ShapeDtypeStruct((n, wd), F32) for wd in (A_W, B_W, C_W, D_W)],
        compiler_params=_params("parallel"),
        name="in_proj",
    )(x, g.reshape(1, D_MODEL), w)


def _post_body(x_ref, oa_ref, ob_ref, oc_ref, od_ref, wo_ref, g_ref, wu_ref, wd_ref, out_ref):
    mixed = jnp.concatenate([r[...] for r in (oa_ref, ob_ref, oc_ref, od_ref)], axis=1)
    x1 = x_ref[...] + _dot(mixed, wo_ref[...])
    h = (x1 * lax.rsqrt(jnp.mean(x1 * x1, axis=-1, keepdims=True) + RMS_EPS) * g_ref[...]).astype(BF16)
    y = x1
    for f in range(D_FF // D_MODEL):
        cols = slice(f * D_MODEL, (f + 1) * D_MODEL)
        u = jnp.square(jnp.maximum(_dot(h, wu_ref[:, cols]), 0.0)).astype(BF16)
        y = y + _dot(u, wd_ref[cols, :])
    out_ref[...] = y


def _post(x, outs, w_out, g, w_up, w_down):
    n = x.shape[0]
    tm = min(ROW_TILE, n)
    row = lambda width: pl.BlockSpec((tm, width), lambda i: (i, 0))
    return pl.pallas_call(
        _post_body,
        grid=(n // tm,),
        in_specs=[row(D_MODEL)] + [row(GROUP)] * 4 + [
            _const_spec((D_MODEL, D_MODEL)), _const_spec((1, D_MODEL)),
            _const_spec((D_MODEL, D_FF)), _const_spec((D_FF, D_MODEL))],
        out_specs=row(D_MODEL),
        out_shape=jax.ShapeDtypeStruct((n, D_MODEL), F32),
        compiler_params=_params("parallel"),
        name="post",
    )(x, *outs, w_out, g.reshape(1, D_MODEL), w_up, w_down)


def _group_mean_matrix(width, group):
    i = jnp.arange(width) // group
    return ((i[:, None] == i[None, :]).astype(F32) / group).astype(BF16)


def _group_sum_matrix(width, group):
    i = jnp.arange(width) // group
    return (i[:, None] == i[None, :]).astype(BF16)


def _da_prep_body(pa_ref, gq_ref, gk_ref, g32_ref, kpos_ref, qt_ref, k_ref, vt_ref):
    tm = pa_ref.shape[0]
    blk = vt_ref.shape[-1]
    q = pa_ref[:, 0:GROUP]
    k = pa_ref[:, GROUP:2 * GROUP]
    qm = _dot_x_exact(q * q, g32_ref[...], 2)
    km = _dot_x_exact(k * k, g32_ref[...], 2)
    q_t = (q * lax.rsqrt(qm + RMS_EPS) * gq_ref[...] * (DA_QK ** -0.5 * LOG2E)).T
    comp = (lax.broadcasted_iota(jnp.int32, q_t.shape, 0) // DA_QK) % 2
    n_extra = ATT_KA - 2 * DA_QK
    q_extra = (lax.broadcasted_iota(jnp.int32, (n_extra, tm), 0) < 3).astype(BF16)
    for c in range(2):
        q_c = jnp.where(comp == c, q_t, 0.0).astype(BF16)
        for h in range(HEADS):
            qt_ref[0, c, h * ATT_KA:h * ATT_KA + 2 * DA_QK, :] = q_c[h * HEAD_V:(h + 1) * HEAD_V]
            qt_ref[0, c, h * ATT_KA + 2 * DA_QK:(h + 1) * ATT_KA, :] = q_extra
    kn = (k * lax.rsqrt(km + RMS_EPS) * gk_ref[...]).astype(BF16)
    v_t = pa_ref[:, 2 * GROUP:3 * GROUP].T.astype(BF16)
    ones_row = (lax.broadcasted_iota(jnp.int32, (ATT_VA - HEAD_V, blk), 0) == 0).astype(BF16)
    for h in range(HEADS):
        k_ref[0, h, :, 0:2 * DA_QK] = kn[:, h * HEAD_V:(h + 1) * HEAD_V]
        for sb in range(tm // blk):
            k_ref[0, h, sb * blk:(sb + 1) * blk, 2 * DA_QK:ATT_KA] = kpos_ref[h]
        for sb in range(tm // blk):
            vt_ref[0, h, sb, 0:HEAD_V, :] = v_t[h * HEAD_V:(h + 1) * HEAD_V, sb * blk:(sb + 1) * blk]
            vt_ref[0, h, sb, HEAD_V:ATT_VA, :] = ones_row


def _da_prep(pa, batch, seq, blk, gq, gk, kpos):
    tm = min(ROW_TILE, seq)
    nt = seq // tm
    tile = lambda g: jnp.tile(g, GROUP // DA_QK).reshape(1, GROUP)
    return pl.pallas_call(
        _da_prep_body,
        grid=(batch, nt),
        in_specs=[pl.BlockSpec((tm, A_W), lambda b, t: (b * nt + t, 0)),
                  _const_spec((1, GROUP)), _const_spec((1, GROUP)), _const_spec((GROUP, GROUP)),
                  _const_spec(kpos.shape)],
        out_specs=[pl.BlockSpec((1, 2, HEADS * ATT_KA, tm), lambda b, t: (b, 0, 0, t)),
                   pl.BlockSpec((1, HEADS, tm, ATT_KA), lambda b, t: (b, 0, t, 0)),
                   pl.BlockSpec((1, HEADS, tm // blk, ATT_VA, blk), lambda b, t: (b, 0, t, 0, 0))],
        out_shape=[jax.ShapeDtypeStruct((batch, 2, HEADS * ATT_KA, seq), BF16),
                   jax.ShapeDtypeStruct((batch, HEADS, seq, ATT_KA), BF16),
                   jax.ShapeDtypeStruct((batch, HEADS, seq // blk, ATT_VA, blk), BF16)],
        compiler_params=_params("parallel", "parallel"),
        name="da_prep",
    )(pa, tile(gq), tile(gk), _group_mean_matrix(GROUP, DA_QK), kpos)


def _attn_body(qt_ref, k_ref, vt_ref, td_ref, lp_ref, go_ref, o_ref, m_ref, acc_ref, s_ref, *,
               lam_init, slopes, blk):
    i = pl.program_id(1)
    chains = [(h, c) for h in range(HEADS) for c in range(2)]

    def stage(cur, cur_slot, first, nxt=None, nxt_slot=None):
        for n, (h, c) in enumerate(chains):
            if nxt is not None:
                q_hc = qt_ref[0, c, h * ATT_KA:(h + 1) * ATT_KA, :]
                s_ref[nxt_slot, n] = _dot(k_ref[0, h, nxt], q_hc)
            if cur is None:
                continue
            off = -cur.astype(F32) * (slopes[h] * LOG2E * blk)
            s = s_ref[cur_slot, n]
            if first:
                s = s + td_ref[...]
            smax = jnp.max(s, axis=0, keepdims=True) - off
            if first:
                mx = smax
            else:
                m = m_ref[n]
                mx = jnp.maximum(m, smax)
            m_ref[n] = mx
            p = jnp.exp2(s - (mx + off)).astype(BF16)
            pv = _dot(vt_ref[0, h, cur], p)
            acc_ref[n] = pv if first else jnp.exp2(m - mx) * acc_ref[n] + pv

    stage(None, None, False, nxt=i, nxt_slot=0)
    stage(i, 0, True, nxt=0, nxt_slot=1)

    def body(u, carry):
        stage(2 * u, 1, False, nxt=2 * u + 1, nxt_slot=0)
        stage(2 * u + 1, 0, False, nxt=2 * u + 2, nxt_slot=1)
        return carry

    lax.fori_loop(0, (i - 1) // 2, body, 0)
    odd = lax.rem(i, 2) == 1

    @pl.when(jnp.logical_and(jnp.logical_not(odd), i > 0))
    def _():
        stage(i - 2, 1, False, nxt=i - 1, nxt_slot=0)
        stage(i - 1, 0, False)

    @pl.when(odd)
    def _():
        stage(i - 1, 1, False)

    lp = lp_ref[...]
    lam = (jnp.exp(jnp.sum(lp[0:1] * lp[1:2], axis=-1, keepdims=True))
           - jnp.exp(jnp.sum(lp[2:3] * lp[3:4], axis=-1, keepdims=True)) + lam_init)
    outs = []
    for h in range(HEADS):
        a1, a2 = acc_ref[2 * h], acc_ref[2 * h + 1]
        o = a1[0:HEAD_V] / a1[HEAD_V:HEAD_V + 1] - lam * (a2[0:HEAD_V] / a2[HEAD_V:HEAD_V + 1])
        outs.append(o * lax.rsqrt(jnp.mean(o * o, axis=0, keepdims=True) + RMS_EPS) * go_ref[...]
                    * (1.0 - lam_init))
    o_ref[...] = jnp.concatenate(outs, axis=0).T.astype(BF16)


def _diff_attention(pa, batch, seq, gq, gk, lq1, lk1, lq2, lk2, g_out, lam_init):
    blk = min(ATT_BLOCK, seq)
    nb = seq // blk
    slopes = tuple(2.0 ** (-8.0 * (h + 1) / HEADS) for h in range(HEADS))
    kpos = np.zeros((HEADS, blk, ATT_KA - 2 * DA_QK), ml_dtypes.bfloat16)
    for h, s in enumerate(slopes):
        r = np.float32(s * LOG2E) * np.arange(blk, dtype=np.float32)
        for part in range(3):
            kpos[h, :, part] = r.astype(ml_dtypes.bfloat16)
            r = r - kpos[h, :, part].astype(np.float32)
    qt, k, vt = _da_prep(pa, batch, seq, blk, gq, gk, jnp.asarray(kpos))
    kh = k.reshape(batch, HEADS, nb, blk, ATT_KA)
    rel = jnp.arange(blk)[None, :] - jnp.arange(blk)[:, None]
    td = jnp.where(rel >= 0, 0.0, MASK_VALUE).astype(F32)
    lp = jnp.stack([lq1, lk1, lq2, lk2])
    go = jnp.broadcast_to(g_out[:, None], (HEAD_V, blk))
    body = functools.partial(_attn_body, lam_init=lam_init, slopes=slopes, blk=blk)
    return pl.pallas_call(
        body,
        grid=(batch, nb),
        in_specs=[
            pl.BlockSpec((1, 2, HEADS * ATT_KA, blk), lambda b, i: (b, 0, 0, i)),
            pl.BlockSpec((1, HEADS, nb, blk, ATT_KA), lambda b, i: (b, 0, 0, 0, 0)),
            pl.BlockSpec((1, HEADS, nb, ATT_VA, blk), lambda b, i: (b, 0, 0, 0, 0)),
            _const_spec((blk, blk)),
            _const_spec((4, DA_QK)), _const_spec((HEAD_V, blk))],
        out_specs=pl.BlockSpec((blk, GROUP), lambda b, i: (b * nb + i, 0)),
        out_shape=jax.ShapeDtypeStruct((batch * seq, GROUP), BF16),
        scratch_shapes=[pltpu.VMEM((2 * HEADS, 1, blk), F32), pltpu.VMEM((2 * HEADS, ATT_VA, blk), F32),
                        pltpu.VMEM((2, 2 * HEADS, blk, blk), F32)],
        compiler_params=_params("parallel", "arbitrary"),
        name="diff_attention",
    )(qt, kh, vt, td, lp, go)


def _gla_consts(tb, hk):
    c, nc, kh = GLA_CHUNK, tb // GLA_CHUNK, hk // HEADS
    t = jnp.arange(tb)
    lseg = ((t[:, None] // c == t[None, :] // c) & (t[None, :] <= t[:, None])).astype(BF16)
    r = jnp.arange(c * hk)
    col = jnp.arange(HEADS * c)
    ind = ((r[:, None] // hk == col[None, :] % c)
           & ((r[:, None] % hk) // kh == col[None, :] // c)).astype(BF16)
    cmask_a = (t[:, None] // c == jnp.arange(nc * HEADS * c)[None, :] // (HEADS * c)).astype(F32)
    cmask_q = (t[:, None] // c == jnp.arange(nc * hk)[None, :] // hk).astype(F32)
    vr = jnp.arange(nc * HEADS * c)
    vmask = ((vr[:, None] % (HEADS * c)) // c == jnp.arange(GROUP)[None, :] // HEAD_V).astype(BF16)
    bdmask = (jnp.arange(GROUP)[:, None] // HEAD_V == jnp.arange(hk)[None, :] // kh).astype(F32)
    return lseg, ind, cmask_a, cmask_q, vmask, bdmask


def _gla_core(q, k, lf, v, consts, st_ref, tcat_ref, bk_ref, stack_ref):
    lseg_ref, ind_ref, cmask_a_ref, cmask_q_ref, vmask_ref, bdmask_ref = consts
    tb, hk = q.shape
    c = GLA_CHUNK
    nc = tb // c
    b = _dot_exact_x(lseg_ref[...], lf, 3) * LOG2E
    bk_ref[0] = b
    bk_ref[1] = k
    rowmod = lax.broadcasted_iota(jnp.int32, (tb, hk), 0) % c
    for jj in range(c):
        bj = _chunk_row_bcast(bk_ref, 0, jj, c)
        kj = _chunk_row_bcast(bk_ref, 1, jj, c)
        t = q * kj * jnp.exp2(b - bj)
        tcat_ref[:, jj * hk:(jj + 1) * hk] = jnp.where(rowmod >= jj, t, 0.0).astype(BF16)
    a = _dot(tcat_ref[...], ind_ref[...])
    a_exp = (jnp.concatenate([a] * nc, axis=1) * cmask_a_ref[...]).astype(BF16)
    vb = v.astype(BF16)
    v_bd = jnp.concatenate([vb[ci * c:(ci + 1) * c] for ci in range(nc) for _ in range(HEADS)],
                           axis=0) * vmask_ref[...]
    o = _dot(a_exp, v_bd)
    blast = _chunk_row_bcast(bk_ref, 0, c - 1, c)
    qe = q * jnp.exp2(b)
    ke = (k * jnp.exp2(blast - b)).astype(BF16)
    q_exp = (jnp.concatenate([qe] * nc, axis=1) * cmask_q_ref[...]).astype(BF16)
    st = st_ref[...]
    for ci in range(nc):
        stack_ref[:, ci * hk:(ci + 1) * hk] = st.astype(BF16)
        pt = _dot_tn(vb[ci * c:(ci + 1) * c], ke[ci * c:(ci + 1) * c]) * bdmask_ref[...]
        gam = jnp.exp2(bk_ref[0, ci * c + c - 1:ci * c + c, :])
        st = st * gam + pt
    st_ref[...] = st
    return o + _dot_nt(q_exp, stack_ref[...])


def _gla_scratch(tb, hk):
    nc = tb // GLA_CHUNK
    return [pltpu.VMEM((GROUP, hk), F32), pltpu.VMEM((tb, GLA_CHUNK * hk), BF16),
            pltpu.VMEM((2, tb, hk), F32), pltpu.VMEM((GROUP, nc * hk), BF16)]


def _gla_body(pb_ref, gup_ref, gb_ref, gout_ref, g64_ref, *rest):
    consts, (o_ref, st_ref, tcat_ref, bk_ref, stack_ref) = rest[:6], rest[6:]

    @pl.when(pl.program_id(1) == 0)
    def _():
        st_ref[...] = jnp.zeros_like(st_ref)

    hk = HEADS * GLA_K
    q = pb_ref[:, 0:hk] * (GLA_K ** -0.5)
    k = pb_ref[:, hk:2 * hk]
    v = pb_ref[:, 2 * hk:2 * hk + GROUP]
    logit = _dot(pb_ref[:, 512:640].astype(BF16), gup_ref[...]) + gb_ref[...]
    lf = (jnp.minimum(logit, 0.0) - jnp.log(1.0 + jnp.exp(-jnp.abs(logit)))) * (1.0 / GLA_NORMALIZER)
    o = _gla_core(q, k, lf, v, consts, st_ref, tcat_ref, bk_ref, stack_ref)
    ms = _dot_x_exact(o * o, g64_ref[...], 2)
    og = pb_ref[:, 640:896]
    o_ref[...] = (o * lax.rsqrt(ms + RMS_EPS) * gout_ref[...] * _silu(og)).astype(BF16)


def _hgrn_body(pc_ref, lg_ref, gout_ref, g64_ref, *rest, layer):
    consts, (o_ref, st_ref, tcat_ref, bk_ref, stack_ref) = rest[:6], rest[6:]

    @pl.when(pl.program_id(1) == 0)
    def _():
        st_ref[...] = jnp.zeros_like(st_ref)

    lg = lg_ref[...]
    e = jnp.exp(lg - jnp.max(lg, axis=0, keepdims=True))
    p = e / jnp.sum(e, axis=0, keepdims=True)
    cs = p[0:1]
    for i in range(1, layer + 1):
        cs = cs + p[i:i + 1]
    lb = cs - p[0:1]
    q = _silu(pc_ref[:, 0:GROUP])
    z = pc_ref[:, GROUP:2 * GROUP]
    forget = lb + (1.0 - lb) * _sigmoid(z)
    lf = jnp.log(jnp.maximum(forget, TINY))
    k = (1.0 - lb) * _sigmoid(-z)
    v = pc_ref[:, 2 * GROUP:3 * GROUP]
    o = _gla_core(q, k, lf, v, consts, st_ref, tcat_ref, bk_ref, stack_ref)
    ms = _dot_x_exact(o * o, g64_ref[...], 2)
    og = pc_ref[:, 3 * GROUP:4 * GROUP]
    o_ref[...] = (o * lax.rsqrt(ms + RMS_EPS) * gout_ref[...] * _silu(og)).astype(BF16)


def _time_grid_call(body, name, p, batch, seq, extra, scratch):
    tb = min(TIME_BLOCK, seq)
    nt = seq // tb
    width = p.shape[1]
    return pl.pallas_call(
        body,
        grid=(batch, nt),
        in_specs=[pl.BlockSpec((tb, width), lambda b, t: (b * nt + t, 0))]
                 + [_const_spec(e.shape) for e in extra],
        out_specs=pl.BlockSpec((tb, GROUP), lambda b, t: (b * nt + t, 0)),
        out_shape=jax.ShapeDtypeStruct((batch * seq, GROUP), BF16),
        scratch_shapes=scratch,
        compiler_params=_params("parallel", "arbitrary"),
        name=name,
    )(p, *extra)


def _gla(pb, batch, seq, gate_up, gate_b, g_out):
    tb = min(TIME_BLOCK, seq)
    hk = HEADS * GLA_K
    gup = jnp.zeros((LANE, hk), F32).at[:GLA_RANK].set(gate_up).astype(BF16)
    extra = [gup, gate_b.reshape(1, hk), jnp.tile(g_out, HEADS).reshape(1, GROUP),
             _group_mean_matrix(GROUP, HEAD_V), *_gla_consts(tb, hk)]
    return _time_grid_call(_gla_body, "gla", pb, batch, seq, extra, _gla_scratch(tb, hk))


def _hgrn(pc, batch, seq, lb_logits, g_out, layer):
    tb = min(TIME_BLOCK, seq)
    hk = HEADS * HG_K
    extra = [lb_logits, jnp.tile(g_out, HEADS).reshape(1, GROUP),
             _group_mean_matrix(GROUP, HEAD_V), *_gla_consts(tb, hk)]
    return _time_grid_call(functools.partial(_hgrn_body, layer=layer), "hgrn2", pc, batch, seq, extra,
                           _gla_scratch(tb, hk))


def _rwkv_consts(tb):
    c, nch = RW_CHUNK, tb // RW_CHUNK
    t = jnp.arange(tb)
    same = t[:, None] // c == t[None, :] // c
    lseg = (same & (t[None, :] <= t[:, None])).astype(BF16)
    strict = (same & (t[None, :] < t[:, None])).astype(F32)
    incl = (same & (t[None, :] <= t[:, None])).astype(F32)
    eye = jnp.eye(tb, dtype=F32)
    hmask = (jnp.arange(HEADS)[:, None] == jnp.arange(GROUP)[None, :] // HEAD_V).astype(F32)
    bd = (jnp.arange(GROUP)[:, None] // HEAD_V == jnp.arange(GROUP)[None, :] // HEAD_V).astype(F32)
    cmask = (t[:, None] // c == jnp.arange(nch * GROUP)[None, :] // GROUP).astype(F32)
    eye_g = jnp.eye(GROUP, dtype=F32)
    return lseg, strict, incl, eye, hmask, bd, cmask, eye_g


def _rwkv_core(r, logw, k2, v, kk, a, consts, mt_ref, gam_ref, stack_ref):
    lseg_ref, strict_ref, incl_ref, eye_ref, hmask_ref, bd_ref, cmask_ref, eyeg_ref = consts
    tb = r.shape[0]
    c = RW_CHUNK
    nch = tb // c
    gam = _dot_exact_x(lseg_ref[...], logw, 3)
    gam_ref[0] = gam
    glast = _chunk_row_bcast(gam_ref, 0, c - 1, c)
    eng = jnp.exp(-gam)
    ecl = jnp.exp(glast - gam)
    beta = a * kk
    abar = kk * jnp.exp(gam - logw)
    rbar = r * jnp.exp(gam)
    yb = jnp.concatenate([beta * eng, k2 * eng], axis=0).astype(BF16)
    bhat = beta * ecl
    khat = k2 * ecl
    strict, incl = strict_ref[...] > 0.0, incl_ref[...] > 0.0
    eye = eye_ref[...]
    heads = range(HEADS)
    hms = [hmask_ref[h:h + 1, :] for h in heads]
    abar_h = [abar * hms[h] for h in heads]
    vh = [(v * hms[h]).astype(BF16) for h in heads]
    pw, a_rb, a_rk, z = [], [], [], []
    for h in heads:
        xh = jnp.concatenate([abar_h[h], rbar * hms[h]], axis=0).astype(BF16)
        s = _dot_nt(xh, yb)
        pw.append(jnp.where(strict, -s[:tb, :tb], 0.0))
        a_ak = jnp.where(strict, s[:tb, tb:], 0.0).astype(BF16)
        a_rb.append(jnp.where(incl, s[tb:, :tb], 0.0).astype(BF16))
        a_rk.append(jnp.where(incl, s[tb:, tb:], 0.0).astype(BF16))
        z.append(jnp.concatenate([abar_h[h], _dot(a_ak, vh[h])], axis=1).astype(BF16))
    tinv = [eye + pw[h] for h in heads]
    for _ in range(int(math.log2(c)) - 1):
        pwb = [pw[h].astype(BF16) for h in heads]
        pw = [_dot(pwb[h], pwb[h]) for h in heads]
        tinv = [tinv[h] + _dot(tinv[h].astype(BF16), pw[h].astype(BF16)) for h in heads]
    wu_h = [_dot(tinv[h].astype(BF16), z[h]) for h in heads]
    om_h = [_dot(a_rb[h], wu_h[h].astype(BF16)) for h in heads]
    arkv_h = [_dot(a_rk[h], vh[h]) for h in heads]
    wu, om, arkv = sum(wu_h[1:], wu_h[0]), sum(om_h[1:], om_h[0]), sum(arkv_h[1:], arkv_h[0])
    wbar, u0 = wu[:, :GROUP], wu[:, GROUP:]
    omega = rbar - om[:, :GROUP]
    y0 = arkv - om[:, GROUP:]
    wb, bhb = wbar.astype(BF16), bhat.astype(BF16)
    vb, nub, khb = v.astype(BF16), (-u0).astype(BF16), khat.astype(BF16)
    x_c, psi = [], []
    for ci in range(nch):
        rows = slice(ci * c, (ci + 1) * c)
        x_c.append((_dot_tn(wb[rows], bhb[rows]) * bd_ref[...]).astype(BF16))
        psi.append(_dot_tn(jnp.concatenate([vb[rows], nub[rows]], axis=0),
                           jnp.concatenate([khb[rows], bhb[rows]], axis=0)) * bd_ref[...])
    mt = mt_ref[...]
    for ci in range(nch):
        stack_ref[:, ci * GROUP:(ci + 1) * GROUP] = mt.astype(BF16)
        gl = jnp.exp(gam_ref[0, ci * c + c - 1:ci * c + c, :])
        mt = mt * gl - _dot(mt.astype(BF16), x_c[ci]) + psi[ci]
    mt_ref[...] = mt
    om_exp = (jnp.concatenate([omega] * nch, axis=1) * cmask_ref[...]).astype(BF16)
    return y0 + _dot_nt(om_exp, stack_ref[...])


def _rwkv_body(pd_ref, mu_ref, w0_ref, wup_ref, a0_ref, aup_ref, gup_ref, kk_ref, ka_ref, rk_ref,
               lng_ref, lnb_ref, g64_ref, s64_ref, *rest):
    consts, (o_ref, mt_ref, carry_ref, gam_ref, stack_ref) = rest[:8], rest[8:]
    tb = pd_ref.shape[0]

    @pl.when(pl.program_id(1) == 0)
    def _():
        mt_ref[...] = jnp.zeros_like(mt_ref)
        carry_ref[...] = jnp.zeros_like(carry_ref)

    p = pd_ref[...]
    rowid = lax.broadcasted_iota(jnp.int32, p.shape, 0)
    prev = jnp.where(rowid == 0, carry_ref[...], pltpu.roll(p, 1, axis=0))
    carry_ref[...] = p[tb - 1:tb, :]
    p = p + mu_ref[...] * (prev - p)
    r, k, v = p[:, 0:GROUP], p[:, GROUP:2 * GROUP], p[:, 2 * GROUP:3 * GROUP]
    low = p[:, 3 * GROUP:3 * GROUP + LANE]
    logw = -RW_DECAY_SCALE * _sigmoid(w0_ref[...] + _dot(jnp.tanh(low).astype(BF16), wup_ref[...]))
    a = _sigmoid(a0_ref[...] + _dot(low.astype(BF16), aup_ref[...]))
    g = _dot(_sigmoid(low).astype(BF16), gup_ref[...])
    kk = k * kk_ref[...]
    kk = kk * lax.rsqrt(jnp.maximum(_dot_x_exact(kk * kk, s64_ref[...], 2), 1e-24))
    k2 = k * (1.0 + (a - 1.0) * ka_ref[...])
    y = _rwkv_core(r, logw, k2, v, kk, a, consts, mt_ref, gam_ref, stack_ref)
    mean = _dot_x_exact(y, g64_ref[...], 2)
    d = y - mean
    var = _dot_x_exact(d * d, g64_ref[...], 2)
    yn = d * lax.rsqrt(var + RW_LN_EPS) * lng_ref[...] + lnb_ref[...]
    bonus = _dot_x_exact(r * k2 * rk_ref[...], s64_ref[...], 2) * v
    o_ref[...] = ((yn + bonus) * g).astype(BF16)


def _rwkv(pd, batch, seq, mu, w0, w_up, a0, a_up, g_up, k_k, k_a, r_k, ln_g, ln_b):
    tb = min(TIME_BLOCK, seq)
    nch = tb // RW_CHUNK
    row = lambda z: z.reshape(1, -1)
    low = lambda w, start: jnp.zeros((LANE, GROUP), F32).at[start:start + w.shape[0]].set(w).astype(BF16)
    extra = [row(mu), row(w0), low(w_up, 0), row(a0), low(a_up, 32), low(g_up, 64), row(k_k), row(k_a),
             row(r_k), row(ln_g), row(ln_b), _group_mean_matrix(GROUP, HEAD_V),
             _group_sum_matrix(GROUP, HEAD_V), *_rwkv_consts(tb)]
    scratch = [pltpu.VMEM((GROUP, GROUP), F32), pltpu.VMEM((1, D_W), F32),
               pltpu.VMEM((1, tb, GROUP), F32), pltpu.VMEM((GROUP, nch * GROUP), BF16)]
    return _time_grid_call(_rwkv_body, "rwkv7", pd, batch, seq, extra, scratch)


def _relayout_w_in(w_in):
    a_end, b_end, c_end = 768, 768 + 784, 768 + 784 + 1024
    wb = w_in[:, a_end:b_end]
    pad = jnp.zeros((w_in.shape[0], LANE - GLA_RANK), w_in.dtype)
    wb = jnp.concatenate([wb[:, :528], pad, wb[:, 528:]], axis=1)
    return jnp.concatenate([w_in[:, :a_end], wb, w_in[:, b_end:c_end], w_in[:, c_end:]], axis=1).astype(BF16)


def kernel(x, norm_mix_g, w_in, da_q_norm_g, da_k_norm_g, da_lambda_q1, da_lambda_k1, da_lambda_q2, da_lambda_k2, da_out_norm_g, gla_gate_up, gla_gate_b, gla_out_norm_g, hgrn_lb_logits, hgrn_out_norm_g, rw_shift_mu, rw_w0, rw_w_up, rw_a0, rw_a_up, rw_g_up, rw_k_k, rw_k_a, rw_r_k, rw_ln_g, rw_ln_b, w_out, norm_mlp_g, w_mlp_up, w_mlp_down):
    batch, seq, _ = x.shape
    xf = x.reshape(batch * seq, D_MODEL)
    for l in range(DEPTH):
        lam_init = 0.8 - 0.6 * math.exp(-0.3 * l)
        pa, pb, pc, pd = _in_proj(xf, norm_mix_g[l], _relayout_w_in(w_in[l]))
        o_a = _diff_attention(pa, batch, seq, da_q_norm_g[l], da_k_norm_g[l], da_lambda_q1[l],
                              da_lambda_k1[l], da_lambda_q2[l], da_lambda_k2[l], da_out_norm_g[l], lam_init)
        o_b = _gla(pb, batch, seq, gla_gate_up[l], gla_gate_b[l], gla_out_norm_g[l])
        o_c = _hgrn(pc, batch, seq, hgrn_lb_logits, hgrn_out_norm_g[l], l)
        o_d = _rwkv(pd, batch, seq, rw_shift_mu[l], rw_w0[l], rw_w_up[l], rw_a0[l], rw_a_up[l], rw_g_up[l],
                    rw_k_k[l], rw_k_a[l], rw_r_k[l].reshape(-1), rw_ln_g[l], rw_ln_b[l])
        xf = _post(xf, (o_a, o_b, o_c, o_d), w_out[l].astype(BF16), norm_mlp_g[l],
                   w_mlp_up[l].astype(BF16), w_mlp_down[l].astype(BF16))
    return xf.reshape(batch, seq, D_MODEL)
```

```python
import functools
import math

import jax
import jax.numpy as jnp
import ml_dtypes
import numpy as np
from jax import lax
from jax.experimental import pallas as pl
from jax.experimental.pallas import tpu as pltpu

F32, BF16 = jnp.float32, jnp.bfloat16

D_MODEL = 1024
GROUP = 256
HEADS = 4
HEAD_V = 64
DA_QK = 32
GLA_K = 32
GLA_RANK = 16
GLA_NORMALIZER = 16.0
HG_K = 64
RW_DECAY_SCALE = 0.606531
RW_LN_EPS = 64e-5
D_FF = 4 * D_MODEL
RMS_EPS = 1e-6
MASK_VALUE = -1e30
TINY = 1e-30
DEPTH = 4

LANE = 128
A_W, B_W, C_W, D_W = 768, 896, 1024, 896
P_W = A_W + B_W + C_W + D_W

ROW_TILE = 512
TIME_BLOCK = 256
ATT_BLOCK = 256
ATT_VA = 80
ATT_KA = 80
LOG2E = 1.4426950408889634
GLA_CHUNK = 32
GLA_SUB = 16
RW_CHUNK = 64
VMEM_LIMIT = 56 * 1024 * 1024


def _dot(a, b):
    return jnp.dot(a, b, preferred_element_type=F32)


def _dot_nt(a, b):
    return lax.dot_general(a, b, (((1,), (1,)), ((), ())), preferred_element_type=F32)


def _dot_tn(a, b):
    return lax.dot_general(a, b, (((0,), (0,)), ((), ())), preferred_element_type=F32)


def _split(x, parts):
    out, r = [], x
    for i in range(parts):
        h = r.astype(BF16)
        out.append(h)
        if i + 1 < parts:
            r = r - h.astype(F32)
    return out


def _dot_x_exact(x, m, parts):
    acc = None
    for h in _split(x, parts):
        d = _dot(h, m)
        acc = d if acc is None else acc + d
    return acc


def _dot_exact_x(m, x, parts):
    acc = None
    for h in _split(x, parts):
        d = _dot(m, h)
        acc = d if acc is None else acc + d
    return acc


def _sigmoid(x):
    return 1.0 / (1.0 + jnp.exp(-x))


def _silu(x):
    return x * _sigmoid(x)


def _const_spec(shape):
    zeros = (0,) * len(shape)
    return pl.BlockSpec(shape, lambda *_: zeros)


def _chunk_row_bcast(ref, idx, row, chunk):
    n_rows, width = ref.shape[1], ref.shape[2]
    pieces = []
    for c in range(n_rows // chunk):
        r = c * chunk + row
        pieces.append(jnp.broadcast_to(ref[idx, r:r + 1, :], (chunk, width)))
    return jnp.concatenate(pieces, axis=0)


def _params(*sem):
    return pltpu.CompilerParams(dimension_semantics=sem, vmem_limit_bytes=VMEM_LIMIT)


def _in_proj_body(x_ref, g_ref, w_ref, pa_ref, pb_ref, pc_ref, pd_ref):
    x = x_ref[...]
    h = (x * lax.rsqrt(jnp.mean(x * x, axis=-1, keepdims=True) + RMS_EPS) * g_ref[...]).astype(BF16)
    off = 0
    for o_ref in (pa_ref, pb_ref, pc_ref, pd_ref):
        n = o_ref.shape[1]
        o_ref[...] = _dot(h, w_ref[:, off:off + n])
        off += n


def _in_proj(x, g, w):
    n = x.shape[0]
    tm = min(ROW_TILE, n)
    row = lambda width: pl.BlockSpec((tm, width), lambda i: (i, 0))
    return pl.pallas_call(
        _in_proj_body,
        grid=(n // tm,),
        in_specs=[row(D_MODEL), _const_spec((1, D_MODEL)), _const_spec((D_MODEL, P_W))],
        out_specs=[row(A_W), row(B_W), row(C_W), row(D_W)],
        out_shape=[jax.ShapeDtypeStruct((n, wd), F32) for wd in (A_W, B_W, C_W, D_W)],
        compiler_params=_params("parallel"),
        name="in_proj",
    )(x, g.reshape(1, D_MODEL), w)


def _post_body(x_ref, oa_ref, ob_ref, oc_ref, od_ref, wo_ref, g_ref, wu_ref, wd_ref, out_ref):
    mixed = jnp.concatenate([r[...] for r in (oa_ref, ob_ref, oc_ref, od_ref)], axis=1)
    x1 = x_ref[...] + _dot(mixed, wo_ref[...])
    h = (x1 * lax.rsqrt(jnp.mean(x1 * x1, axis=-1, keepdims=True) + RMS_EPS) * g_ref[...]).astype(BF16)
    y = x1
    for f in range(D_FF // D_MODEL):
        cols = slice(f * D_MODEL, (f + 1) * D_MODEL)
        u = jnp.square(jnp.maximum(_dot(h, wu_ref[:, cols]), 0.0)).astype(BF16)
        y = y + _dot(u, wd_ref[cols, :])
    out_ref[...] = y


def _post(x, outs, w_out, g, w_up, w_down):
    n = x.shape[0]
    tm = min(ROW_TILE, n)
    row = lambda width: pl.BlockSpec((tm, width), lambda i: (i, 0))
    return pl.pallas_call(
        _post_body,
        grid=(n // tm,),
        in_specs=[row(D_MODEL)] + [row(GROUP)] * 4 + [
            _const_spec((D_MODEL, D_MODEL)), _const_spec((1, D_MODEL)),
            _const_spec((D_MODEL, D_FF)), _const_spec((D_FF, D_MODEL))],
        out_specs=row(D_MODEL),
        out_shape=jax.ShapeDtypeStruct((n, D_MODEL), F32),
        compiler_params=_params("parallel"),
        name="post",
    )(x, *outs, w_out, g.reshape(1, D_MODEL), w_up, w_down)


def _group_mean_matrix(width, group):
    i = jnp.arange(width) // group
    return ((i[:, None] == i[None, :]).astype(F32) / group).astype(BF16)


def _group_sum_matrix(width, group):
    i = jnp.arange(width) // group
    return (i[:, None] == i[None, :]).astype(BF16)


def _da_prep_body(pa_ref, gq_ref, gk_ref, g32_ref, kpos_ref, qt_ref, k_ref, vt_ref):
    tm = pa_ref.shape[0]
    blk = vt_ref.shape[-1]
    q = pa_ref[:, 0:GROUP]
    k = pa_ref[:, GROUP:2 * GROUP]
    qm = _dot_x_exact(q * q, g32_ref[...], 2)
    km = _dot_x_exact(k * k, g32_ref[...], 2)
    q_t = (q * lax.rsqrt(qm + RMS_EPS) * gq_ref[...] * (DA_QK ** -0.5 * LOG2E)).T
    comp = (lax.broadcasted_iota(jnp.int32, q_t.shape, 0) // DA_QK) % 2
    n_extra = ATT_KA - 2 * DA_QK
    q_extra = (lax.broadcasted_iota(jnp.int32, (n_extra, tm), 0) < 3).astype(BF16)
    for c in range(2):
        q_c = jnp.where(comp == c, q_t, 0.0).astype(BF16)
        for h in range(HEADS):
            qt_ref[0, c, h * ATT_KA:h * ATT_KA + 2 * DA_QK, :] = q_c[h * HEAD_V:(h + 1) * HEAD_V]
            qt_ref[0, c, h * ATT_KA + 2 * DA_QK:(h + 1) * ATT_KA, :] = q_extra
    kn = (k * lax.rsqrt(km + RMS_EPS) * gk_ref[...]).astype(BF16)
    v_t = pa_ref[:, 2 * GROUP:3 * GROUP].T.astype(BF16)
    ones_row = (lax.broadcasted_iota(jnp.int32, (ATT_VA - HEAD_V, blk), 0) == 0).astype(BF16)
    for h in range(HEADS):
        k_ref[0, h, :, 0:2 * DA_QK] = kn[:, h * HEAD_V:(h + 1) * HEAD_V]
        for sb in range(tm // blk):
            k_ref[0, h, sb * blk:(sb + 1) * blk, 2 * DA_QK:ATT_KA] = kpos_ref[h]
        for sb in range(tm // blk):
            vt_ref[0, h, sb, 0:HEAD_V, :] = v_t[h * HEAD_V:(h + 1) * HEAD_V, sb * blk:(sb + 1) * blk]
            vt_ref[0, h, sb, HEAD_V:ATT_VA, :] = ones_row


def _da_prep(pa, batch, seq, blk, gq, gk, kpos):
    tm = min(ROW_TILE, seq)
    nt = seq // tm
    tile = lambda g: jnp.tile(g, GROUP // DA_QK).reshape(1, GROUP)
    return pl.pallas_call(
        _da_prep_body,
        grid=(batch, nt),
        in_specs=[pl.BlockSpec((tm, A_W), lambda b, t: (b * nt + t, 0)),
                  _const_spec((1, GROUP)), _const_spec((1, GROUP)), _const_spec((GROUP, GROUP)),
                  _const_spec(kpos.shape)],
        out_specs=[pl.BlockSpec((1, 2, HEADS * ATT_KA, tm), lambda b, t: (b, 0, 0, t)),
                   pl.BlockSpec((1, HEADS, tm, ATT_KA), lambda b, t: (b, 0, t, 0)),
                   pl.BlockSpec((1, HEADS, tm // blk, ATT_VA, blk), lambda b, t: (b, 0, t, 0, 0))],
        out_shape=[jax.ShapeDtypeStruct((batch, 2, HEADS * ATT_KA, seq), BF16),
                   jax.ShapeDtypeStruct((batch, HEADS, seq, ATT_KA), BF16),
                   jax.ShapeDtypeStruct((batch, HEADS, seq // blk, ATT_VA, blk), BF16)],
        compiler_params=_params("parallel", "parallel"),
        name="da_prep",
    )(pa, tile(gq), tile(gk), _group_mean_matrix(GROUP, DA_QK), kpos)


def _attn_body(qt_ref, k_ref, vt_ref, td_ref, lp_ref, go_ref, o_ref, m_ref, acc_ref, s_ref, *,
               lam_init, slopes, blk):
    i = pl.program_id(1)
    chains = [(h, c) for h in range(HEADS) for c in range(2)]

    def stage(cur, cur_slot, first, nxt=None, nxt_slot=None):
        for n, (h, c) in enumerate(chains):
            if nxt is not None:
                q_hc = qt_ref[0, c, h * ATT_KA:(h + 1) * ATT_KA, :]
                s_ref[nxt_slot, n] = _dot(k_ref[0, h, nxt], q_hc)
            if cur is None:
                continue
            off = -cur.astype(F32) * (slopes[h] * LOG2E * blk)
            s = s_ref[cur_slot, n]
            if first:
                s = s + td_ref[...]
            smax = jnp.max(s, axis=0, keepdims=True) - off
            if first:
                mx = smax
            else:
                m = m_ref[n]
                mx = jnp.maximum(m, smax)
            m_ref[n] = mx
            p = jnp.exp2(s - (mx + off)).astype(BF16)
            pv = _dot(vt_ref[0, h, cur], p)
            acc_ref[n] = pv if first else jnp.exp2(m - mx) * acc_ref[n] + pv

    stage(None, None, False, nxt=i, nxt_slot=0)
    stage(i, 0, True, nxt=0, nxt_slot=1)

    def body(u, carry):
        stage(2 * u, 1, False, nxt=2 * u + 1, nxt_slot=0)
        stage(2 * u + 1, 0, False, nxt=2 * u + 2, nxt_slot=1)
        return carry

    lax.fori_loop(0, (i - 1) // 2, body, 0)
    odd = lax.rem(i, 2) == 1

    @pl.when(jnp.logical_and(jnp.logical_not(odd), i > 0))
    def _():
        stage(i - 2, 1, False, nxt=i - 1, nxt_slot=0)
        stage(i - 1, 0, False)

    @pl.when(odd)
    def _():
        stage(i - 1, 1, False)

    lp = lp_ref[...]
    lam = (jnp.exp(jnp.sum(lp[0:1] * lp[1:2], axis=-1, keepdims=True))
           - jnp.exp(jnp.sum(lp[2:3] * lp[3:4], axis=-1, keepdims=True)) + lam_init)
    outs = []
    for h in range(HEADS):
        a1, a2 = acc_ref[2 * h], acc_ref[2 * h + 1]
        o = a1[0:HEAD_V] / a1[HEAD_V:HEAD_V + 1] - lam * (a2[0:HEAD_V] / a2[HEAD_V:HEAD_V + 1])
        outs.append(o * lax.rsqrt(jnp.mean(o * o, axis=0, keepdims=True) + RMS_EPS) * go_ref[...]
                    * (1.0 - lam_init))
    o_ref[...] = jnp.concatenate(outs, axis=0).T.astype(BF16)


def _diff_attention(pa, batch, seq, gq, gk, lq1, lk1, lq2, lk2, g_out, lam_init):
    blk = min(ATT_BLOCK, seq)
    nb = seq // blk
    slopes = tuple(2.0 ** (-8.0 * (h + 1) / HEADS) for h in range(HEADS))
    kpos = np.zeros((HEADS, blk, ATT_KA - 2 * DA_QK), ml_dtypes.bfloat16)
    for h, s in enumerate(slopes):
        r = np.float32(s * LOG2E) * np.arange(blk, dtype=np.float32)
        for part in range(3):
            kpos[h, :, part] = r.astype(ml_dtypes.bfloat16)
            r = r - kpos[h, :, part].astype(np.float32)
    qt, k, vt = _da_prep(pa, batch, seq, blk, gq, gk, jnp.asarray(kpos))
    kh = k.reshape(batch, HEADS, nb, blk, ATT_KA)
    rel = jnp.arange(blk)[None, :] - jnp.arange(blk)[:, None]
    td = jnp.where(rel >= 0, 0.0, MASK_VALUE).astype(F32)
    lp = jnp.stack([lq1, lk1, lq2, lk2])
    go = jnp.broadcast_to(g_out[:, None], (HEAD_V, blk))
    body = functools.partial(_attn_body, lam_init=lam_init, slopes=slopes, blk=blk)
    return pl.pallas_call(
        body,
        grid=(batch, nb),
        in_specs=[
            pl.BlockSpec((1, 2, HEADS * ATT_KA, blk), lambda b, i: (b, 0, 0, i)),
            pl.BlockSpec((1, HEADS, nb, blk, ATT_KA), lambda b, i: (b, 0, 0, 0, 0)),
            pl.BlockSpec((1, HEADS, nb, ATT_VA, blk), lambda b, i: (b, 0, 0, 0, 0)),
            _const_spec((blk, blk)),
            _const_spec((4, DA_QK)), _const_spec((HEAD_V, blk))],
        out_specs=pl.BlockSpec((blk, GROUP), lambda b, i: (b * nb + i, 0)),
        out_shape=jax.ShapeDtypeStruct((batch * seq, GROUP), BF16),
        scratch_shapes=[pltpu.VMEM((2 * HEADS, 1, blk), F32), pltpu.VMEM((2 * HEADS, ATT_VA, blk), F32),
                        pltpu.VMEM((2, 2 * HEADS, blk, blk), F32)],
        compiler_params=_params("parallel", "arbitrary"),
        name="diff_attention",
    )(qt, kh, vt, td, lp, go)


def _gla_consts(tb, hk):
    c, sub, nc, kh = GLA_CHUNK, GLA_SUB, tb // GLA_CHUNK, hk // HEADS
    nsub = c // sub
    width = nsub * HEADS * sub
    t = jnp.arange(tb)
    lseg = ((t[:, None] // c == t[None, :] // c) & (t[None, :] <= t[:, None])).astype(BF16)
    r = jnp.arange(sub * hk)
    col = jnp.arange(width)
    ind = ((r[:, None] // hk == col[None, :] % sub)
           & ((r[:, None] % hk) // kh == (col[None, :] // sub) % HEADS)).astype(BF16)
    submask = ((t[:, None] % c) // sub == col[None, :] // (HEADS * sub)).astype(F32)
    cmask_a = (t[:, None] // c == jnp.arange(nc * width)[None, :] // width).astype(F32)
    cmask_q = (t[:, None] // c == jnp.arange(nc * hk)[None, :] // hk).astype(F32)
    vr = jnp.arange(nc * width)
    row_h = (vr[:, None] // sub) % HEADS
    vmask = (row_h == jnp.arange(GROUP)[None, :] // HEAD_V).astype(BF16)
    kmask = (((vr[:, None] % width) // (HEADS * sub) == 0)
             & (row_h == jnp.arange(hk)[None, :] // kh)).astype(BF16)
    bdmask = (jnp.arange(GROUP)[:, None] // HEAD_V == jnp.arange(hk)[None, :] // kh).astype(F32)
    return lseg, ind, submask, cmask_a, cmask_q, vmask, kmask, bdmask


class _GlaStream:
    def __init__(self, q, k, lf, v, consts, scratch):
        self.q, self.k, self.lf, self.v = q, k, lf, v
        (self.lseg, self.ind, self.submask, self.cmask_a, self.cmask_q, self.vmask, self.kmask,
         self.bdmask) = consts
        self.st, self.tcat, self.bk, self.stack = scratch


def _gla_core(streams):
    c, sub = GLA_CHUNK, GLA_SUB
    assert c == 2 * sub
    tb = streams[0].q.shape[0]
    nc = tb // c
    for s in streams:
        s.b = _dot_exact_x(s.lseg[...], s.lf, 3) * LOG2E
    for s in streams:
        hk = s.q.shape[1]
        s.bk[0] = s.b
        s.bk[1] = s.k
        s.row = lax.broadcasted_iota(jnp.int32, (tb, hk), 0)
        rowmod = s.row % sub
        for jj in range(sub):
            bj = _chunk_row_bcast(s.bk, 0, jj, sub)
            kj = _chunk_row_bcast(s.bk, 1, jj, sub)
            t = s.q * kj * jnp.exp2(s.b - bj)
            s.tcat[:, jj * hk:(jj + 1) * hk] = jnp.where(rowmod >= jj, t, 0.0).astype(BF16)
    for s in streams:
        s.a = _dot(s.tcat[...], s.ind[...]) * s.submask[...]
    for s in streams:
        bref = _chunk_row_bcast(s.bk, 0, sub - 1, c)
        second = s.row % c >= sub
        s.qm = jnp.where(second, s.q * jnp.exp2(s.b - bref), 0.0).astype(BF16)
        kt = jnp.where(second, 0.0, s.k * jnp.exp2(bref - s.b)).astype(BF16)
        s.k_exp = jnp.concatenate([kt[ci * c:ci * c + sub] for ci in range(nc) for _ in range(2 * HEADS)],
                                  axis=0) * s.kmask[...]
    for s in streams:
        s.a_off = _dot_nt(s.qm, s.k_exp)
    for s in streams:
        s.a_exp = ((jnp.concatenate([s.a] * nc, axis=1) + s.a_off) * s.cmask_a[...]).astype(BF16)
        s.vb = s.v.astype(BF16)
        s.v_bd = jnp.concatenate([s.vb[ci * c + h * sub:ci * c + (h + 1) * sub]
                                  for ci in range(nc) for h in range(2) for _ in range(HEADS)],
                                 axis=0) * s.vmask[...]
    for s in streams:
        s.o = _dot(s.a_exp, s.v_bd)
    for s in streams:
        blast = _chunk_row_bcast(s.bk, 0, c - 1, c)
        qe = s.q * jnp.exp2(s.b)
        s.ke = (s.k * jnp.exp2(blast - s.b)).astype(BF16)
        s.q_exp = (jnp.concatenate([qe] * nc, axis=1) * s.cmask_q[...]).astype(BF16)
    for s in streams:
        s.pt = [_dot_tn(s.vb[ci * c:(ci + 1) * c], s.ke[ci * c:(ci + 1) * c]) * s.bdmask[...]
                for ci in range(nc)]
    for s in streams:
        hk = s.q.shape[1]
        st = s.st[...]
        for ci in range(nc):
            s.stack[:, ci * hk:(ci + 1) * hk] = st.astype(BF16)
            st = st * jnp.exp2(s.bk[0, ci * c + c - 1:ci * c + c, :]) + s.pt[ci]
        s.st[...] = st
    return [s.o + _dot_nt(s.q_exp, s.stack[...]) for s in streams]


def _gla_scratch(tb, hk):
    nc = tb // GLA_CHUNK
    return [pltpu.VMEM((GROUP, hk), F32), pltpu.VMEM((tb, GLA_SUB * hk), BF16),
            pltpu.VMEM((2, tb, hk), F32), pltpu.VMEM((GROUP, nc * hk), BF16)]


N_GLA_CONSTS = 8
N_GLA_SCRATCH = 4


def _gla_hgrn_body(pb_ref, pc_ref, gup_ref, gb_ref, gout_b_ref, lg_ref, gout_c_ref, g64_ref, *rest, layer):
    consts_b, consts_c = rest[:N_GLA_CONSTS], rest[N_GLA_CONSTS:2 * N_GLA_CONSTS]
    ob_ref, oc_ref = rest[2 * N_GLA_CONSTS:2 * N_GLA_CONSTS + 2]
    scratch = rest[2 * N_GLA_CONSTS + 2:]
    scratch_b, scratch_c = scratch[:N_GLA_SCRATCH], scratch[N_GLA_SCRATCH:]

    @pl.when(pl.program_id(1) == 0)
    def _():
        scratch_b[0][...] = jnp.zeros_like(scratch_b[0])
        scratch_c[0][...] = jnp.zeros_like(scratch_c[0])

    hk = HEADS * GLA_K
    gate_lo, og_lo = 2 * hk + GROUP, 2 * hk + GROUP + LANE
    logit = _dot(pb_ref[:, gate_lo:og_lo].astype(BF16), gup_ref[...]) + gb_ref[...]
    lf_b = (jnp.minimum(logit, 0.0) - jnp.log(1.0 + jnp.exp(-jnp.abs(logit)))) * (1.0 / GLA_NORMALIZER)
    gla = _GlaStream(pb_ref[:, 0:hk] * (GLA_K ** -0.5), pb_ref[:, hk:2 * hk], lf_b,
                     pb_ref[:, 2 * hk:2 * hk + GROUP], consts_b, scratch_b)
    lg = lg_ref[...]
    e = jnp.exp(lg - jnp.max(lg, axis=0, keepdims=True))
    p = e / jnp.sum(e, axis=0, keepdims=True)
    cs = p[0:1]
    for i in range(1, layer + 1):
        cs = cs + p[i:i + 1]
    lb = cs - p[0:1]
    z = pc_ref[:, GROUP:2 * GROUP]
    forget = lb + (1.0 - lb) * _sigmoid(z)
    hgrn = _GlaStream(_silu(pc_ref[:, 0:GROUP]), (1.0 - lb) * _sigmoid(-z),
                      jnp.log(jnp.maximum(forget, TINY)), pc_ref[:, 2 * GROUP:3 * GROUP], consts_c, scratch_c)
    o_b, o_c = _gla_core([gla, hgrn])
    for o, og, gout_ref, o_ref in ((o_b, pb_ref[:, og_lo:og_lo + GROUP], gout_b_ref, ob_ref),
                                   (o_c, pc_ref[:, 3 * GROUP:4 * GROUP], gout_c_ref, oc_ref)):
        ms = _dot_x_exact(o * o, g64_ref[...], 2)
        o_ref[...] = (o * lax.rsqrt(ms + RMS_EPS) * gout_ref[...] * _silu(og)).astype(BF16)


def _gla_hgrn(pb, pc, batch, seq, gate_up, gate_b, g_out_b, lb_logits, g_out_c, layer):
    tb = min(TIME_BLOCK, seq)
    nt = seq // tb
    hk_b, hk_c = HEADS * GLA_K, HEADS * HG_K
    gup = jnp.zeros((LANE, hk_b), F32).at[:GLA_RANK].set(gate_up).astype(BF16)
    tile = lambda g: jnp.tile(g, HEADS).reshape(1, GROUP)
    extra = [gup, gate_b.reshape(1, hk_b), tile(g_out_b), lb_logits, tile(g_out_c),
             _group_mean_matrix(GROUP, HEAD_V), *_gla_consts(tb, hk_b), *_gla_consts(tb, hk_c)]
    row = lambda width: pl.BlockSpec((tb, width), lambda b, t: (b * nt + t, 0))
    return pl.pallas_call(
        functools.partial(_gla_hgrn_body, layer=layer),
        grid=(batch, nt),
        in_specs=[row(B_W), row(C_W)] + [_const_spec(e.shape) for e in extra],
        out_specs=[row(GROUP), row(GROUP)],
        out_shape=[jax.ShapeDtypeStruct((batch * seq, GROUP), BF16)] * 2,
        scratch_shapes=_gla_scratch(tb, hk_b) + _gla_scratch(tb, hk_c),
        compiler_params=_params("parallel", "arbitrary"),
        name="gla_hgrn2",
    )(pb, pc, *extra)


def _time_grid_call(body, name, p, batch, seq, extra, scratch):
    tb = min(TIME_BLOCK, seq)
    nt = seq // tb
    width = p.shape[1]
    return pl.pallas_call(
        body,
        grid=(batch, nt),
        in_specs=[pl.BlockSpec((tb, width), lambda b, t: (b * nt + t, 0))]
                 + [_const_spec(e.shape) for e in extra],
        out_specs=pl.BlockSpec((tb, GROUP), lambda b, t: (b * nt + t, 0)),
        out_shape=jax.ShapeDtypeStruct((batch * seq, GROUP), BF16),
        scratch_shapes=scratch,
        compiler_params=_params("parallel", "arbitrary"),
        name=name,
    )(p, *extra)


def _rwkv_consts(tb):
    c, nch = RW_CHUNK, tb // RW_CHUNK
    t = jnp.arange(tb)
    same = t[:, None] // c == t[None, :] // c
    lseg = (same & (t[None, :] <= t[:, None])).astype(BF16)
    strict = (same & (t[None, :] < t[:, None])).astype(F32)
    incl = (same & (t[None, :] <= t[:, None])).astype(F32)
    eye = jnp.eye(tb, dtype=F32)
    hmask = (jnp.arange(HEADS)[:, None] == jnp.arange(GROUP)[None, :] // HEAD_V).astype(F32)
    bd = (jnp.arange(GROUP)[:, None] // HEAD_V == jnp.arange(GROUP)[None, :] // HEAD_V).astype(F32)
    cmask = (t[:, None] // c == jnp.arange(nch * GROUP)[None, :] // GROUP).astype(F32)
    eye_g = jnp.eye(GROUP, dtype=F32)
    return lseg, strict, incl, eye, hmask, bd, cmask, eye_g


def _rwkv_core(r, logw, k2, v, kk, a, consts, mt_ref, gam_ref, stack_ref):
    lseg_ref, strict_ref, incl_ref, eye_ref, hmask_ref, bd_ref, cmask_ref, eyeg_ref = consts
    tb = r.shape[0]
    c = RW_CHUNK
    nch = tb // c
    gam = _dot_exact_x(lseg_ref[...], logw, 3)
    gam_ref[0] = gam
    glast = _chunk_row_bcast(gam_ref, 0, c - 1, c)
    eng = jnp.exp(-gam)
    ecl = jnp.exp(glast - gam)
    beta = a * kk
    abar = kk * jnp.exp(gam - logw)
    rbar = r * jnp.exp(gam)
    yb = jnp.concatenate([beta * eng, k2 * eng], axis=0).astype(BF16)
    bhat = beta * ecl
    khat = k2 * ecl
    strict, incl = strict_ref[...] > 0.0, incl_ref[...] > 0.0
    eye = eye_ref[...]
    heads = range(HEADS)
    hms = [hmask_ref[h:h + 1, :] for h in heads]
    abar_h = [abar * hms[h] for h in heads]
    vh = [(v * hms[h]).astype(BF16) for h in heads]
    pw, a_rb, a_rk, z = [], [], [], []
    for h in heads:
        xh = jnp.concatenate([abar_h[h], rbar * hms[h]], axis=0).astype(BF16)
        s = _dot_nt(xh, yb)
        pw.append(jnp.where(strict, -s[:tb, :tb], 0.0))
        a_ak = jnp.where(strict, s[:tb, tb:], 0.0).astype(BF16)
        a_rb.append(jnp.where(incl, s[tb:, :tb], 0.0).astype(BF16))
        a_rk.append(jnp.where(incl, s[tb:, tb:], 0.0).astype(BF16))
        z.append(jnp.concatenate([abar_h[h], _dot(a_ak, vh[h])], axis=1).astype(BF16))
    tinv = [eye + pw[h] for h in heads]
    for _ in range(int(math.log2(c)) - 1):
        pwb = [pw[h].astype(BF16) for h in heads]
        pw = [_dot(pwb[h], pwb[h]) for h in heads]
        tinv = [tinv[h] + _dot(tinv[h].astype(BF16), pw[h].astype(BF16)) for h in heads]
    wu_h = [_dot(tinv[h].astype(BF16), z[h]) for h in heads]
    om_h = [_dot(a_rb[h], wu_h[h].astype(BF16)) for h in heads]
    arkv_h = [_dot(a_rk[h], vh[h]) for h in heads]
    wu, om, arkv = sum(wu_h[1:], wu_h[0]), sum(om_h[1:], om_h[0]), sum(arkv_h[1:], arkv_h[0])
    wbar, u0 = wu[:, :GROUP], wu[:, GROUP:]
    omega = rbar - om[:, :GROUP]
    y0 = arkv - om[:, GROUP:]
    wb, bhb = wbar.astype(BF16), bhat.astype(BF16)
    vb, nub, khb = v.astype(BF16), (-u0).astype(BF16), khat.astype(BF16)
    x_c, psi = [], []
    for ci in range(nch):
        rows = slice(ci * c, (ci + 1) * c)
        x_c.append((_dot_tn(wb[rows], bhb[rows]) * bd_ref[...]).astype(BF16))
        psi.append(_dot_tn(jnp.concatenate([vb[rows], nub[rows]], axis=0),
                           jnp.concatenate([khb[rows], bhb[rows]], axis=0)) * bd_ref[...])
    mt = mt_ref[...]
    for ci in range(nch):
        stack_ref[:, ci * GROUP:(ci + 1) * GROUP] = mt.astype(BF16)
        gl = jnp.exp(gam_ref[0, ci * c + c - 1:ci * c + c, :])
        mt = mt * gl - _dot(mt.astype(BF16), x_c[ci]) + psi[ci]
    mt_ref[...] = mt
    om_exp = (jnp.concatenate([omega] * nch, axis=1) * cmask_ref[...]).astype(BF16)
    return y0 + _dot_nt(om_exp, stack_ref[...])


def _rwkv_body(pd_ref, mu_ref, w0_ref, wup_ref, a0_ref, aup_ref, gup_ref, kk_ref, ka_ref, rk_ref,
               lng_ref, lnb_ref, g64_ref, s64_ref, *rest):
    consts, (o_ref, mt_ref, carry_ref, gam_ref, stack_ref) = rest[:8], rest[8:]
    tb = pd_ref.shape[0]

    @pl.when(pl.program_id(1) == 0)
    def _():
        mt_ref[...] = jnp.zeros_like(mt_ref)
        carry_ref[...] = jnp.zeros_like(carry_ref)

    p = pd_ref[...]
    rowid = lax.broadcasted_iota(jnp.int32, p.shape, 0)
    prev = jnp.where(rowid == 0, carry_ref[...], pltpu.roll(p, 1, axis=0))
    carry_ref[...] = p[tb - 1:tb, :]
    p = p + mu_ref[...] * (prev - p)
    r, k, v = p[:, 0:GROUP], p[:, GROUP:2 * GROUP], p[:, 2 * GROUP:3 * GROUP]
    low = p[:, 3 * GROUP:3 * GROUP + LANE]
    logw = -RW_DECAY_SCALE * _sigmoid(w0_ref[...] + _dot(jnp.tanh(low).astype(BF16), wup_ref[...]))
    a = _sigmoid(a0_ref[...] + _dot(low.astype(BF16), aup_ref[...]))
    g = _dot(_sigmoid(low).astype(BF16), gup_ref[...])
    kk = k * kk_ref[...]
    kk = kk * lax.rsqrt(jnp.maximum(_dot_x_exact(kk * kk, s64_ref[...], 2), 1e-24))
    k2 = k * (1.0 + (a - 1.0) * ka_ref[...])
    y = _rwkv_core(r, logw, k2, v, kk, a, consts, mt_ref, gam_ref, stack_ref)
    mean = _dot_x_exact(y, g64_ref[...], 2)
    d = y - mean
    var = _dot_x_exact(d * d, g64_ref[...], 2)
    yn = d * lax.rsqrt(var + RW_LN_EPS) * lng_ref[...] + lnb_ref[...]
    bonus = _dot_x_exact(r * k2 * rk_ref[...], s64_ref[...], 2) * v
    o_ref[...] = ((yn + bonus) * g).astype(BF16)


def _rwkv(pd, batch, seq, mu, w0, w_up, a0, a_up, g_up, k_k, k_a, r_k, ln_g, ln_b):
    tb = min(TIME_BLOCK, seq)
    nch = tb // RW_CHUNK
    row = lambda z: z.reshape(1, -1)
    low = lambda w, start: jnp.zeros((LANE, GROUP), F32).at[start:start + w.shape[0]].set(w).astype(BF16)
    extra = [row(mu), row(w0), low(w_up, 0), row(a0), low(a_up, 32), low(g_up, 64), row(k_k), row(k_a),
             row(r_k), row(ln_g), row(ln_b), _group_mean_matrix(GROUP, HEAD_V),
             _group_sum_matrix(GROUP, HEAD_V), *_rwkv_consts(tb)]
    scratch = [pltpu.VMEM((GROUP, GROUP), F32), pltpu.VMEM((1, D_W), F32),
               pltpu.VMEM((1, tb, GROUP), F32), pltpu.VMEM((GROUP, nch * GROUP), BF16)]
    return _time_grid_call(_rwkv_body, "rwkv7", pd, batch, seq, extra, scratch)


def _relayout_w_in(w_in):
    a_end, b_end, c_end = 768, 768 + 784, 768 + 784 + 1024
    wb = w_in[:, a_end:b_end]
    pad = jnp.zeros((w_in.shape[0], LANE - GLA_RANK), w_in.dtype)
    wb = jnp.concatenate([wb[:, :528], pad, wb[:, 528:]], axis=1)
    return jnp.concatenate([w_in[:, :a_end], wb, w_in[:, b_end:c_end], w_in[:, c_end:]], axis=1).astype(BF16)


def kernel(x, norm_mix_g, w_in, da_q_norm_g, da_k_norm_g, da_lambda_q1, da_lambda_k1, da_lambda_q2, da_lambda_k2, da_out_norm_g, gla_gate_up, gla_gate_b, gla_out_norm_g, hgrn_lb_logits, hgrn_out_norm_g, rw_shift_mu, rw_w0, rw_w_up, rw_a0, rw_a_up, rw_g_up, rw_k_k, rw_k_a, rw_r_k, rw_ln_g, rw_ln_b, w_out, norm_mlp_g, w_mlp_up, w_mlp_down):
    batch, seq, _ = x.shape
    xf = x.reshape(batch * seq, D_MODEL)
    for l in range(DEPTH):
        lam_init = 0.8 - 0.6 * math.exp(-0.3 * l)
        pa, pb, pc, pd = _in_proj(xf, norm_mix_g[l], _relayout_w_in(w_in[l]))
        o_a = _diff_attention(pa, batch, seq, da_q_norm_g[l], da_k_norm_g[l], da_lambda_q1[l],
                              da_lambda_k1[l], da_lambda_q2[l], da_lambda_k2[l], da_out_norm_g[l], lam_init)
        o_b, o_c = _gla_hgrn(pb, pc, batch, seq, gla_gate_up[l], gla_gate_b[l], gla_out_norm_g[l],
                             hgrn_lb_logits, hgrn_out_norm_g[l], l)
        o_d = _rwkv(pd, batch, seq, rw_shift_mu[l], rw_w0[l], rw_w_up[l], rw_a0[l], rw_a_up[l], rw_g_up[l],
                    rw_k_k[l], rw_k_a[l], rw_r_k[l].reshape(-1), rw_ln_g[l], rw_ln_b[l])
        xf = _post(xf, (o_a, o_b, o_c, o_d), w_out[l].astype(BF16), norm_mlp_g[l],
                   w_mlp_up[l].astype(BF16), w_mlp_down[l].astype(BF16))
    return xf.reshape(batch, seq, D_MODEL)
```

```python
import functools
import math

import jax
import jax.numpy as jnp
import ml_dtypes
import numpy as np
from jax import lax
from jax.experimental import pallas as pl
from jax.experimental.pallas import tpu as pltpu

F32, BF16 = jnp.float32, jnp.bfloat16

D_MODEL = 1024
GROUP = 256
HEADS = 4
HEAD_V = 64
DA_QK = 32
GLA_K = 32
GLA_RANK = 16
GLA_NORMALIZER = 16.0
HG_K = 64
RW_DECAY_SCALE = 0.606531
RW_LN_EPS = 64e-5
D_FF = 4 * D_MODEL
RMS_EPS = 1e-6
MASK_VALUE = -1e30
TINY = 1e-30
DEPTH = 4

LANE = 128
A_W, B_W, C_W, D_W = 768, 896, 1024, 896
P_W = A_W + B_W + C_W + D_W

ROW_TILE = 512
TIME_BLOCK = 256
ATT_BLOCK = 256
ATT_VA = 80
ATT_KA = 80
LOG2E = 1.4426950408889634
GLA_CHUNK = 32
GLA_SUB = 16
RW_CHUNK = 64
VMEM_LIMIT = 56 * 1024 * 1024


def _dot(a, b):
    return jnp.dot(a, b, preferred_element_type=F32)


def _dot_nt(a, b):
    return lax.dot_general(a, b, (((1,), (1,)), ((), ())), preferred_element_type=F32)


def _dot_tn(a, b):
    return lax.dot_general(a, b, (((0,), (0,)), ((), ())), preferred_element_type=F32)


def _split(x, parts):
    out, r = [], x
    for i in range(parts):
        h = r.astype(BF16)
        out.append(h)
        if i + 1 < parts:
            r = r - h.astype(F32)
    return out


def _dot_x_exact(x, m, parts):
    acc = None
    for h in _split(x, parts):
        d = _dot(h, m)
        acc = d if acc is None else acc + d
    return acc


def _dot_exact_x(m, x, parts):
    acc = None
    for h in _split(x, parts):
        d = _dot(m, h)
        acc = d if acc is None else acc + d
    return acc


def _sigmoid(x):
    return 1.0 / (1.0 + jnp.exp(-x))


def _silu(x):
    return x * _sigmoid(x)


def _const_spec(shape):
    zeros = (0,) * len(shape)
    return pl.BlockSpec(shape, lambda *_: zeros)


def _chunk_row_bcast(ref, idx, row, chunk):
    n_rows, width = ref.shape[1], ref.shape[2]
    pieces = []
    for c in range(n_rows // chunk):
        r = c * chunk + row
        pieces.append(jnp.broadcast_to(ref[idx, r:r + 1, :], (chunk, width)))
    return jnp.concatenate(pieces, axis=0)


def _params(*sem):
    return pltpu.CompilerParams(dimension_semantics=sem, vmem_limit_bytes=VMEM_LIMIT)


def _in_proj_body(x_ref, g_ref, w_ref, pa_ref, pb_ref, pc_ref, pd_ref):
    x = x_ref[...]
    h = (x * lax.rsqrt(jnp.mean(x * x, axis=-1, keepdims=True) + RMS_EPS) * g_ref[...]).astype(BF16)
    off = 0
    for o_ref in (pa_ref, pb_ref, pc_ref, pd_ref):
        n = o_ref.shape[1]
        o_ref[...] = _dot(h, w_ref[:, off:off + n])
        off += n


def _in_proj(x, g, w):
    n = x.shape[0]
    tm = min(ROW_TILE, n)
    row = lambda width: pl.BlockSpec((tm, width), lambda i: (i, 0))
    return pl.pallas_call(
        _in_proj_body,
        grid=(n // tm,),
        in_specs=[row(D_MODEL), _const_spec((1, D_MODEL)), _const_spec((D_MODEL, P_W))],
        out_specs=[row(A_W), row(B_W), row(C_W), row(D_W)],
        out_shape=[jax.ShapeDtypeStruct((n, wd), F32) for wd in (A_W, B_W, C_W, D_W)],
        compiler_params=_params("parallel"),
        name="in_proj",
    )(x, g.reshape(1, D_MODEL), w)


def _post_body(x_ref, oa_ref, ob_ref, oc_ref, od_ref, wo_ref, g_ref, wu_ref, wd_ref, out_ref):
    mixed = jnp.concatenate([r[...] for r in (oa_ref, ob_ref, oc_ref, od_ref)], axis=1)
    x1 = x_ref[...] + _dot(mixed, wo_ref[...])
    h = (x1 * lax.rsqrt(jnp.mean(x1 * x1, axis=-1, keepdims=True) + RMS_EPS) * g_ref[...]).astype(BF16)
    y = x1
    for f in range(D_FF // D_MODEL):
        cols = slice(f * D_MODEL, (f + 1) * D_MODEL)
        u = jnp.square(jnp.maximum(_dot(h, wu_ref[:, cols]), 0.0)).astype(BF16)
        y = y + _dot(u, wd_ref[cols, :])
    out_ref[...] = y


def _post(x, outs, w_out, g, w_up, w_down):
    n = x.shape[0]
    tm = min(ROW_TILE, n)
    row = lambda width: pl.BlockSpec((tm, width), lambda i: (i, 0))
    return pl.pallas_call(
        _post_body,
        grid=(n // tm,),
        in_specs=[row(D_MODEL)] + [row(GROUP)] * 4 + [
            _const_spec((D_MODEL, D_MODEL)), _const_spec((1, D_MODEL)),
            _const_spec((D_MODEL, D_FF)), _const_spec((D_FF, D_MODEL))],
        out_specs=row(D_MODEL),
        out_shape=jax.ShapeDtypeStruct((n, D_MODEL), F32),
        compiler_params=_params("parallel"),
        name="post",
    )(x, *outs, w_out, g.reshape(1, D_MODEL), w_up, w_down)


def _group_mean_matrix(width, group):
    i = jnp.arange(width) // group
    return ((i[:, None] == i[None, :]).astype(F32) / group).astype(BF16)


def _group_sum_matrix(width, group):
    i = jnp.arange(width) // group
    return (i[:, None] == i[None, :]).astype(BF16)


def _da_prep_body(pa_ref, gq_ref, gk_ref, g32_ref, kpos_ref, qt_ref, k_ref, vt_ref):
    tm = pa_ref.shape[0]
    blk = vt_ref.shape[-1]
    q = pa_ref[:, 0:GROUP]
    k = pa_ref[:, GROUP:2 * GROUP]
    qm = _dot_x_exact(q * q, g32_ref[...], 2)
    km = _dot_x_exact(k * k, g32_ref[...], 2)
    q_t = (q * lax.rsqrt(qm + RMS_EPS) * gq_ref[...] * (DA_QK ** -0.5 * LOG2E)).T
    comp = (lax.broadcasted_iota(jnp.int32, q_t.shape, 0) // DA_QK) % 2
    n_extra = ATT_KA - 2 * DA_QK
    q_extra = (lax.broadcasted_iota(jnp.int32, (n_extra, tm), 0) < 3).astype(BF16)
    for c in range(2):
        q_c = jnp.where(comp == c, q_t, 0.0).astype(BF16)
        for h in range(HEADS):
            qt_ref[0, c, h * ATT_KA:h * ATT_KA + 2 * DA_QK, :] = q_c[h * HEAD_V:(h + 1) * HEAD_V]
            qt_ref[0, c, h * ATT_KA + 2 * DA_QK:(h + 1) * ATT_KA, :] = q_extra
    kn = (k * lax.rsqrt(km + RMS_EPS) * gk_ref[...]).astype(BF16)
    v_t = pa_ref[:, 2 * GROUP:3 * GROUP].T.astype(BF16)
    ones_row = (lax.broadcasted_iota(jnp.int32, (ATT_VA - HEAD_V, blk), 0) == 0).astype(BF16)
    for h in range(HEADS):
        k_ref[0, h, :, 0:2 * DA_QK] = kn[:, h * HEAD_V:(h + 1) * HEAD_V]
        for sb in range(tm // blk):
            k_ref[0, h, sb * blk:(sb + 1) * blk, 2 * DA_QK:ATT_KA] = kpos_ref[h]
        for sb in range(tm // blk):
            vt_ref[0, h, sb, 0:HEAD_V, :] = v_t[h * HEAD_V:(h + 1) * HEAD_V, sb * blk:(sb + 1) * blk]
            vt_ref[0, h, sb, HEAD_V:ATT_VA, :] = ones_row


def _da_prep(pa, batch, seq, blk, gq, gk, kpos):
    tm = min(ROW_TILE, seq)
    nt = seq // tm
    tile = lambda g: jnp.tile(g, GROUP // DA_QK).reshape(1, GROUP)
    return pl.pallas_call(
        _da_prep_body,
        grid=(batch, nt),
        in_specs=[pl.BlockSpec((tm, A_W), lambda b, t: (b * nt + t, 0)),
                  _const_spec((1, GROUP)), _const_spec((1, GROUP)), _const_spec((GROUP, GROUP)),
                  _const_spec(kpos.shape)],
        out_specs=[pl.BlockSpec((1, 2, HEADS * ATT_KA, tm), lambda b, t: (b, 0, 0, t)),
                   pl.BlockSpec((1, HEADS, tm, ATT_KA), lambda b, t: (b, 0, t, 0)),
                   pl.BlockSpec((1, HEADS, tm // blk, ATT_VA, blk), lambda b, t: (b, 0, t, 0, 0))],
        out_shape=[jax.ShapeDtypeStruct((batch, 2, HEADS * ATT_KA, seq), BF16),
                   jax.ShapeDtypeStruct((batch, HEADS, seq, ATT_KA), BF16),
                   jax.ShapeDtypeStruct((batch, HEADS, seq // blk, ATT_VA, blk), BF16)],
        compiler_params=_params("parallel", "parallel"),
        name="da_prep",
    )(pa, tile(gq), tile(gk), _group_mean_matrix(GROUP, DA_QK), kpos)


def _attn_body(qt_ref, k_ref, vt_ref, td_ref, lp_ref, go_ref, o_ref, m_ref, acc_ref, s_ref, *,
               lam_init, slopes, blk):
    i = pl.program_id(1)
    chains = [(h, c) for h in range(HEADS) for c in range(2)]

    def stage(cur, cur_slot, first, nxt=None, nxt_slot=None):
        for n, (h, c) in enumerate(chains):
            if nxt is not None:
                q_hc = qt_ref[0, c, h * ATT_KA:(h + 1) * ATT_KA, :]
                s_ref[nxt_slot, n] = _dot(k_ref[0, h, nxt], q_hc)
            if cur is None:
                continue
            off = -cur.astype(F32) * (slopes[h] * LOG2E * blk)
            s = s_ref[cur_slot, n]
            if first:
                s = s + td_ref[...]
            smax = jnp.max(s, axis=0, keepdims=True) - off
            if first:
                mx = smax
            else:
                m = m_ref[n]
                mx = jnp.maximum(m, smax)
            m_ref[n] = mx
            p = jnp.exp2(s - (mx + off)).astype(BF16)
            pv = _dot(vt_ref[0, h, cur], p)
            acc_ref[n] = pv if first else jnp.exp2(m - mx) * acc_ref[n] + pv

    stage(None, None, False, nxt=i, nxt_slot=0)
    stage(i, 0, True, nxt=0, nxt_slot=1)

    def body(u, carry):
        stage(2 * u, 1, False, nxt=2 * u + 1, nxt_slot=0)
        stage(2 * u + 1, 0, False, nxt=2 * u + 2, nxt_slot=1)
        return carry

    lax.fori_loop(0, (i - 1) // 2, body, 0)
    odd = lax.rem(i, 2) == 1

    @pl.when(jnp.logical_and(jnp.logical_not(odd), i > 0))
    def _():
        stage(i - 2, 1, False, nxt=i - 1, nxt_slot=0)
        stage(i - 1, 0, False)

    @pl.when(odd)
    def _():
        stage(i - 1, 1, False)

    lp = lp_ref[...]
    lam = (jnp.exp(jnp.sum(lp[0:1] * lp[1:2], axis=-1, keepdims=True))
           - jnp.exp(jnp.sum(lp[2:3] * lp[3:4], axis=-1, keepdims=True)) + lam_init)
    outs = []
    for h in range(HEADS):
        a1, a2 = acc_ref[2 * h], acc_ref[2 * h + 1]
        o = a1[0:HEAD_V] / a1[HEAD_V:HEAD_V + 1] - lam * (a2[0:HEAD_V] / a2[HEAD_V:HEAD_V + 1])
        outs.append(o * lax.rsqrt(jnp.mean(o * o, axis=0, keepdims=True) + RMS_EPS) * go_ref[...]
                    * (1.0 - lam_init))
    o_ref[...] = jnp.concatenate(outs, axis=0).T.astype(BF16)


def _diff_attention(pa, batch, seq, gq, gk, lq1, lk1, lq2, lk2, g_out, lam_init):
    blk = min(ATT_BLOCK, seq)
    nb = seq // blk
    slopes = tuple(2.0 ** (-8.0 * (h + 1) / HEADS) for h in range(HEADS))
    kpos = np.zeros((HEADS, blk, ATT_KA - 2 * DA_QK), ml_dtypes.bfloat16)
    for h, s in enumerate(slopes):
        r = np.float32(s * LOG2E) * np.arange(blk, dtype=np.float32)
        for part in range(3):
            kpos[h, :, part] = r.astype(ml_dtypes.bfloat16)
            r = r - kpos[h, :, part].astype(np.float32)
    qt, k, vt = _da_prep(pa, batch, seq, blk, gq, gk, jnp.asarray(kpos))
    kh = k.reshape(batch, HEADS, nb, blk, ATT_KA)
    rel = jnp.arange(blk)[None, :] - jnp.arange(blk)[:, None]
    td = jnp.where(rel >= 0, 0.0, MASK_VALUE).astype(F32)
    lp = jnp.stack([lq1, lk1, lq2, lk2])
    go = jnp.broadcast_to(g_out[:, None], (HEAD_V, blk))
    body = functools.partial(_attn_body, lam_init=lam_init, slopes=slopes, blk=blk)
    return pl.pallas_call(
        body,
        grid=(batch, nb),
        in_specs=[
            pl.BlockSpec((1, 2, HEADS * ATT_KA, blk), lambda b, i: (b, 0, 0, i)),
            pl.BlockSpec((1, HEADS, nb, blk, ATT_KA), lambda b, i: (b, 0, 0, 0, 0)),
            pl.BlockSpec((1, HEADS, nb, ATT_VA, blk), lambda b, i: (b, 0, 0, 0, 0)),
            _const_spec((blk, blk)),
            _const_spec((4, DA_QK)), _const_spec((HEAD_V, blk))],
        out_specs=pl.BlockSpec((blk, GROUP), lambda b, i: (b * nb + i, 0)),
        out_shape=jax.ShapeDtypeStruct((batch * seq, GROUP), BF16),
        scratch_shapes=[pltpu.VMEM((2 * HEADS, 1, blk), F32), pltpu.VMEM((2 * HEADS, ATT_VA, blk), F32),
                        pltpu.VMEM((2, 2 * HEADS, blk, blk), F32)],
        compiler_params=_params("parallel", "arbitrary"),
        name="diff_attention",
    )(qt, kh, vt, td, lp, go)


def _gla_consts(tb, hk):
    c, sub, nc, kh = GLA_CHUNK, GLA_SUB, tb // GLA_CHUNK, hk // HEADS
    nsub = c // sub
    width = nsub * HEADS * sub
    t = jnp.arange(tb)
    lseg = ((t[:, None] // c == t[None, :] // c) & (t[None, :] <= t[:, None])).astype(BF16)
    r = jnp.arange(sub * hk)
    col = jnp.arange(width)
    ind = ((r[:, None] // hk == col[None, :] % sub)
           & ((r[:, None] % hk) // kh == (col[None, :] // sub) % HEADS)).astype(BF16)
    submask = ((t[:, None] % c) // sub == col[None, :] // (HEADS * sub)).astype(F32)
    cmask_a = (t[:, None] // c == jnp.arange(nc * width)[None, :] // width).astype(F32)
    cmask_q = (t[:, None] // c == jnp.arange(nc * hk)[None, :] // hk).astype(F32)
    vr = jnp.arange(nc * width)
    row_h = (vr[:, None] // sub) % HEADS
    vmask = (row_h == jnp.arange(GROUP)[None, :] // HEAD_V).astype(BF16)
    kmask = (((vr[:, None] % width) // (HEADS * sub) == 0)
             & (row_h == jnp.arange(hk)[None, :] // kh)).astype(BF16)
    bdmask = (jnp.arange(GROUP)[:, None] // HEAD_V == jnp.arange(hk)[None, :] // kh).astype(F32)
    return lseg, ind, submask, cmask_a, cmask_q, vmask, kmask, bdmask


class _GlaStream:
    def __init__(self, q, k, lf, v, consts, scratch):
        self.q, self.k, self.lf, self.v = q, k, lf, v
        (self.lseg, self.ind, self.submask, self.cmask_a, self.cmask_q, self.vmask, self.kmask,
         self.bdmask) = consts
        self.st, self.tcat, self.bk, self.stack = scratch


def _gla_core(streams):
    c, sub = GLA_CHUNK, GLA_SUB
    assert c == 2 * sub
    tb = streams[0].q.shape[0]
    nc = tb // c
    for s in streams:
        s.b = _dot_exact_x(s.lseg[...], s.lf, 3) * LOG2E
    yield
    for s in streams:
        hk = s.q.shape[1]
        s.bk[0] = s.b
        s.bk[1] = s.k
        s.row = lax.broadcasted_iota(jnp.int32, (tb, hk), 0)
        rowmod = s.row % sub
        for jj in range(sub):
            bj = _chunk_row_bcast(s.bk, 0, jj, sub)
            kj = _chunk_row_bcast(s.bk, 1, jj, sub)
            t = s.q * kj * jnp.exp2(s.b - bj)
            s.tcat[:, jj * hk:(jj + 1) * hk] = jnp.where(rowmod >= jj, t, 0.0).astype(BF16)
            if jj % 4 == 3:
                yield
    yield
    for s in streams:
        s.a = _dot(s.tcat[...], s.ind[...]) * s.submask[...]
    yield
    for s in streams:
        bref = _chunk_row_bcast(s.bk, 0, sub - 1, c)
        second = s.row % c >= sub
        s.qm = jnp.where(second, s.q * jnp.exp2(s.b - bref), 0.0).astype(BF16)
        kt = jnp.where(second, 0.0, s.k * jnp.exp2(bref - s.b)).astype(BF16)
        s.k_exp = jnp.concatenate([kt[ci * c:ci * c + sub] for ci in range(nc) for _ in range(2 * HEADS)],
                                  axis=0) * s.kmask[...]
    yield
    for s in streams:
        s.a_off = _dot_nt(s.qm, s.k_exp)
    yield
    for s in streams:
        s.a_exp = ((jnp.concatenate([s.a] * nc, axis=1) + s.a_off) * s.cmask_a[...]).astype(BF16)
        s.vb = s.v.astype(BF16)
        s.v_bd = jnp.concatenate([s.vb[ci * c + h * sub:ci * c + (h + 1) * sub]
                                  for ci in range(nc) for h in range(2) for _ in range(HEADS)],
                                 axis=0) * s.vmask[...]
    yield
    for s in streams:
        s.o = _dot(s.a_exp, s.v_bd)
    yield
    for s in streams:
        blast = _chunk_row_bcast(s.bk, 0, c - 1, c)
        qe = s.q * jnp.exp2(s.b)
        s.ke = (s.k * jnp.exp2(blast - s.b)).astype(BF16)
        s.q_exp = (jnp.concatenate([qe] * nc, axis=1) * s.cmask_q[...]).astype(BF16)
    yield
    for s in streams:
        s.pt = [_dot_tn(s.vb[ci * c:(ci + 1) * c], s.ke[ci * c:(ci + 1) * c]) * s.bdmask[...]
                for ci in range(nc)]
    yield
    for s in streams:
        hk = s.q.shape[1]
        st = s.st[...]
        for ci in range(nc):
            s.stack[:, ci * hk:(ci + 1) * hk] = st.astype(BF16)
            st = st * jnp.exp2(s.bk[0, ci * c + c - 1:ci * c + c, :]) + s.pt[ci]
        s.st[...] = st
    yield
    return [s.o + _dot_nt(s.q_exp, s.stack[...]) for s in streams]


def _gla_scratch(tb, hk):
    nc = tb // GLA_CHUNK
    return [pltpu.VMEM((GROUP, hk), F32), pltpu.VMEM((tb, GLA_SUB * hk), BF16),
            pltpu.VMEM((2, tb, hk), F32), pltpu.VMEM((GROUP, nc * hk), BF16)]


N_GLA_CONSTS = 8
N_GLA_SCRATCH = 4


def _gla_hgrn_body(pb_ref, pc_ref, gup_ref, gb_ref, gout_b_ref, lg_ref, gout_c_ref, g64_ref, *rest, layer):
    consts_b, consts_c = rest[:N_GLA_CONSTS], rest[N_GLA_CONSTS:2 * N_GLA_CONSTS]
    ob_ref, oc_ref = rest[2 * N_GLA_CONSTS:2 * N_GLA_CONSTS + 2]
    scratch = rest[2 * N_GLA_CONSTS + 2:]
    scratch_b, scratch_c = scratch[:N_GLA_SCRATCH], scratch[N_GLA_SCRATCH:]
    hk = HEADS * GLA_K
    gate_lo, og_lo = 2 * hk + GROUP, 2 * hk + GROUP + LANE
    logit = _dot(pb_ref[:, gate_lo:og_lo].astype(BF16), gup_ref[...]) + gb_ref[...]
    lf_b = (jnp.minimum(logit, 0.0) - jnp.log(1.0 + jnp.exp(-jnp.abs(logit)))) * (1.0 / GLA_NORMALIZER)
    gla = _GlaStream(pb_ref[:, 0:hk] * (GLA_K ** -0.5), pb_ref[:, hk:2 * hk], lf_b,
                     pb_ref[:, 2 * hk:2 * hk + GROUP], consts_b, scratch_b)
    lg = lg_ref[...]
    e = jnp.exp(lg - jnp.max(lg, axis=0, keepdims=True))
    p = e / jnp.sum(e, axis=0, keepdims=True)
    cs = p[0:1]
    for i in range(1, layer + 1):
        cs = cs + p[i:i + 1]
    lb = cs - p[0:1]
    z = pc_ref[:, GROUP:2 * GROUP]
    forget = lb + (1.0 - lb) * _sigmoid(z)
    hgrn = _GlaStream(_silu(pc_ref[:, 0:GROUP]), (1.0 - lb) * _sigmoid(-z),
                      jnp.log(jnp.maximum(forget, TINY)), pc_ref[:, 2 * GROUP:3 * GROUP], consts_c, scratch_c)
    o_b, o_c = yield from _gla_core([gla, hgrn])
    for o, og, gout_ref, o_ref in ((o_b, pb_ref[:, og_lo:og_lo + GROUP], gout_b_ref, ob_ref),
                                   (o_c, pc_ref[:, 3 * GROUP:4 * GROUP], gout_c_ref, oc_ref)):
        ms = _dot_x_exact(o * o, g64_ref[...], 2)
        o_ref[...] = (o * lax.rsqrt(ms + RMS_EPS) * gout_ref[...] * _silu(og)).astype(BF16)


def _gla_hgrn_operands(tb, gate_up, gate_b, g_out_b, lb_logits, g_out_c):
    hk_b, hk_c = HEADS * GLA_K, HEADS * HG_K
    gup = jnp.zeros((LANE, hk_b), F32).at[:GLA_RANK].set(gate_up).astype(BF16)
    tile = lambda g: jnp.tile(g, HEADS).reshape(1, GROUP)
    extra = [gup, gate_b.reshape(1, hk_b), tile(g_out_b), lb_logits, tile(g_out_c),
             _group_mean_matrix(GROUP, HEAD_V), *_gla_consts(tb, hk_b), *_gla_consts(tb, hk_c)]
    return extra, _gla_scratch(tb, hk_b) + _gla_scratch(tb, hk_c)


def _rwkv_consts(tb):
    c, nch = RW_CHUNK, tb // RW_CHUNK
    t = jnp.arange(tb)
    same = t[:, None] // c == t[None, :] // c
    lseg = (same & (t[None, :] <= t[:, None])).astype(BF16)
    strict = (same & (t[None, :] < t[:, None])).astype(F32)
    incl = (same & (t[None, :] <= t[:, None])).astype(F32)
    eye = jnp.eye(tb, dtype=F32)
    hmask = (jnp.arange(HEADS)[:, None] == jnp.arange(GROUP)[None, :] // HEAD_V).astype(F32)
    bd = (jnp.arange(GROUP)[:, None] // HEAD_V == jnp.arange(GROUP)[None, :] // HEAD_V).astype(F32)
    cmask = (t[:, None] // c == jnp.arange(nch * GROUP)[None, :] // GROUP).astype(F32)
    eye_g = jnp.eye(GROUP, dtype=F32)
    return lseg, strict, incl, eye, hmask, bd, cmask, eye_g


def _rwkv_core(r, logw, k2, v, kk, a, consts, mt_ref, gam_ref, stack_ref):
    lseg_ref, strict_ref, incl_ref, eye_ref, hmask_ref, bd_ref, cmask_ref, eyeg_ref = consts
    tb = r.shape[0]
    c = RW_CHUNK
    nch = tb // c
    gam = _dot_exact_x(lseg_ref[...], logw, 3)
    yield
    gam_ref[0] = gam
    glast = _chunk_row_bcast(gam_ref, 0, c - 1, c)
    eng = jnp.exp(-gam)
    ecl = jnp.exp(glast - gam)
    beta = a * kk
    abar = kk * jnp.exp(gam - logw)
    rbar = r * jnp.exp(gam)
    yb = jnp.concatenate([beta * eng, k2 * eng], axis=0).astype(BF16)
    bhat = beta * ecl
    khat = k2 * ecl
    strict, incl = strict_ref[...] > 0.0, incl_ref[...] > 0.0
    eye = eye_ref[...]
    heads = range(HEADS)
    hms = [hmask_ref[h:h + 1, :] for h in heads]
    abar_h = [abar * hms[h] for h in heads]
    vh = [(v * hms[h]).astype(BF16) for h in heads]
    pw, a_rb, a_rk, z = [], [], [], []
    for h in heads:
        xh = jnp.concatenate([abar_h[h], rbar * hms[h]], axis=0).astype(BF16)
        s = _dot_nt(xh, yb)
        pw.append(jnp.where(strict, -s[:tb, :tb], 0.0))
        a_ak = jnp.where(strict, s[:tb, tb:], 0.0).astype(BF16)
        a_rb.append(jnp.where(incl, s[tb:, :tb], 0.0).astype(BF16))
        a_rk.append(jnp.where(incl, s[tb:, tb:], 0.0).astype(BF16))
        z.append(jnp.concatenate([abar_h[h], _dot(a_ak, vh[h])], axis=1).astype(BF16))
        yield
    tinv = [eye + pw[h] for h in heads]
    for _ in range(int(math.log2(c)) - 1):
        pwb = [pw[h].astype(BF16) for h in heads]
        pw = [_dot(pwb[h], pwb[h]) for h in heads]
        yield
        tinv = [tinv[h] + _dot(tinv[h].astype(BF16), pw[h].astype(BF16)) for h in heads]
        yield
    wu_h = [_dot(tinv[h].astype(BF16), z[h]) for h in heads]
    yield
    om_h = [_dot(a_rb[h], wu_h[h].astype(BF16)) for h in heads]
    arkv_h = [_dot(a_rk[h], vh[h]) for h in heads]
    yield
    wu, om, arkv = sum(wu_h[1:], wu_h[0]), sum(om_h[1:], om_h[0]), sum(arkv_h[1:], arkv_h[0])
    wbar, u0 = wu[:, :GROUP], wu[:, GROUP:]
    omega = rbar - om[:, :GROUP]
    y0 = arkv - om[:, GROUP:]
    wb, bhb = wbar.astype(BF16), bhat.astype(BF16)
    vb, nub, khb = v.astype(BF16), (-u0).astype(BF16), khat.astype(BF16)
    x_c, psi = [], []
    for ci in range(nch):
        rows = slice(ci * c, (ci + 1) * c)
        x_c.append((_dot_tn(wb[rows], bhb[rows]) * bd_ref[...]).astype(BF16))
        psi.append(_dot_tn(jnp.concatenate([vb[rows], nub[rows]], axis=0),
                           jnp.concatenate([khb[rows], bhb[rows]], axis=0)) * bd_ref[...])
    yield
    mt = mt_ref[...]
    for ci in range(nch):
        stack_ref[:, ci * GROUP:(ci + 1) * GROUP] = mt.astype(BF16)
        gl = jnp.exp(gam_ref[0, ci * c + c - 1:ci * c + c, :])
        mt = mt * gl - _dot(mt.astype(BF16), x_c[ci]) + psi[ci]
        yield
    mt_ref[...] = mt
    om_exp = (jnp.concatenate([omega] * nch, axis=1) * cmask_ref[...]).astype(BF16)
    return y0 + _dot_nt(om_exp, stack_ref[...])


def _rwkv_body(pd_ref, mu_ref, w0_ref, wup_ref, a0_ref, aup_ref, gup_ref, kk_ref, ka_ref, rk_ref,
               lng_ref, lnb_ref, g64_ref, s64_ref, *rest):
    consts, (o_ref, mt_ref, carry_ref, gam_ref, stack_ref) = rest[:8], rest[8:]
    tb = pd_ref.shape[0]
    p = pd_ref[...]
    rowid = lax.broadcasted_iota(jnp.int32, p.shape, 0)
    prev = jnp.where(rowid == 0, carry_ref[...], pltpu.roll(p, 1, axis=0))
    carry_ref[...] = p[tb - 1:tb, :]
    p = p + mu_ref[...] * (prev - p)
    r, k, v = p[:, 0:GROUP], p[:, GROUP:2 * GROUP], p[:, 2 * GROUP:3 * GROUP]
    low = p[:, 3 * GROUP:3 * GROUP + LANE]
    logw = -RW_DECAY_SCALE * _sigmoid(w0_ref[...] + _dot(jnp.tanh(low).astype(BF16), wup_ref[...]))
    a = _sigmoid(a0_ref[...] + _dot(low.astype(BF16), aup_ref[...]))
    g = _dot(_sigmoid(low).astype(BF16), gup_ref[...])
    kk = k * kk_ref[...]
    kk = kk * lax.rsqrt(jnp.maximum(_dot_x_exact(kk * kk, s64_ref[...], 2), 1e-24))
    k2 = k * (1.0 + (a - 1.0) * ka_ref[...])
    yield
    y = yield from _rwkv_core(r, logw, k2, v, kk, a, consts, mt_ref, gam_ref, stack_ref)
    yield
    mean = _dot_x_exact(y, g64_ref[...], 2)
    d = y - mean
    yield
    var = _dot_x_exact(d * d, g64_ref[...], 2)
    yn = d * lax.rsqrt(var + RW_LN_EPS) * lng_ref[...] + lnb_ref[...]
    bonus = _dot_x_exact(r * k2 * rk_ref[...], s64_ref[...], 2) * v
    o_ref[...] = ((yn + bonus) * g).astype(BF16)


def _rwkv_operands(tb, mu, w0, w_up, a0, a_up, g_up, k_k, k_a, r_k, ln_g, ln_b):
    nch = tb // RW_CHUNK
    row = lambda z: z.reshape(1, -1)
    low = lambda w, start: jnp.zeros((LANE, GROUP), F32).at[start:start + w.shape[0]].set(w).astype(BF16)
    extra = [row(mu), row(w0), low(w_up, 0), row(a0), low(a_up, 32), low(g_up, 64), row(k_k), row(k_a),
             row(r_k), row(ln_g), row(ln_b), _group_mean_matrix(GROUP, HEAD_V),
             _group_sum_matrix(GROUP, HEAD_V), *_rwkv_consts(tb)]
    scratch = [pltpu.VMEM((GROUP, GROUP), F32), pltpu.VMEM((1, D_W), F32),
               pltpu.VMEM((1, tb, GROUP), F32), pltpu.VMEM((GROUP, nch * GROUP), BF16)]
    return extra, scratch


def _recurrent_body(*refs, n_rw, n_gh, n_rw_scratch, layer):
    pd_ref, pb_ref, pc_ref = refs[:3]
    rw_extra = refs[3:3 + n_rw]
    gh_extra = refs[3 + n_rw:3 + n_rw + n_gh]
    od_ref, ob_ref, oc_ref = refs[3 + n_rw + n_gh:6 + n_rw + n_gh]
    scratch = refs[6 + n_rw + n_gh:]
    gh_scratch = scratch[n_rw_scratch:]

    @pl.when(pl.program_id(1) == 0)
    def _():
        for ref in (scratch[0], scratch[1], gh_scratch[0], gh_scratch[N_GLA_SCRATCH]):
            ref[...] = jnp.zeros_like(ref)

    live = [_rwkv_body(pd_ref, *rw_extra, od_ref, *scratch[:n_rw_scratch]),
            _gla_hgrn_body(pb_ref, pc_ref, *gh_extra, ob_ref, oc_ref, *gh_scratch, layer=layer)]
    while live:
        for gen in list(live):
            if next(gen, StopIteration) is StopIteration:
                live.remove(gen)


def _recurrent_mixers(pb, pc, pd, batch, seq, gla_hgrn_params, rwkv_params, layer):
    tb = min(TIME_BLOCK, seq)
    nt = seq // tb
    rw_extra, rw_scratch = _rwkv_operands(tb, *rwkv_params)
    gh_extra, gh_scratch = _gla_hgrn_operands(tb, *gla_hgrn_params)
    row = lambda width: pl.BlockSpec((tb, width), lambda b, t: (b * nt + t, 0))
    body = functools.partial(_recurrent_body, n_rw=len(rw_extra), n_gh=len(gh_extra),
                             n_rw_scratch=len(rw_scratch), layer=layer)
    o_d, o_b, o_c = pl.pallas_call(
        body,
        grid=(batch, nt),
        in_specs=[row(D_W), row(B_W), row(C_W)] + [_const_spec(e.shape) for e in rw_extra + gh_extra],
        out_specs=[row(GROUP)] * 3,
        out_shape=[jax.ShapeDtypeStruct((batch * seq, GROUP), BF16)] * 3,
        scratch_shapes=rw_scratch + gh_scratch,
        compiler_params=_params("parallel", "arbitrary"),
        name="recurrent_mixers",
    )(pd, pb, pc, *rw_extra, *gh_extra)
    return o_b, o_c, o_d


def _relayout_w_in(w_in):
    a_end, b_end, c_end = 768, 768 + 784, 768 + 784 + 1024
    wb = w_in[:, a_end:b_end]
    pad = jnp.zeros((w_in.shape[0], LANE - GLA_RANK), w_in.dtype)
    wb = jnp.concatenate([wb[:, :528], pad, wb[:, 528:]], axis=1)
    return jnp.concatenate([w_in[:, :a_end], wb, w_in[:, b_end:c_end], w_in[:, c_end:]], axis=1).astype(BF16)


def kernel(x, norm_mix_g, w_in, da_q_norm_g, da_k_norm_g, da_lambda_q1, da_lambda_k1, da_lambda_q2, da_lambda_k2, da_out_norm_g, gla_gate_up, gla_gate_b, gla_out_norm_g, hgrn_lb_logits, hgrn_out_norm_g, rw_shift_mu, rw_w0, rw_w_up, rw_a0, rw_a_up, rw_g_up, rw_k_k, rw_k_a, rw_r_k, rw_ln_g, rw_ln_b, w_out, norm_mlp_g, w_mlp_up, w_mlp_down):
    batch, seq, _ = x.shape
    xf = x.reshape(batch * seq, D_MODEL)
    for l in range(DEPTH):
        lam_init = 0.8 - 0.6 * math.exp(-0.3 * l)
        pa, pb, pc, pd = _in_proj(xf, norm_mix_g[l], _relayout_w_in(w_in[l]))
        o_a = _diff_attention(pa, batch, seq, da_q_norm_g[l], da_k_norm_g[l], da_lambda_q1[l],
                              da_lambda_k1[l], da_lambda_q2[l], da_lambda_k2[l], da_out_norm_g[l], lam_init)
        o_b, o_c, o_d = _recurrent_mixers(
            pb, pc, pd, batch, seq,
            (gla_gate_up[l], gla_gate_b[l], gla_out_norm_g[l], hgrn_lb_logits, hgrn_out_norm_g[l]),
            (rw_shift_mu[l], rw_w0[l], rw_w_up[l], rw_a0[l], rw_a_up[l], rw_g_up[l], rw_k_k[l], rw_k_a[l],
             rw_r_k[l].reshape(-1), rw_ln_g[l], rw_ln_b[l]), l)
        xf = _post(xf, (o_a, o_b, o_c, o_d), w_out[l].astype(BF16), norm_mlp_g[l],
                   w_mlp_up[l].astype(BF16), w_mlp_down[l].astype(BF16))
    return xf.reshape(batch, seq, D_MODEL)
```

```python
import functools
import math

import jax
import jax.numpy as jnp
import ml_dtypes
import numpy as np
from jax import lax
from jax.experimental import pallas as pl
from jax.experimental.pallas import tpu as pltpu

F32, BF16 = jnp.float32, jnp.bfloat16

D_MODEL = 1024
GROUP = 256
HEADS = 4
HEAD_V = 64
DA_QK = 32
GLA_K = 32
GLA_RANK = 16
GLA_NORMALIZER = 16.0
HG_K = 64
RW_DECAY_SCALE = 0.606531
RW_LN_EPS = 64e-5
D_FF = 4 * D_MODEL
RMS_EPS = 1e-6
MASK_VALUE = -1e30
TINY = 1e-30
DEPTH = 4

LANE = 128
A_W, B_W, C_W, D_W = 768, 896, 1024, 896
P_W = A_W + B_W + C_W + D_W

ROW_TILE = 512
TIME_BLOCK = 256
ATT_BLOCK = 256
ATT_VA = 80
ATT_KA = 80
LOG2E = 1.4426950408889634
GLA_CHUNK = 32
GLA_SUB = 16
RW_CHUNK = 64
VMEM_LIMIT = 56 * 1024 * 1024


def _dot(a, b):
    return jnp.dot(a, b, preferred_element_type=F32)


def _dot_nt(a, b):
    return lax.dot_general(a, b, (((1,), (1,)), ((), ())), preferred_element_type=F32)


def _dot_tn(a, b):
    return lax.dot_general(a, b, (((0,), (0,)), ((), ())), preferred_element_type=F32)


def _split(x, parts):
    out, r = [], x
    for i in range(parts):
        h = r.astype(BF16)
        out.append(h)
        if i + 1 < parts:
            r = r - h.astype(F32)
    return out


def _dot_x_exact(x, m, parts):
    acc = None
    for h in _split(x, parts):
        d = _dot(h, m)
        acc = d if acc is None else acc + d
    return acc


def _dot_exact_x(m, x, parts):
    acc = None
    for h in _split(x, parts):
        d = _dot(m, h)
        acc = d if acc is None else acc + d
    return acc


def _sigmoid(x):
    return 1.0 / (1.0 + jnp.exp(-x))


def _silu(x):
    return x * _sigmoid(x)


def _const_spec(shape):
    zeros = (0,) * len(shape)
    return pl.BlockSpec(shape, lambda *_: zeros)


def _chunk_row_bcast(ref, idx, row, chunk):
    n_rows, width = ref.shape[1], ref.shape[2]
    pieces = []
    for c in range(n_rows // chunk):
        r = c * chunk + row
        pieces.append(jnp.broadcast_to(ref[idx, r:r + 1, :], (chunk, width)))
    return jnp.concatenate(pieces, axis=0)


def _params(*sem):
    return pltpu.CompilerParams(dimension_semantics=sem, vmem_limit_bytes=VMEM_LIMIT)


def _in_proj_body(x_ref, g_ref, w_ref, gq_ref, gk_ref, g32_ref, kpos_ref,
                  qt_ref, k_ref, vt_ref, pb_ref, pc_ref, pd_ref):
    x = x_ref[...]
    h = (x * lax.rsqrt(jnp.mean(x * x, axis=-1, keepdims=True) + RMS_EPS) * g_ref[...]).astype(BF16)
    pa = _dot(h, w_ref[:, 0:A_W])
    q, k = pa[:, 0:GROUP], pa[:, GROUP:2 * GROUP]
    qm = _dot_x_exact(q * q, g32_ref[...], 2)
    km = _dot_x_exact(k * k, g32_ref[...], 2)
    off = A_W
    for o_ref in (pb_ref, pc_ref, pd_ref):
        n = o_ref.shape[1]
        o_ref[...] = _dot(h, w_ref[:, off:off + n])
        off += n
    _attention_layouts(pa, qm, km, gq_ref, gk_ref, kpos_ref, qt_ref, k_ref, vt_ref)


def _in_proj(x, g, w, batch, seq, blk, gq, gk, kpos):
    tm = min(ROW_TILE, seq)
    nt = seq // tm
    n = batch * seq
    row = lambda width: pl.BlockSpec((tm, width), lambda b, t: (b * nt + t, 0))
    tile = lambda gain: jnp.tile(gain, GROUP // DA_QK).reshape(1, GROUP)
    return pl.pallas_call(
        _in_proj_body,
        grid=(batch, nt),
        in_specs=[row(D_MODEL), _const_spec((1, D_MODEL)), _const_spec((D_MODEL, P_W)),
                  _const_spec((1, GROUP)), _const_spec((1, GROUP)), _const_spec((GROUP, GROUP)),
                  _const_spec(kpos.shape)],
        out_specs=[pl.BlockSpec((1, 2, HEADS * ATT_KA, tm), lambda b, t: (b, 0, 0, t)),
                   pl.BlockSpec((1, HEADS, tm, ATT_KA), lambda b, t: (b, 0, t, 0)),
                   pl.BlockSpec((1, HEADS, tm // blk, ATT_VA, blk), lambda b, t: (b, 0, t, 0, 0)),
                   row(B_W), row(C_W), row(D_W)],
        out_shape=[jax.ShapeDtypeStruct((batch, 2, HEADS * ATT_KA, seq), BF16),
                   jax.ShapeDtypeStruct((batch, HEADS, seq, ATT_KA), BF16),
                   jax.ShapeDtypeStruct((batch, HEADS, seq // blk, ATT_VA, blk), BF16)]
                  + [jax.ShapeDtypeStruct((n, wd), F32) for wd in (B_W, C_W, D_W)],
        compiler_params=_params("parallel", "parallel"),
        name="in_proj",
    )(x, g.reshape(1, D_MODEL), w, tile(gq), tile(gk), _group_mean_matrix(GROUP, DA_QK), kpos)


def _post_body(x_ref, oa_ref, ob_ref, oc_ref, od_ref, wo_ref, g_ref, wu_ref, wd_ref, out_ref):
    mixed = jnp.concatenate([r[...] for r in (oa_ref, ob_ref, oc_ref, od_ref)], axis=1)
    x1 = x_ref[...] + _dot(mixed, wo_ref[...])
    h = (x1 * lax.rsqrt(jnp.mean(x1 * x1, axis=-1, keepdims=True) + RMS_EPS) * g_ref[...]).astype(BF16)
    y = x1
    for f in range(D_FF // D_MODEL):
        cols = slice(f * D_MODEL, (f + 1) * D_MODEL)
        u = jnp.square(jnp.maximum(_dot(h, wu_ref[:, cols]), 0.0)).astype(BF16)
        y = y + _dot(u, wd_ref[cols, :])
    out_ref[...] = y


def _post(x, outs, w_out, g, w_up, w_down):
    n = x.shape[0]
    tm = min(ROW_TILE, n)
    row = lambda width: pl.BlockSpec((tm, width), lambda i: (i, 0))
    return pl.pallas_call(
        _post_body,
        grid=(n // tm,),
        in_specs=[row(D_MODEL)] + [row(GROUP)] * 4 + [
            _const_spec((D_MODEL, D_MODEL)), _const_spec((1, D_MODEL)),
            _const_spec((D_MODEL, D_FF)), _const_spec((D_FF, D_MODEL))],
        out_specs=row(D_MODEL),
        out_shape=jax.ShapeDtypeStruct((n, D_MODEL), F32),
        compiler_params=_params("parallel"),
        name="post",
    )(x, *outs, w_out, g.reshape(1, D_MODEL), w_up, w_down)


def _group_mean_matrix(width, group):
    i = jnp.arange(width) // group
    return ((i[:, None] == i[None, :]).astype(F32) / group).astype(BF16)


def _group_sum_matrix(width, group):
    i = jnp.arange(width) // group
    return (i[:, None] == i[None, :]).astype(BF16)


def _attention_layouts(pa, qm, km, gq_ref, gk_ref, kpos_ref, qt_ref, k_ref, vt_ref):
    tm = pa.shape[0]
    blk = vt_ref.shape[-1]
    q, k = pa[:, 0:GROUP], pa[:, GROUP:2 * GROUP]
    q_t = (q * lax.rsqrt(qm + RMS_EPS) * gq_ref[...] * (DA_QK ** -0.5 * LOG2E)).T
    comp = (lax.broadcasted_iota(jnp.int32, q_t.shape, 0) // DA_QK) % 2
    n_extra = ATT_KA - 2 * DA_QK
    q_extra = (lax.broadcasted_iota(jnp.int32, (n_extra, tm), 0) < 3).astype(BF16)
    for c in range(2):
        q_c = jnp.where(comp == c, q_t, 0.0).astype(BF16)
        for h in range(HEADS):
            qt_ref[0, c, h * ATT_KA:h * ATT_KA + 2 * DA_QK, :] = q_c[h * HEAD_V:(h + 1) * HEAD_V]
            qt_ref[0, c, h * ATT_KA + 2 * DA_QK:(h + 1) * ATT_KA, :] = q_extra
    kn = (k * lax.rsqrt(km + RMS_EPS) * gk_ref[...]).astype(BF16)
    v_t = pa[:, 2 * GROUP:3 * GROUP].T.astype(BF16)
    ones_row = (lax.broadcasted_iota(jnp.int32, (ATT_VA - HEAD_V, blk), 0) == 0).astype(BF16)
    for h in range(HEADS):
        k_ref[0, h, :, 0:2 * DA_QK] = kn[:, h * HEAD_V:(h + 1) * HEAD_V]
        for sb in range(tm // blk):
            k_ref[0, h, sb * blk:(sb + 1) * blk, 2 * DA_QK:ATT_KA] = kpos_ref[h]
        for sb in range(tm // blk):
            vt_ref[0, h, sb, 0:HEAD_V, :] = v_t[h * HEAD_V:(h + 1) * HEAD_V, sb * blk:(sb + 1) * blk]
            vt_ref[0, h, sb, HEAD_V:ATT_VA, :] = ones_row


def _attn_body(qt_ref, k_ref, vt_ref, td_ref, lp_ref, go_ref, o_ref, m_ref, acc_ref, s_ref, *,
               lam_init, slopes, blk):
    i = pl.program_id(1)
    chains = [(h, c) for h in range(HEADS) for c in range(2)]

    def stage(cur, cur_slot, first, nxt=None, nxt_slot=None):
        for n, (h, c) in enumerate(chains):
            if nxt is not None:
                q_hc = qt_ref[0, c, h * ATT_KA:(h + 1) * ATT_KA, :]
                s_ref[nxt_slot, n] = _dot(k_ref[0, h, nxt], q_hc)
            if cur is None:
                continue
            off = -cur.astype(F32) * (slopes[h] * LOG2E * blk)
            s = s_ref[cur_slot, n]
            if first:
                s = s + td_ref[...]
            smax = jnp.max(s, axis=0, keepdims=True) - off
            if first:
                mx = smax
            else:
                m = m_ref[n]
                mx = jnp.maximum(m, smax)
            m_ref[n] = mx
            p = jnp.exp2(s - (mx + off)).astype(BF16)
            pv = _dot(vt_ref[0, h, cur], p)
            acc_ref[n] = pv if first else jnp.exp2(m - mx) * acc_ref[n] + pv

    stage(None, None, False, nxt=i, nxt_slot=0)
    stage(i, 0, True, nxt=0, nxt_slot=1)

    def body(u, carry):
        stage(2 * u, 1, False, nxt=2 * u + 1, nxt_slot=0)
        stage(2 * u + 1, 0, False, nxt=2 * u + 2, nxt_slot=1)
        return carry

    lax.fori_loop(0, (i - 1) // 2, body, 0)
    odd = lax.rem(i, 2) == 1

    @pl.when(jnp.logical_and(jnp.logical_not(odd), i > 0))
    def _():
        stage(i - 2, 1, False, nxt=i - 1, nxt_slot=0)
        stage(i - 1, 0, False)

    @pl.when(odd)
    def _():
        stage(i - 1, 1, False)

    lp = lp_ref[...]
    lam = (jnp.exp(jnp.sum(lp[0:1] * lp[1:2], axis=-1, keepdims=True))
           - jnp.exp(jnp.sum(lp[2:3] * lp[3:4], axis=-1, keepdims=True)) + lam_init)
    outs = []
    for h in range(HEADS):
        a1, a2 = acc_ref[2 * h], acc_ref[2 * h + 1]
        o = a1[0:HEAD_V] / a1[HEAD_V:HEAD_V + 1] - lam * (a2[0:HEAD_V] / a2[HEAD_V:HEAD_V + 1])
        outs.append(o * lax.rsqrt(jnp.mean(o * o, axis=0, keepdims=True) + RMS_EPS) * go_ref[...]
                    * (1.0 - lam_init))
    o_ref[...] = jnp.concatenate(outs, axis=0).T.astype(BF16)


def _alibi_slopes():
    return tuple(2.0 ** (-8.0 * (h + 1) / HEADS) for h in range(HEADS))


def _alibi_key_terms(blk):
    kpos = np.zeros((HEADS, blk, ATT_KA - 2 * DA_QK), ml_dtypes.bfloat16)
    for h, s in enumerate(_alibi_slopes()):
        r = np.float32(s * LOG2E) * np.arange(blk, dtype=np.float32)
        for part in range(3):
            kpos[h, :, part] = r.astype(ml_dtypes.bfloat16)
            r = r - kpos[h, :, part].astype(np.float32)
    return jnp.asarray(kpos)


def _diff_attention(qt, k, vt, batch, seq, blk, lq1, lk1, lq2, lk2, g_out, lam_init):
    nb = seq // blk
    slopes = _alibi_slopes()
    kh = k.reshape(batch, HEADS, nb, blk, ATT_KA)
    rel = jnp.arange(blk)[None, :] - jnp.arange(blk)[:, None]
    td = jnp.where(rel >= 0, 0.0, MASK_VALUE).astype(F32)
    lp = jnp.stack([lq1, lk1, lq2, lk2])
    go = jnp.broadcast_to(g_out[:, None], (HEAD_V, blk))
    body = functools.partial(_attn_body, lam_init=lam_init, slopes=slopes, blk=blk)
    return pl.pallas_call(
        body,
        grid=(batch, nb),
        in_specs=[
            pl.BlockSpec((1, 2, HEADS * ATT_KA, blk), lambda b, i: (b, 0, 0, i)),
            pl.BlockSpec((1, HEADS, nb, blk, ATT_KA), lambda b, i: (b, 0, 0, 0, 0)),
            pl.BlockSpec((1, HEADS, nb, ATT_VA, blk), lambda b, i: (b, 0, 0, 0, 0)),
            _const_spec((blk, blk)),
            _const_spec((4, DA_QK)), _const_spec((HEAD_V, blk))],
        out_specs=pl.BlockSpec((blk, GROUP), lambda b, i: (b * nb + i, 0)),
        out_shape=jax.ShapeDtypeStruct((batch * seq, GROUP), BF16),
        scratch_shapes=[pltpu.VMEM((2 * HEADS, 1, blk), F32), pltpu.VMEM((2 * HEADS, ATT_VA, blk), F32),
                        pltpu.VMEM((2, 2 * HEADS, blk, blk), F32)],
        compiler_params=_params("parallel", "arbitrary"),
        name="diff_attention",
    )(qt, kh, vt, td, lp, go)


def _gla_consts(tb, hk):
    c, sub, nc, kh = GLA_CHUNK, GLA_SUB, tb // GLA_CHUNK, hk // HEADS
    nsub = c // sub
    width = nsub * HEADS * sub
    t = jnp.arange(tb)
    lseg = ((t[:, None] // c == t[None, :] // c) & (t[None, :] <= t[:, None])).astype(BF16)
    r = jnp.arange(sub * hk)
    col = jnp.arange(width)
    ind = ((r[:, None] // hk == col[None, :] % sub)
           & ((r[:, None] % hk) // kh == (col[None, :] // sub) % HEADS)).astype(BF16)
    submask = ((t[:, None] % c) // sub == col[None, :] // (HEADS * sub)).astype(F32)
    cmask_a = (t[:, None] // c == jnp.arange(nc * width)[None, :] // width).astype(F32)
    cmask_q = (t[:, None] // c == jnp.arange(nc * hk)[None, :] // hk).astype(F32)
    vr = jnp.arange(nc * width)
    row_h = (vr[:, None] // sub) % HEADS
    vmask = (row_h == jnp.arange(GROUP)[None, :] // HEAD_V).astype(BF16)
    kmask = (((vr[:, None] % width) // (HEADS * sub) == 0)
             & (row_h == jnp.arange(hk)[None, :] // kh)).astype(BF16)
    bdmask = (jnp.arange(GROUP)[:, None] // HEAD_V == jnp.arange(hk)[None, :] // kh).astype(F32)
    return lseg, ind, submask, cmask_a, cmask_q, vmask, kmask, bdmask


class _GlaStream:
    def __init__(self, q, k, lf, v, consts, scratch):
        self.q, self.k, self.lf, self.v = q, k, lf, v
        (self.lseg, self.ind, self.submask, self.cmask_a, self.cmask_q, self.vmask, self.kmask,
         self.bdmask) = consts
        self.st, self.tcat, self.bk, self.stack = scratch


def _gla_core(streams):
    c, sub = GLA_CHUNK, GLA_SUB
    assert c == 2 * sub
    tb = streams[0].q.shape[0]
    nc = tb // c
    for s in streams:
        s.b = _dot_exact_x(s.lseg[...], s.lf, 3) * LOG2E
    yield
    for s in streams:
        hk = s.q.shape[1]
        s.bk[0] = s.b
        s.bk[1] = s.k
        s.row = lax.broadcasted_iota(jnp.int32, (tb, hk), 0)
        rowmod = s.row % sub
        for jj in range(sub):
            bj = _chunk_row_bcast(s.bk, 0, jj, sub)
            kj = _chunk_row_bcast(s.bk, 1, jj, sub)
            t = s.q * kj * jnp.exp2(s.b - bj)
            s.tcat[:, jj * hk:(jj + 1) * hk] = jnp.where(rowmod >= jj, t, 0.0).astype(BF16)
            if jj % 4 == 3:
                yield
    yield
    for s in streams:
        s.a = _dot(s.tcat[...], s.ind[...]) * s.submask[...]
    yield
    for s in streams:
        bref = _chunk_row_bcast(s.bk, 0, sub - 1, c)
        second = s.row % c >= sub
        s.qm = jnp.where(second, s.q * jnp.exp2(s.b - bref), 0.0).astype(BF16)
        kt = jnp.where(second, 0.0, s.k * jnp.exp2(bref - s.b)).astype(BF16)
        s.k_exp = jnp.concatenate([kt[ci * c:ci * c + sub] for ci in range(nc) for _ in range(2 * HEADS)],
                                  axis=0) * s.kmask[...]
    yield
    for s in streams:
        s.a_off = _dot_nt(s.qm, s.k_exp)
    yield
    for s in streams:
        s.a_exp = ((jnp.concatenate([s.a] * nc, axis=1) + s.a_off) * s.cmask_a[...]).astype(BF16)
        s.vb = s.v.astype(BF16)
        s.v_bd = jnp.concatenate([s.vb[ci * c + h * sub:ci * c + (h + 1) * sub]
                                  for ci in range(nc) for h in range(2) for _ in range(HEADS)],
                                 axis=0) * s.vmask[...]
    yield
    for s in streams:
        s.o = _dot(s.a_exp, s.v_bd)
    yield
    for s in streams:
        blast = _chunk_row_bcast(s.bk, 0, c - 1, c)
        qe = s.q * jnp.exp2(s.b)
        s.ke = (s.k * jnp.exp2(blast - s.b)).astype(BF16)
        s.q_exp = (jnp.concatenate([qe] * nc, axis=1) * s.cmask_q[...]).astype(BF16)
    yield
    for s in streams:
        s.pt = [_dot_tn(s.vb[ci * c:(ci + 1) * c], s.ke[ci * c:(ci + 1) * c]) * s.bdmask[...]
                for ci in range(nc)]
    yield
    for s in streams:
        hk = s.q.shape[1]
        st = s.st[...]
        for ci in range(nc):
            s.stack[:, ci * hk:(ci + 1) * hk] = st.astype(BF16)
            st = st * jnp.exp2(s.bk[0, ci * c + c - 1:ci * c + c, :]) + s.pt[ci]
        s.st[...] = st
    yield
    return [s.o + _dot_nt(s.q_exp, s.stack[...]) for s in streams]


def _gla_scratch(tb, hk):
    nc = tb // GLA_CHUNK
    return [pltpu.VMEM((GROUP, hk), F32), pltpu.VMEM((tb, GLA_SUB * hk), BF16),
            pltpu.VMEM((2, tb, hk), F32), pltpu.VMEM((GROUP, nc * hk), BF16)]


N_GLA_CONSTS = 8
N_GLA_SCRATCH = 4


def _gla_hgrn_body(pb_ref, pc_ref, gup_ref, gb_ref, gout_b_ref, lg_ref, gout_c_ref, g64_ref, *rest, layer):
    consts_b, consts_c = rest[:N_GLA_CONSTS], rest[N_GLA_CONSTS:2 * N_GLA_CONSTS]
    ob_ref, oc_ref = rest[2 * N_GLA_CONSTS:2 * N_GLA_CONSTS + 2]
    scratch = rest[2 * N_GLA_CONSTS + 2:]
    scratch_b, scratch_c = scratch[:N_GLA_SCRATCH], scratch[N_GLA_SCRATCH:]
    hk = HEADS * GLA_K
    gate_lo, og_lo = 2 * hk + GROUP, 2 * hk + GROUP + LANE
    logit = _dot(pb_ref[:, gate_lo:og_lo].astype(BF16), gup_ref[...]) + gb_ref[...]
    lf_b = (jnp.minimum(logit, 0.0) - jnp.log(1.0 + jnp.exp(-jnp.abs(logit)))) * (1.0 / GLA_NORMALIZER)
    gla = _GlaStream(pb_ref[:, 0:hk] * (GLA_K ** -0.5), pb_ref[:, hk:2 * hk], lf_b,
                     pb_ref[:, 2 * hk:2 * hk + GROUP], consts_b, scratch_b)
    lg = lg_ref[...]
    e = jnp.exp(lg - jnp.max(lg, axis=0, keepdims=True))
    p = e / jnp.sum(e, axis=0, keepdims=True)
    cs = p[0:1]
    for i in range(1, layer + 1):
        cs = cs + p[i:i + 1]
    lb = cs - p[0:1]
    z = pc_ref[:, GROUP:2 * GROUP]
    forget = lb + (1.0 - lb) * _sigmoid(z)
    hgrn = _GlaStream(_silu(pc_ref[:, 0:GROUP]), (1.0 - lb) * _sigmoid(-z),
                      jnp.log(jnp.maximum(forget, TINY)), pc_ref[:, 2 * GROUP:3 * GROUP], consts_c, scratch_c)
    o_b, o_c = yield from _gla_core([gla, hgrn])
    for o, og, gout_ref, o_ref in ((o_b, pb_ref[:, og_lo:og_lo + GROUP], gout_b_ref, ob_ref),
                                   (o_c, pc_ref[:, 3 * GROUP:4 * GROUP], gout_c_ref, oc_ref)):
        ms = _dot_x_exact(o * o, g64_ref[...], 2)
        o_ref[...] = (o * lax.rsqrt(ms + RMS_EPS) * gout_ref[...] * _silu(og)).astype(BF16)


def _gla_hgrn_operands(tb, gate_up, gate_b, g_out_b, lb_logits, g_out_c):
    hk_b, hk_c = HEADS * GLA_K, HEADS * HG_K
    gup = jnp.zeros((LANE, hk_b), F32).at[:GLA_RANK].set(gate_up).astype(BF16)
    tile = lambda g: jnp.tile(g, HEADS).reshape(1, GROUP)
    extra = [gup, gate_b.reshape(1, hk_b), tile(g_out_b), lb_logits, tile(g_out_c),
             _group_mean_matrix(GROUP, HEAD_V), *_gla_consts(tb, hk_b), *_gla_consts(tb, hk_c)]
    return extra, _gla_scratch(tb, hk_b) + _gla_scratch(tb, hk_c)


def _rwkv_consts(tb):
    c, nch = RW_CHUNK, tb // RW_CHUNK
    t = jnp.arange(tb)
    same = t[:, None] // c == t[None, :] // c
    lseg = (same & (t[None, :] <= t[:, None])).astype(BF16)
    strict = (same & (t[None, :] < t[:, None])).astype(F32)
    incl = (same & (t[None, :] <= t[:, None])).astype(F32)
    eye = jnp.eye(tb, dtype=F32)
    hmask = (jnp.arange(HEADS)[:, None] == jnp.arange(GROUP)[None, :] // HEAD_V).astype(F32)
    bd = (jnp.arange(GROUP)[:, None] // HEAD_V == jnp.arange(GROUP)[None, :] // HEAD_V).astype(F32)
    cmask = (t[:, None] // c == jnp.arange(nch * GROUP)[None, :] // GROUP).astype(F32)
    eye_g = jnp.eye(GROUP, dtype=F32)
    return lseg, strict, incl, eye, hmask, bd, cmask, eye_g


def _rwkv_core(r, logw, k2, v, kk, a, consts, mt_ref, gam_ref, stack_ref):
    lseg_ref, strict_ref, incl_ref, eye_ref, hmask_ref, bd_ref, cmask_ref, eyeg_ref = consts
    tb = r.shape[0]
    c = RW_CHUNK
    nch = tb // c
    gam = _dot_exact_x(lseg_ref[...], logw, 3)
    yield
    gam_ref[0] = gam
    glast = _chunk_row_bcast(gam_ref, 0, c - 1, c)
    eng = jnp.exp(-gam)
    ecl = jnp.exp(glast - gam)
    beta = a * kk
    abar = kk * jnp.exp(gam - logw)
    rbar = r * jnp.exp(gam)
    yb = jnp.concatenate([beta * eng, k2 * eng], axis=0).astype(BF16)
    bhat = beta * ecl
    khat = k2 * ecl
    strict, incl = strict_ref[...] > 0.0, incl_ref[...] > 0.0
    eye = eye_ref[...]
    heads = range(HEADS)
    hms = [hmask_ref[h:h + 1, :] for h in heads]
    abar_h = [abar * hms[h] for h in heads]
    vh = [(v * hms[h]).astype(BF16) for h in heads]
    pw, a_rb, a_rk, z = [], [], [], []
    for h in heads:
        xh = jnp.concatenate([abar_h[h], rbar * hms[h]], axis=0).astype(BF16)
        s = _dot_nt(xh, yb)
        pw.append(jnp.where(strict, -s[:tb, :tb], 0.0))
        a_ak = jnp.where(strict, s[:tb, tb:], 0.0).astype(BF16)
        a_rb.append(jnp.where(incl, s[tb:, :tb], 0.0).astype(BF16))
        a_rk.append(jnp.where(incl, s[tb:, tb:], 0.0).astype(BF16))
        z.append(jnp.concatenate([abar_h[h], _dot(a_ak, vh[h])], axis=1).astype(BF16))
        yield
    tinv = [eye + pw[h] for h in heads]
    for _ in range(int(math.log2(c)) - 1):
        pwb = [pw[h].astype(BF16) for h in heads]
        pw = [_dot(pwb[h], pwb[h]) for h in heads]
        yield
        tinv = [tinv[h] + _dot(tinv[h].astype(BF16), pw[h].astype(BF16)) for h in heads]
        yield
    wu_h = [_dot(tinv[h].astype(BF16), z[h]) for h in heads]
    yield
    om_h = [_dot(a_rb[h], wu_h[h].astype(BF16)) for h in heads]
    arkv_h = [_dot(a_rk[h], vh[h]) for h in heads]
    yield
    wu, om, arkv = sum(wu_h[1:], wu_h[0]), sum(om_h[1:], om_h[0]), sum(arkv_h[1:], arkv_h[0])
    wbar, u0 = wu[:, :GROUP], wu[:, GROUP:]
    omega = rbar - om[:, :GROUP]
    y0 = arkv - om[:, GROUP:]
    wb, bhb = wbar.astype(BF16), bhat.astype(BF16)
    vb, nub, khb = v.astype(BF16), (-u0).astype(BF16), khat.astype(BF16)
    x_c, psi = [], []
    for ci in range(nch):
        rows = slice(ci * c, (ci + 1) * c)
        x_c.append((_dot_tn(wb[rows], bhb[rows]) * bd_ref[...]).astype(BF16))
        psi.append(_dot_tn(jnp.concatenate([vb[rows], nub[rows]], axis=0),
                           jnp.concatenate([khb[rows], bhb[rows]], axis=0)) * bd_ref[...])
    yield
    mt = mt_ref[...]
    for ci in range(nch):
        stack_ref[:, ci * GROUP:(ci + 1) * GROUP] = mt.astype(BF16)
        gl = jnp.exp(gam_ref[0, ci * c + c - 1:ci * c + c, :])
        mt = mt * gl - _dot(mt.astype(BF16), x_c[ci]) + psi[ci]
        yield
    mt_ref[...] = mt
    om_exp = (jnp.concatenate([omega] * nch, axis=1) * cmask_ref[...]).astype(BF16)
    return y0 + _dot_nt(om_exp, stack_ref[...])


def _rwkv_body(pd_ref, mu_ref, w0_ref, wup_ref, a0_ref, aup_ref, gup_ref, kk_ref, ka_ref, rk_ref,
               lng_ref, lnb_ref, g64_ref, s64_ref, *rest):
    consts, (o_ref, mt_ref, carry_ref, gam_ref, stack_ref) = rest[:8], rest[8:]
    tb = pd_ref.shape[0]
    p = pd_ref[...]
    rowid = lax.broadcasted_iota(jnp.int32, p.shape, 0)
    prev = jnp.where(rowid == 0, carry_ref[...], pltpu.roll(p, 1, axis=0))
    carry_ref[...] = p[tb - 1:tb, :]
    p = p + mu_ref[...] * (prev - p)
    r, k, v = p[:, 0:GROUP], p[:, GROUP:2 * GROUP], p[:, 2 * GROUP:3 * GROUP]
    low = p[:, 3 * GROUP:3 * GROUP + LANE]
    logw = -RW_DECAY_SCALE * _sigmoid(w0_ref[...] + _dot(jnp.tanh(low).astype(BF16), wup_ref[...]))
    a = _sigmoid(a0_ref[...] + _dot(low.astype(BF16), aup_ref[...]))
    g = _dot(_sigmoid(low).astype(BF16), gup_ref[...])
    kk = k * kk_ref[...]
    kk = kk * lax.rsqrt(jnp.maximum(_dot_x_exact(kk * kk, s64_ref[...], 2), 1e-24))
    k2 = k * (1.0 + (a - 1.0) * ka_ref[...])
    yield
    y = yield from _rwkv_core(r, logw, k2, v, kk, a, consts, mt_ref, gam_ref, stack_ref)
    yield
    mean = _dot_x_exact(y, g64_ref[...], 2)
    d = y - mean
    yield
    var = _dot_x_exact(d * d, g64_ref[...], 2)
    yn = d * lax.rsqrt(var + RW_LN_EPS) * lng_ref[...] + lnb_ref[...]
    bonus = _dot_x_exact(r * k2 * rk_ref[...], s64_ref[...], 2) * v
    o_ref[...] = ((yn + bonus) * g).astype(BF16)


def _rwkv_operands(tb, mu, w0, w_up, a0, a_up, g_up, k_k, k_a, r_k, ln_g, ln_b):
    nch = tb // RW_CHUNK
    row = lambda z: z.reshape(1, -1)
    low = lambda w, start: jnp.zeros((LANE, GROUP), F32).at[start:start + w.shape[0]].set(w).astype(BF16)
    extra = [row(mu), row(w0), low(w_up, 0), row(a0), low(a_up, 32), low(g_up, 64), row(k_k), row(k_a),
             row(r_k), row(ln_g), row(ln_b), _group_mean_matrix(GROUP, HEAD_V),
             _group_sum_matrix(GROUP, HEAD_V), *_rwkv_consts(tb)]
    scratch = [pltpu.VMEM((GROUP, GROUP), F32), pltpu.VMEM((1, D_W), F32),
               pltpu.VMEM((1, tb, GROUP), F32), pltpu.VMEM((GROUP, nch * GROUP), BF16)]
    return extra, scratch


def _recurrent_body(*refs, n_rw, n_gh, n_rw_scratch, layer):
    pd_ref, pb_ref, pc_ref = refs[:3]
    rw_extra = refs[3:3 + n_rw]
    gh_extra = refs[3 + n_rw:3 + n_rw + n_gh]
    od_ref, ob_ref, oc_ref = refs[3 + n_rw + n_gh:6 + n_rw + n_gh]
    scratch = refs[6 + n_rw + n_gh:]
    gh_scratch = scratch[n_rw_scratch:]

    @pl.when(pl.program_id(1) == 0)
    def _():
        for ref in (scratch[0], scratch[1], gh_scratch[0], gh_scratch[N_GLA_SCRATCH]):
            ref[...] = jnp.zeros_like(ref)

    live = [_rwkv_body(pd_ref, *rw_extra, od_ref, *scratch[:n_rw_scratch]),
            _gla_hgrn_body(pb_ref, pc_ref, *gh_extra, ob_ref, oc_ref, *gh_scratch, layer=layer)]
    while live:
        for gen in list(live):
            if next(gen, StopIteration) is StopIteration:
                live.remove(gen)


def _recurrent_mixers(pb, pc, pd, batch, seq, gla_hgrn_params, rwkv_params, layer):
    tb = min(TIME_BLOCK, seq)
    nt = seq // tb
    rw_extra, rw_scratch = _rwkv_operands(tb, *rwkv_params)
    gh_extra, gh_scratch = _gla_hgrn_operands(tb, *gla_hgrn_params)
    row = lambda width: pl.BlockSpec((tb, width), lambda b, t: (b * nt + t, 0))
    body = functools.partial(_recurrent_body, n_rw=len(rw_extra), n_gh=len(gh_extra),
                             n_rw_scratch=len(rw_scratch), layer=layer)
    o_d, o_b, o_c = pl.pallas_call(
        body,
        grid=(batch, nt),
        in_specs=[row(D_W), row(B_W), row(C_W)] + [_const_spec(e.shape) for e in rw_extra + gh_extra],
        out_specs=[row(GROUP)] * 3,
        out_shape=[jax.ShapeDtypeStruct((batch * seq, GROUP), BF16)] * 3,
        scratch_shapes=rw_scratch + gh_scratch,
        compiler_params=_params("parallel", "arbitrary"),
        name="recurrent_mixers",
    )(pd, pb, pc, *rw_extra, *gh_extra)
    return o_b, o_c, o_d


def _relayout_w_in(w_in):
    a_end, b_end, c_end = 768, 768 + 784, 768 + 784 + 1024
    wb = w_in[:, a_end:b_end]
    pad = jnp.zeros((w_in.shape[0], LANE - GLA_RANK), w_in.dtype)
    wb = jnp.concatenate([wb[:, :528], pad, wb[:, 528:]], axis=1)
    return jnp.concatenate([w_in[:, :a_end], wb, w_in[:, b_end:c_end], w_in[:, c_end:]], axis=1).astype(BF16)


def kernel(x, norm_mix_g, w_in, da_q_norm_g, da_k_norm_g, da_lambda_q1, da_lambda_k1, da_lambda_q2, da_lambda_k2, da_out_norm_g, gla_gate_up, gla_gate_b, gla_out_norm_g, hgrn_lb_logits, hgrn_out_norm_g, rw_shift_mu, rw_w0, rw_w_up, rw_a0, rw_a_up, rw_g_up, rw_k_k, rw_k_a, rw_r_k, rw_ln_g, rw_ln_b, w_out, norm_mlp_g, w_mlp_up, w_mlp_down):
    batch, seq, _ = x.shape
    xf = x.reshape(batch * seq, D_MODEL)
    blk = min(ATT_BLOCK, seq)
    kpos = _alibi_key_terms(blk)
    for l in range(DEPTH):
        lam_init = 0.8 - 0.6 * math.exp(-0.3 * l)
        qt, k, vt, pb, pc, pd = _in_proj(xf, norm_mix_g[l], _relayout_w_in(w_in[l]), batch, seq, blk,
                                          da_q_norm_g[l], da_k_norm_g[l], kpos)
        o_a = _diff_attention(qt, k, vt, batch, seq, blk, da_lambda_q1[l], da_lambda_k1[l], da_lambda_q2[l],
                              da_lambda_k2[l], da_out_norm_g[l], lam_init)
        o_b, o_c, o_d = _recurrent_mixers(
            pb, pc, pd, batch, seq,
            (gla_gate_up[l], gla_gate_b[l], gla_out_norm_g[l], hgrn_lb_logits, hgrn_out_norm_g[l]),
            (rw_shift_mu[l], rw_w0[l], rw_w_up[l], rw_a0[l], rw_a_up[l], rw_g_up[l], rw_k_k[l], rw_k_a[l],
             rw_r_k[l].reshape(-1), rw_ln_g[l], rw_ln_b[l]), l)
        xf = _post(xf, (o_a, o_b, o_c, o_d), w_out[l].astype(BF16), norm_mlp_g[l],
                   w_mlp_up[l].astype(BF16), w_mlp_down[l].astype(BF16))
    return xf.reshape(batch, seq, D_MODEL)
```

```python
import functools
import math

import jax
import jax.numpy as jnp
import ml_dtypes
import numpy as np
from jax import lax
from jax.experimental import pallas as pl
from jax.experimental.pallas import tpu as pltpu

F32, BF16 = jnp.float32, jnp.bfloat16

D_MODEL = 1024
GROUP = 256
HEADS = 4
HEAD_V = 64
DA_QK = 32
GLA_K = 32
GLA_RANK = 16
GLA_NORMALIZER = 16.0
HG_K = 64
RW_DECAY_SCALE = 0.606531
RW_LN_EPS = 64e-5
D_FF = 4 * D_MODEL
RMS_EPS = 1e-6
MASK_VALUE = -1e30
TINY = 1e-30
DEPTH = 4

LANE = 128
A_W, B_W, C_W, D_W = 768, 896, 1024, 896
P_W = A_W + B_W + C_W + D_W

ROW_TILE = 512
TIME_BLOCK = 256
ATT_BLOCK = 256
ATT_VA = 80
ATT_KA = 80
LOG2E = 1.4426950408889634
GLA_CHUNK = 32
GLA_SUB = 16
RW_CHUNK = 64
VMEM_LIMIT = 56 * 1024 * 1024


def _dot(a, b):
    return jnp.dot(a, b, preferred_element_type=F32)


def _dot_nt(a, b):
    return lax.dot_general(a, b, (((1,), (1,)), ((), ())), preferred_element_type=F32)


def _dot_tn(a, b):
    return lax.dot_general(a, b, (((0,), (0,)), ((), ())), preferred_element_type=F32)


def _split(x, parts):
    out, r = [], x
    for i in range(parts):
        h = r.astype(BF16)
        out.append(h)
        if i + 1 < parts:
            r = r - h.astype(F32)
    return out


def _dot_x_exact(x, m, parts):
    acc = None
    for h in _split(x, parts):
        d = _dot(h, m)
        acc = d if acc is None else acc + d
    return acc


def _dot_exact_x(m, x, parts):
    acc = None
    for h in _split(x, parts):
        d = _dot(m, h)
        acc = d if acc is None else acc + d
    return acc


def _sigmoid(x):
    return 1.0 / (1.0 + jnp.exp(-x))


def _silu(x):
    return x * _sigmoid(x)


def _const_spec(shape):
    zeros = (0,) * len(shape)
    return pl.BlockSpec(shape, lambda *_: zeros)


def _chunk_row_bcast(ref, idx, row, chunk):
    n_rows, width = ref.shape[1], ref.shape[2]
    pieces = []
    for c in range(n_rows // chunk):
        r = c * chunk + row
        pieces.append(jnp.broadcast_to(ref[idx, r:r + 1, :], (chunk, width)))
    return jnp.concatenate(pieces, axis=0)


def _params(*sem):
    return pltpu.CompilerParams(dimension_semantics=sem, vmem_limit_bytes=VMEM_LIMIT)


def _in_proj_body(x_ref, g_ref, w_ref, gq_ref, gk_ref, g32_ref, kpos_ref,
                  qt_ref, k_ref, vt_ref, pb_ref, pc_ref, pd_ref):
    x = x_ref[...]
    h = (x * lax.rsqrt(jnp.mean(x * x, axis=-1, keepdims=True) + RMS_EPS) * g_ref[...]).astype(BF16)
    pa = _dot(h, w_ref[:, 0:A_W])
    q, k = pa[:, 0:GROUP], pa[:, GROUP:2 * GROUP]
    qm = _dot_x_exact(q * q, g32_ref[...], 2)
    km = _dot_x_exact(k * k, g32_ref[...], 2)
    off = A_W
    for o_ref in (pb_ref, pc_ref, pd_ref):
        n = o_ref.shape[1]
        o_ref[...] = _dot(h, w_ref[:, off:off + n])
        off += n
    _attention_layouts(pa, qm, km, gq_ref, gk_ref, kpos_ref, qt_ref, k_ref, vt_ref)


def _in_proj(x, g, w, batch, seq, blk, gq, gk, kpos):
    tm = min(ROW_TILE, seq)
    nt = seq // tm
    n = batch * seq
    row = lambda width: pl.BlockSpec((tm, width), lambda b, t: (b * nt + t, 0))
    tile = lambda gain: jnp.tile(gain, GROUP // DA_QK).reshape(1, GROUP)
    return pl.pallas_call(
        _in_proj_body,
        grid=(batch, nt),
        in_specs=[row(D_MODEL), _const_spec((1, D_MODEL)), _const_spec((D_MODEL, P_W)),
                  _const_spec((1, GROUP)), _const_spec((1, GROUP)), _const_spec((GROUP, GROUP)),
                  _const_spec(kpos.shape)],
        out_specs=[pl.BlockSpec((1, 2, HEADS * ATT_KA, tm), lambda b, t: (b, 0, 0, t)),
                   pl.BlockSpec((1, HEADS, tm, ATT_KA), lambda b, t: (b, 0, t, 0)),
                   pl.BlockSpec((1, HEADS, tm // blk, ATT_VA, blk), lambda b, t: (b, 0, t, 0, 0)),
                   row(B_W), row(C_W), row(D_W)],
        out_shape=[jax.ShapeDtypeStruct((batch, 2, HEADS * ATT_KA, seq), BF16),
                   jax.ShapeDtypeStruct((batch, HEADS, seq, ATT_KA), BF16),
                   jax.ShapeDtypeStruct((batch, HEADS, seq // blk, ATT_VA, blk), BF16)]
                  + [jax.ShapeDtypeStruct((n, wd), F32) for wd in (B_W, C_W, D_W)],
        compiler_params=_params("parallel", "parallel"),
        name="in_proj",
    )(x, g.reshape(1, D_MODEL), w, tile(gq), tile(gk), _group_mean_matrix(GROUP, DA_QK), kpos)


def _post_body(x_ref, oa_ref, ob_ref, oc_ref, od_ref, wo_ref, g_ref, wu_ref, wd_ref, out_ref):
    mixed = jnp.concatenate([r[...] for r in (oa_ref, ob_ref, oc_ref, od_ref)], axis=1)
    x1 = x_ref[...] + _dot(mixed, wo_ref[...])
    h = (x1 * lax.rsqrt(jnp.mean(x1 * x1, axis=-1, keepdims=True) + RMS_EPS) * g_ref[...]).astype(BF16)
    y = x1
    for f in range(D_FF // D_MODEL):
        cols = slice(f * D_MODEL, (f + 1) * D_MODEL)
        u = jnp.square(jnp.maximum(_dot(h, wu_ref[:, cols]), 0.0)).astype(BF16)
        y = y + _dot(u, wd_ref[cols, :])
    out_ref[...] = y


def _post(x, outs, w_out, g, w_up, w_down):
    n = x.shape[0]
    tm = min(ROW_TILE, n)
    row = lambda width: pl.BlockSpec((tm, width), lambda i: (i, 0))
    return pl.pallas_call(
        _post_body,
        grid=(n // tm,),
        in_specs=[row(D_MODEL)] + [row(GROUP)] * 4 + [
            _const_spec((D_MODEL, D_MODEL)), _const_spec((1, D_MODEL)),
            _const_spec((D_MODEL, D_FF)), _const_spec((D_FF, D_MODEL))],
        out_specs=row(D_MODEL),
        out_shape=jax.ShapeDtypeStruct((n, D_MODEL), F32),
        compiler_params=_params("parallel"),
        name="post",
    )(x, *outs, w_out, g.reshape(1, D_MODEL), w_up, w_down)


def _group_mean_matrix(width, group):
    i = jnp.arange(width) // group
    return ((i[:, None] == i[None, :]).astype(F32) / group).astype(BF16)


def _group_sum_matrix(width, group):
    i = jnp.arange(width) // group
    return (i[:, None] == i[None, :]).astype(BF16)


def _attention_layouts(pa, qm, km, gq_ref, gk_ref, kpos_ref, qt_ref, k_ref, vt_ref):
    tm = pa.shape[0]
    blk = vt_ref.shape[-1]
    q, k = pa[:, 0:GROUP], pa[:, GROUP:2 * GROUP]
    q_t = (q * lax.rsqrt(qm + RMS_EPS) * gq_ref[...] * (DA_QK ** -0.5 * LOG2E)).T
    comp = (lax.broadcasted_iota(jnp.int32, q_t.shape, 0) // DA_QK) % 2
    n_extra = ATT_KA - 2 * DA_QK
    q_extra = (lax.broadcasted_iota(jnp.int32, (n_extra, tm), 0) < 3).astype(BF16)
    for c in range(2):
        q_c = jnp.where(comp == c, q_t, 0.0).astype(BF16)
        for h in range(HEADS):
            qt_ref[0, c, h * ATT_KA:h * ATT_KA + 2 * DA_QK, :] = q_c[h * HEAD_V:(h + 1) * HEAD_V]
            qt_ref[0, c, h * ATT_KA + 2 * DA_QK:(h + 1) * ATT_KA, :] = q_extra
    kn = (k * lax.rsqrt(km + RMS_EPS) * gk_ref[...]).astype(BF16)
    v_t = pa[:, 2 * GROUP:3 * GROUP].T.astype(BF16)
    ones_row = (lax.broadcasted_iota(jnp.int32, (ATT_VA - HEAD_V, blk), 0) == 0).astype(BF16)
    for h in range(HEADS):
        k_ref[0, h, :, 0:2 * DA_QK] = kn[:, h * HEAD_V:(h + 1) * HEAD_V]
        for sb in range(tm // blk):
            k_ref[0, h, sb * blk:(sb + 1) * blk, 2 * DA_QK:ATT_KA] = kpos_ref[h]
        for sb in range(tm // blk):
            vt_ref[0, h, sb, 0:HEAD_V, :] = v_t[h * HEAD_V:(h + 1) * HEAD_V, sb * blk:(sb + 1) * blk]
            vt_ref[0, h, sb, HEAD_V:ATT_VA, :] = ones_row


def _attn_body(qt_ref, k_ref, vt_ref, td_ref, lp_ref, go_ref, o_ref, m_ref, acc_ref, s_ref, *,
               lam_init, slopes, blk):
    i = pl.program_id(1)
    chains = [(h, c) for h in range(HEADS) for c in range(2)]

    def stage(cur, cur_slot, first, nxt=None, nxt_slot=None):
        for n, (h, c) in enumerate(chains):
            if nxt is not None:
                q_hc = qt_ref[0, c, h * ATT_KA:(h + 1) * ATT_KA, :]
                s_ref[nxt_slot, n] = _dot(k_ref[0, h, nxt], q_hc)
            if cur is None:
                continue
            off = -cur.astype(F32) * (slopes[h] * LOG2E * blk)
            s = s_ref[cur_slot, n]
            if first:
                s = s + td_ref[...]
            smax = jnp.max(s, axis=0, keepdims=True) - off
            if first:
                mx = smax
            else:
                m = m_ref[n]
                mx = jnp.maximum(m, smax)
            m_ref[n] = mx
            p = jnp.exp2(s - (mx + off)).astype(BF16)
            pv = _dot(vt_ref[0, h, cur], p)
            acc_ref[n] = pv if first else jnp.exp2(m - mx) * acc_ref[n] + pv

    stage(None, None, False, nxt=i, nxt_slot=0)
    stage(i, 0, True, nxt=0, nxt_slot=1)

    def body(u, carry):
        stage(2 * u, 1, False, nxt=2 * u + 1, nxt_slot=0)
        stage(2 * u + 1, 0, False, nxt=2 * u + 2, nxt_slot=1)
        return carry

    lax.fori_loop(0, (i - 1) // 2, body, 0)
    odd = lax.rem(i, 2) == 1

    @pl.when(jnp.logical_and(jnp.logical_not(odd), i > 0))
    def _():
        stage(i - 2, 1, False, nxt=i - 1, nxt_slot=0)
        stage(i - 1, 0, False)

    @pl.when(odd)
    def _():
        stage(i - 1, 1, False)

    lp = lp_ref[...]
    lam = (jnp.exp(jnp.sum(lp[0:1] * lp[1:2], axis=-1, keepdims=True))
           - jnp.exp(jnp.sum(lp[2:3] * lp[3:4], axis=-1, keepdims=True)) + lam_init)
    outs = []
    for h in range(HEADS):
        a1, a2 = acc_ref[2 * h], acc_ref[2 * h + 1]
        o = a1[0:HEAD_V] / a1[HEAD_V:HEAD_V + 1] - lam * (a2[0:HEAD_V] / a2[HEAD_V:HEAD_V + 1])
        outs.append(o * lax.rsqrt(jnp.mean(o * o, axis=0, keepdims=True) + RMS_EPS) * go_ref[...]
                    * (1.0 - lam_init))
    o_ref[...] = jnp.concatenate(outs, axis=0).T.astype(BF16)


def _alibi_slopes():
    return tuple(2.0 ** (-8.0 * (h + 1) / HEADS) for h in range(HEADS))


def _alibi_key_terms(blk):
    kpos = np.zeros((HEADS, blk, ATT_KA - 2 * DA_QK), ml_dtypes.bfloat16)
    for h, s in enumerate(_alibi_slopes()):
        r = np.float32(s * LOG2E) * np.arange(blk, dtype=np.float32)
        for part in range(3):
            kpos[h, :, part] = r.astype(ml_dtypes.bfloat16)
            r = r - kpos[h, :, part].astype(np.float32)
    return jnp.asarray(kpos)


def _diff_attention(qt, k, vt, batch, seq, blk, lq1, lk1, lq2, lk2, g_out, lam_init):
    nb = seq // blk
    slopes = _alibi_slopes()
    kh = k.reshape(batch, HEADS, nb, blk, ATT_KA)
    rel = jnp.arange(blk)[None, :] - jnp.arange(blk)[:, None]
    td = jnp.where(rel >= 0, 0.0, MASK_VALUE).astype(F32)
    lp = jnp.stack([lq1, lk1, lq2, lk2])
    go = jnp.broadcast_to(g_out[:, None], (HEAD_V, blk))
    body = functools.partial(_attn_body, lam_init=lam_init, slopes=slopes, blk=blk)
    return pl.pallas_call(
        body,
        grid=(batch, nb),
        in_specs=[
            pl.BlockSpec((1, 2, HEADS * ATT_KA, blk), lambda b, i: (b, 0, 0, i)),
            pl.BlockSpec((1, HEADS, nb, blk, ATT_KA), lambda b, i: (b, 0, 0, 0, 0)),
            pl.BlockSpec((1, HEADS, nb, ATT_VA, blk), lambda b, i: (b, 0, 0, 0, 0)),
            _const_spec((blk, blk)),
            _const_spec((4, DA_QK)), _const_spec((HEAD_V, blk))],
        out_specs=pl.BlockSpec((blk, GROUP), lambda b, i: (b * nb + i, 0)),
        out_shape=jax.ShapeDtypeStruct((batch * seq, GROUP), BF16),
        scratch_shapes=[pltpu.VMEM((2 * HEADS, 1, blk), F32), pltpu.VMEM((2 * HEADS, ATT_VA, blk), F32),
                        pltpu.VMEM((2, 2 * HEADS, blk, blk), F32)],
        compiler_params=_params("parallel", "arbitrary"),
        name="diff_attention",
    )(qt, kh, vt, td, lp, go)


def _gla_consts(tb, hk):
    c, sub, nc, kh = GLA_CHUNK, GLA_SUB, tb // GLA_CHUNK, hk // HEADS
    nsub = c // sub
    width = nsub * HEADS * sub
    t = jnp.arange(tb)
    lseg = ((t[:, None] // c == t[None, :] // c) & (t[None, :] <= t[:, None])).astype(BF16)
    r = jnp.arange(sub * hk)
    col = jnp.arange(width)
    ind = ((r[:, None] // hk == col[None, :] % sub)
           & ((r[:, None] % hk) // kh == (col[None, :] // sub) % HEADS)).astype(BF16)
    submask = ((t[:, None] % c) // sub == col[None, :] // (HEADS * sub)).astype(F32)
    vr = jnp.arange(nc * width)
    row_h = (vr[:, None] // sub) % HEADS
    vmask = (row_h == jnp.arange(GROUP)[None, :] // HEAD_V).astype(BF16)
    kmask = (((vr[:, None] % width) // (HEADS * sub) == 0)
             & (row_h == jnp.arange(hk)[None, :] // kh)).astype(BF16)
    bdmask = (jnp.arange(GROUP)[:, None] // HEAD_V == jnp.arange(hk)[None, :] // kh).astype(F32)
    return lseg, ind, submask, vmask, kmask, bdmask


class _GlaStream:
    def __init__(self, q, k, lf, v, consts, scratch):
        self.q, self.k, self.lf, self.v = q, k, lf, v
        self.lseg, self.ind, self.submask, self.vmask, self.kmask, self.bdmask = consts
        self.st, self.tcat, self.bk, self.stack = scratch


def _gla_core(streams):
    c, sub = GLA_CHUNK, GLA_SUB
    assert c == 2 * sub
    tb = streams[0].q.shape[0]
    nc = tb // c
    width = 2 * HEADS * sub
    chunk = lambda x, ci, n=c: x[ci * n:(ci + 1) * n]
    for s in streams:
        s.b = _dot_exact_x(s.lseg[...], s.lf, 3) * LOG2E
    yield
    for s in streams:
        hk = s.q.shape[1]
        s.bk[0] = s.b
        s.bk[1] = s.k
        s.row = lax.broadcasted_iota(jnp.int32, (tb, hk), 0)
        rowmod = s.row % sub
        for jj in range(sub):
            bj = _chunk_row_bcast(s.bk, 0, jj, sub)
            kj = _chunk_row_bcast(s.bk, 1, jj, sub)
            t = s.q * kj * jnp.exp2(s.b - bj)
            s.tcat[:, jj * hk:(jj + 1) * hk] = jnp.where(rowmod >= jj, t, 0.0).astype(BF16)
            if jj % 4 == 3:
                yield
    yield
    for s in streams:
        s.a = _dot(s.tcat[...], s.ind[...]) * s.submask[...]
    yield
    for s in streams:
        bref = _chunk_row_bcast(s.bk, 0, sub - 1, c)
        second = s.row % c >= sub
        s.qm = jnp.where(second, s.q * jnp.exp2(s.b - bref), 0.0).astype(BF16)
        kt = jnp.where(second, 0.0, s.k * jnp.exp2(bref - s.b)).astype(BF16)
        s.k_exp = jnp.concatenate([kt[ci * c:ci * c + sub] for ci in range(nc) for _ in range(2 * HEADS)],
                                  axis=0) * s.kmask[...]
    yield
    for s in streams:
        s.a_off = [_dot_nt(chunk(s.qm, ci), chunk(s.k_exp, ci, width)) for ci in range(nc)]
    yield
    for s in streams:
        s.a_c = [(chunk(s.a, ci) + s.a_off[ci]).astype(BF16) for ci in range(nc)]
        s.vb = s.v.astype(BF16)
        s.v_bd = jnp.concatenate([s.vb[ci * c + h * sub:ci * c + (h + 1) * sub]
                                  for ci in range(nc) for h in range(2) for _ in range(HEADS)],
                                 axis=0) * s.vmask[...]
    yield
    for s in streams:
        s.o = [_dot(s.a_c[ci], chunk(s.v_bd, ci, width)) for ci in range(nc)]
    yield
    for s in streams:
        blast = _chunk_row_bcast(s.bk, 0, c - 1, c)
        s.qe = (s.q * jnp.exp2(s.b)).astype(BF16)
        s.ke = (s.k * jnp.exp2(blast - s.b)).astype(BF16)
    yield
    for s in streams:
        s.pt = [_dot_tn(s.vb[ci * c:(ci + 1) * c], s.ke[ci * c:(ci + 1) * c]) * s.bdmask[...]
                for ci in range(nc)]
    yield
    for s in streams:
        hk = s.q.shape[1]
        st = s.st[...]
        for ci in range(nc):
            s.stack[:, ci * hk:(ci + 1) * hk] = st.astype(BF16)
            st = st * jnp.exp2(s.bk[0, ci * c + c - 1:ci * c + c, :]) + s.pt[ci]
        s.st[...] = st
    yield
    return [jnp.concatenate([s.o[ci] + _dot_nt(chunk(s.qe, ci), s.stack[:, ci * s.q.shape[1]:(ci + 1) * s.q.shape[1]])
                             for ci in range(nc)], axis=0) for s in streams]


def _gla_scratch(tb, hk):
    nc = tb // GLA_CHUNK
    return [pltpu.VMEM((GROUP, hk), F32), pltpu.VMEM((tb, GLA_SUB * hk), BF16),
            pltpu.VMEM((2, tb, hk), F32), pltpu.VMEM((GROUP, nc * hk), BF16)]


N_GLA_CONSTS = 6
N_GLA_SCRATCH = 4


def _gla_hgrn_body(pb_ref, pc_ref, gup_ref, gb_ref, gout_b_ref, lg_ref, gout_c_ref, g64_ref, *rest, layer):
    consts_b, consts_c = rest[:N_GLA_CONSTS], rest[N_GLA_CONSTS:2 * N_GLA_CONSTS]
    ob_ref, oc_ref = rest[2 * N_GLA_CONSTS:2 * N_GLA_CONSTS + 2]
    scratch = rest[2 * N_GLA_CONSTS + 2:]
    scratch_b, scratch_c = scratch[:N_GLA_SCRATCH], scratch[N_GLA_SCRATCH:]
    hk = HEADS * GLA_K
    gate_lo, og_lo = 2 * hk + GROUP, 2 * hk + GROUP + LANE
    logit = _dot(pb_ref[:, gate_lo:og_lo].astype(BF16), gup_ref[...]) + gb_ref[...]
    lf_b = (jnp.minimum(logit, 0.0) - jnp.log(1.0 + jnp.exp(-jnp.abs(logit)))) * (1.0 / GLA_NORMALIZER)
    gla = _GlaStream(pb_ref[:, 0:hk] * (GLA_K ** -0.5), pb_ref[:, hk:2 * hk], lf_b,
                     pb_ref[:, 2 * hk:2 * hk + GROUP], consts_b, scratch_b)
    lg = lg_ref[...]
    e = jnp.exp(lg - jnp.max(lg, axis=0, keepdims=True))
    p = e / jnp.sum(e, axis=0, keepdims=True)
    cs = p[0:1]
    for i in range(1, layer + 1):
        cs = cs + p[i:i + 1]
    lb = cs - p[0:1]
    z = pc_ref[:, GROUP:2 * GROUP]
    forget = lb + (1.0 - lb) * _sigmoid(z)
    hgrn = _GlaStream(_silu(pc_ref[:, 0:GROUP]), (1.0 - lb) * _sigmoid(-z),
                      jnp.log(jnp.maximum(forget, TINY)), pc_ref[:, 2 * GROUP:3 * GROUP], consts_c, scratch_c)
    o_b, o_c = yield from _gla_core([gla, hgrn])
    for o, og, gout_ref, o_ref in ((o_b, pb_ref[:, og_lo:og_lo + GROUP], gout_b_ref, ob_ref),
                                   (o_c, pc_ref[:, 3 * GROUP:4 * GROUP], gout_c_ref, oc_ref)):
        ms = _dot_x_exact(o * o, g64_ref[...], 2)
        o_ref[...] = (o * lax.rsqrt(ms + RMS_EPS) * gout_ref[...] * _silu(og)).astype(BF16)


def _gla_hgrn_operands(tb, gate_up, gate_b, g_out_b, lb_logits, g_out_c):
    hk_b, hk_c = HEADS * GLA_K, HEADS * HG_K
    gup = jnp.zeros((LANE, hk_b), F32).at[:GLA_RANK].set(gate_up).astype(BF16)
    tile = lambda g: jnp.tile(g, HEADS).reshape(1, GROUP)
    extra = [gup, gate_b.reshape(1, hk_b), tile(g_out_b), lb_logits, tile(g_out_c),
             _group_mean_matrix(GROUP, HEAD_V), *_gla_consts(tb, hk_b), *_gla_consts(tb, hk_c)]
    return extra, _gla_scratch(tb, hk_b) + _gla_scratch(tb, hk_c)


def _rwkv_consts(tb):
    c, nch = RW_CHUNK, tb // RW_CHUNK
    t = jnp.arange(tb)
    same = t[:, None] // c == t[None, :] // c
    lseg = (same & (t[None, :] <= t[:, None])).astype(BF16)
    strict = (same & (t[None, :] < t[:, None])).astype(F32)
    incl = (same & (t[None, :] <= t[:, None])).astype(F32)
    eye = jnp.eye(tb, dtype=F32)
    hmask = (jnp.arange(HEADS)[:, None] == jnp.arange(GROUP)[None, :] // HEAD_V).astype(F32)
    bd = (jnp.arange(GROUP)[:, None] // HEAD_V == jnp.arange(GROUP)[None, :] // HEAD_V).astype(F32)
    return lseg, strict, incl, eye, hmask, bd


def _rwkv_core(r, logw, k2, v, kk, a, consts, mt_ref, gam_ref, stack_ref):
    lseg_ref, strict_ref, incl_ref, eye_ref, hmask_ref, bd_ref = consts
    tb = r.shape[0]
    c = RW_CHUNK
    nch = tb // c
    gam = _dot_exact_x(lseg_ref[...], logw, 3)
    yield
    gam_ref[0] = gam
    glast = _chunk_row_bcast(gam_ref, 0, c - 1, c)
    eng = jnp.exp(-gam)
    ecl = jnp.exp(glast - gam)
    beta = a * kk
    abar = kk * jnp.exp(gam - logw)
    rbar = r * jnp.exp(gam)
    yb = jnp.concatenate([beta * eng, k2 * eng], axis=0).astype(BF16)
    bhat = beta * ecl
    khat = k2 * ecl
    strict, incl = strict_ref[...] > 0.0, incl_ref[...] > 0.0
    eye = eye_ref[...]
    heads = range(HEADS)
    hms = [hmask_ref[h:h + 1, :] for h in heads]
    abar_h = [abar * hms[h] for h in heads]
    vh = [(v * hms[h]).astype(BF16) for h in heads]
    pw, a_rb, a_rk, z = [], [], [], []
    for h in heads:
        xh = jnp.concatenate([abar_h[h], rbar * hms[h]], axis=0).astype(BF16)
        s = _dot_nt(xh, yb)
        pw.append(jnp.where(strict, -s[:tb, :tb], 0.0))
        a_ak = jnp.where(strict, s[:tb, tb:], 0.0).astype(BF16)
        a_rb.append(jnp.where(incl, s[tb:, :tb], 0.0).astype(BF16))
        a_rk.append(jnp.where(incl, s[tb:, tb:], 0.0).astype(BF16))
        z.append(jnp.concatenate([abar_h[h], _dot(a_ak, vh[h])], axis=1).astype(BF16))
        yield
    tinv = [eye + pw[h] for h in heads]
    for _ in range(int(math.log2(c)) - 1):
        pwb = [pw[h].astype(BF16) for h in heads]
        pw = [_dot(pwb[h], pwb[h]) for h in heads]
        yield
        tinv = [tinv[h] + _dot(tinv[h].astype(BF16), pw[h].astype(BF16)) for h in heads]
        yield
    wu_h = [_dot(tinv[h].astype(BF16), z[h]) for h in heads]
    yield
    om_h = [_dot(a_rb[h], wu_h[h].astype(BF16)) for h in heads]
    arkv_h = [_dot(a_rk[h], vh[h]) for h in heads]
    yield
    wu, om, arkv = sum(wu_h[1:], wu_h[0]), sum(om_h[1:], om_h[0]), sum(arkv_h[1:], arkv_h[0])
    wbar, u0 = wu[:, :GROUP], wu[:, GROUP:]
    omega = rbar - om[:, :GROUP]
    y0 = arkv - om[:, GROUP:]
    wb, bhb = wbar.astype(BF16), bhat.astype(BF16)
    vb, nub, khb = v.astype(BF16), (-u0).astype(BF16), khat.astype(BF16)
    x_c, psi = [], []
    for ci in range(nch):
        rows = slice(ci * c, (ci + 1) * c)
        x_c.append((_dot_tn(wb[rows], bhb[rows]) * bd_ref[...]).astype(BF16))
        psi.append(_dot_tn(jnp.concatenate([vb[rows], nub[rows]], axis=0),
                           jnp.concatenate([khb[rows], bhb[rows]], axis=0)) * bd_ref[...])
    yield
    mt = mt_ref[...]
    for ci in range(nch):
        stack_ref[:, ci * GROUP:(ci + 1) * GROUP] = mt.astype(BF16)
        gl = jnp.exp(gam_ref[0, ci * c + c - 1:ci * c + c, :])
        mt = mt * gl - _dot(mt.astype(BF16), x_c[ci]) + psi[ci]
        yield
    mt_ref[...] = mt
    omb = omega.astype(BF16)
    return y0 + jnp.concatenate([_dot_nt(omb[ci * c:(ci + 1) * c], stack_ref[:, ci * GROUP:(ci + 1) * GROUP])
                                 for ci in range(nch)], axis=0)


def _rwkv_body(pd_ref, mu_ref, w0_ref, wup_ref, a0_ref, aup_ref, gup_ref, kk_ref, ka_ref, rk_ref,
               lng_ref, lnb_ref, g64_ref, s64_ref, *rest):
    consts, (o_ref, mt_ref, carry_ref, gam_ref, stack_ref) = rest[:6], rest[6:]
    tb = pd_ref.shape[0]
    p = pd_ref[...]
    rowid = lax.broadcasted_iota(jnp.int32, p.shape, 0)
    prev = jnp.where(rowid == 0, carry_ref[...], pltpu.roll(p, 1, axis=0))
    carry_ref[...] = p[tb - 1:tb, :]
    p = p + mu_ref[...] * (prev - p)
    r, k, v = p[:, 0:GROUP], p[:, GROUP:2 * GROUP], p[:, 2 * GROUP:3 * GROUP]
    low = p[:, 3 * GROUP:3 * GROUP + LANE]
    logw = -RW_DECAY_SCALE * _sigmoid(w0_ref[...] + _dot(jnp.tanh(low).astype(BF16), wup_ref[...]))
    a = _sigmoid(a0_ref[...] + _dot(low.astype(BF16), aup_ref[...]))
    g = _dot(_sigmoid(low).astype(BF16), gup_ref[...])
    kk = k * kk_ref[...]
    kk = kk * lax.rsqrt(jnp.maximum(_dot_x_exact(kk * kk, s64_ref[...], 2), 1e-24))
    k2 = k * (1.0 + (a - 1.0) * ka_ref[...])
    yield
    y = yield from _rwkv_core(r, logw, k2, v, kk, a, consts, mt_ref, gam_ref, stack_ref)
    yield
    mean = _dot_x_exact(y, g64_ref[...], 2)
    d = y - mean
    yield
    var = _dot_x_exact(d * d, g64_ref[...], 2)
    yn = d * lax.rsqrt(var + RW_LN_EPS) * lng_ref[...] + lnb_ref[...]
    bonus = _dot_x_exact(r * k2 * rk_ref[...], s64_ref[...], 2) * v
    o_ref[...] = ((yn + bonus) * g).astype(BF16)


def _rwkv_operands(tb, mu, w0, w_up, a0, a_up, g_up, k_k, k_a, r_k, ln_g, ln_b):
    nch = tb // RW_CHUNK
    row = lambda z: z.reshape(1, -1)
    low = lambda w, start: jnp.zeros((LANE, GROUP), F32).at[start:start + w.shape[0]].set(w).astype(BF16)
    extra = [row(mu), row(w0), low(w_up, 0), row(a0), low(a_up, 32), low(g_up, 64), row(k_k), row(k_a),
             row(r_k), row(ln_g), row(ln_b), _group_mean_matrix(GROUP, HEAD_V),
             _group_sum_matrix(GROUP, HEAD_V), *_rwkv_consts(tb)]
    scratch = [pltpu.VMEM((GROUP, GROUP), F32), pltpu.VMEM((1, D_W), F32),
               pltpu.VMEM((1, tb, GROUP), F32), pltpu.VMEM((GROUP, nch * GROUP), BF16)]
    return extra, scratch


def _recurrent_body(*refs, n_rw, n_gh, n_rw_scratch, layer):
    pd_ref, pb_ref, pc_ref = refs[:3]
    rw_extra = refs[3:3 + n_rw]
    gh_extra = refs[3 + n_rw:3 + n_rw + n_gh]
    od_ref, ob_ref, oc_ref = refs[3 + n_rw + n_gh:6 + n_rw + n_gh]
    scratch = refs[6 + n_rw + n_gh:]
    gh_scratch = scratch[n_rw_scratch:]

    @pl.when(pl.program_id(1) == 0)
    def _():
        for ref in (scratch[0], scratch[1], gh_scratch[0], gh_scratch[N_GLA_SCRATCH]):
            ref[...] = jnp.zeros_like(ref)

    live = [_rwkv_body(pd_ref, *rw_extra, od_ref, *scratch[:n_rw_scratch]),
            _gla_hgrn_body(pb_ref, pc_ref, *gh_extra, ob_ref, oc_ref, *gh_scratch, layer=layer)]
    while live:
        for gen in list(live):
            if next(gen, StopIteration) is StopIteration:
                live.remove(gen)


def _recurrent_mixers(pb, pc, pd, batch, seq, gla_hgrn_params, rwkv_params, layer):
    tb = min(TIME_BLOCK, seq)
    nt = seq // tb
    rw_extra, rw_scratch = _rwkv_operands(tb, *rwkv_params)
    gh_extra, gh_scratch = _gla_hgrn_operands(tb, *gla_hgrn_params)
    row = lambda width: pl.BlockSpec((tb, width), lambda b, t: (b * nt + t, 0))
    body = functools.partial(_recurrent_body, n_rw=len(rw_extra), n_gh=len(gh_extra),
                             n_rw_scratch=len(rw_scratch), layer=layer)
    o_d, o_b, o_c = pl.pallas_call(
        body,
        grid=(batch, nt),
        in_specs=[row(D_W), row(B_W), row(C_W)] + [_const_spec(e.shape) for e in rw_extra + gh_extra],
        out_specs=[row(GROUP)] * 3,
        out_shape=[jax.ShapeDtypeStruct((batch * seq, GROUP), BF16)] * 3,
        scratch_shapes=rw_scratch + gh_scratch,
        compiler_params=_params("parallel", "arbitrary"),
        name="recurrent_mixers",
    )(pd, pb, pc, *rw_extra, *gh_extra)
    return o_b, o_c, o_d


def _relayout_w_in(w_in):
    a_end, b_end, c_end = 768, 768 + 784, 768 + 784 + 1024
    wb = w_in[:, a_end:b_end]
    pad = jnp.zeros((w_in.shape[0], LANE - GLA_RANK), w_in.dtype)
    wb = jnp.concatenate([wb[:, :528], pad, wb[:, 528:]], axis=1)
    return jnp.concatenate([w_in[:, :a_end], wb, w_in[:, b_end:c_end], w_in[:, c_end:]], axis=1).astype(BF16)


def kernel(x, norm_mix_g, w_in, da_q_norm_g, da_k_norm_g, da_lambda_q1, da_lambda_k1, da_lambda_q2, da_lambda_k2, da_out_norm_g, gla_gate_up, gla_gate_b, gla_out_norm_g, hgrn_lb_logits, hgrn_out_norm_g, rw_shift_mu, rw_w0, rw_w_up, rw_a0, rw_a_up, rw_g_up, rw_k_k, rw_k_a, rw_r_k, rw_ln_g, rw_ln_b, w_out, norm_mlp_g, w_mlp_up, w_mlp_down):
    batch, seq, _ = x.shape
    xf = x.reshape(batch * seq, D_MODEL)
    blk = min(ATT_BLOCK, seq)
    kpos = _alibi_key_terms(blk)
    for l in range(DEPTH):
        lam_init = 0.8 - 0.6 * math.exp(-0.3 * l)
        qt, k, vt, pb, pc, pd = _in_proj(xf, norm_mix_g[l], _relayout_w_in(w_in[l]), batch, seq, blk,
                                          da_q_norm_g[l], da_k_norm_g[l], kpos)
        o_a = _diff_attention(qt, k, vt, batch, seq, blk, da_lambda_q1[l], da_lambda_k1[l], da_lambda_q2[l],
                              da_lambda_k2[l], da_out_norm_g[l], lam_init)
        o_b, o_c, o_d = _recurrent_mixers(
            pb, pc, pd, batch, seq,
            (gla_gate_up[l], gla_gate_b[l], gla_out_norm_g[l], hgrn_lb_logits, hgrn_out_norm_g[l]),
            (rw_shift_mu[l], rw_w0[l], rw_w_up[l], rw_a0[l], rw_a_up[l], rw_g_up[l], rw_k_k[l], rw_k_a[l],
             rw_r_k[l].reshape(-1), rw_ln_g[l], rw_ln_b[l]), l)
        xf = _post(xf, (o_a, o_b, o_c, o_d), w_out[l].astype(BF16), norm_mlp_g[l],
                   w_mlp_up[l].astype(BF16), w_mlp_down[l].astype(BF16))
    return xf.reshape(batch, seq, D_MODEL)
```

```python
import functools
import math

import jax
import jax.numpy as jnp
import ml_dtypes
import numpy as np
from jax import lax
from jax.experimental import pallas as pl
from jax.experimental.pallas import tpu as pltpu

F32, BF16 = jnp.float32, jnp.bfloat16

D_MODEL = 1024
GROUP = 256
HEADS = 4
HEAD_V = 64
DA_QK = 32
GLA_K = 32
GLA_RANK = 16
GLA_NORMALIZER = 16.0
HG_K = 64
RW_DECAY_SCALE = 0.606531
RW_LN_EPS = 64e-5
D_FF = 4 * D_MODEL
RMS_EPS = 1e-6
MASK_VALUE = -1e30
TINY = 1e-30
DEPTH = 4

LANE = 128
A_W, B_W, C_W, D_W = 768, 896, 1024, 896
P_W = A_W + B_W + C_W + D_W

ROW_TILE = 512
TIME_BLOCK = 256
ATT_BLOCK = 256
ATT_UNROLL = 4
ATT_VA = 80
ATT_KA = 80
LOG2E = 1.4426950408889634
GLA_CHUNK = 32
GLA_SUB = 16
RW_CHUNK = 64
VMEM_LIMIT = 56 * 1024 * 1024


def _dot(a, b):
    return jnp.dot(a, b, preferred_element_type=F32)


def _dot_nt(a, b):
    return lax.dot_general(a, b, (((1,), (1,)), ((), ())), preferred_element_type=F32)


def _dot_tn(a, b):
    return lax.dot_general(a, b, (((0,), (0,)), ((), ())), preferred_element_type=F32)


def _split(x, parts):
    out, r = [], x
    for i in range(parts):
        h = r.astype(BF16)
        out.append(h)
        if i + 1 < parts:
            r = r - h.astype(F32)
    return out


def _dot_x_exact(x, m, parts):
    acc = None
    for h in _split(x, parts):
        d = _dot(h, m)
        acc = d if acc is None else acc + d
    return acc


def _dot_exact_x(m, x, parts):
    acc = None
    for h in _split(x, parts):
        d = _dot(m, h)
        acc = d if acc is None else acc + d
    return acc


def _sigmoid(x):
    return 1.0 / (1.0 + jnp.exp(-x))


def _silu(x):
    return x * _sigmoid(x)


def _const_spec(shape):
    zeros = (0,) * len(shape)
    return pl.BlockSpec(shape, lambda *_: zeros)


def _chunk_row_bcast(ref, idx, row, chunk):
    n_rows, width = ref.shape[1], ref.shape[2]
    pieces = []
    for c in range(n_rows // chunk):
        r = c * chunk + row
        pieces.append(jnp.broadcast_to(ref[idx, r:r + 1, :], (chunk, width)))
    return jnp.concatenate(pieces, axis=0)


def _params(*sem):
    return pltpu.CompilerParams(dimension_semantics=sem, vmem_limit_bytes=VMEM_LIMIT)


def _in_proj_body(x_ref, g_ref, w_ref, gq_ref, gk_ref, g32_ref, kpos_ref,
                  qt_ref, k_ref, vt_ref, pb_ref, pc_ref, pd_ref):
    x = x_ref[...]
    h = (x * lax.rsqrt(jnp.mean(x * x, axis=-1, keepdims=True) + RMS_EPS) * g_ref[...]).astype(BF16)
    pa = _dot(h, w_ref[:, 0:A_W])
    q, k = pa[:, 0:GROUP], pa[:, GROUP:2 * GROUP]
    qm = _dot_x_exact(q * q, g32_ref[...], 2)
    km = _dot_x_exact(k * k, g32_ref[...], 2)
    off = A_W
    for o_ref in (pb_ref, pc_ref, pd_ref):
        n = o_ref.shape[1]
        o_ref[...] = _dot(h, w_ref[:, off:off + n])
        off += n
    _attention_layouts(pa, qm, km, gq_ref, gk_ref, kpos_ref, qt_ref, k_ref, vt_ref)


def _in_proj(x, g, w, batch, seq, blk, gq, gk, kpos):
    tm = min(ROW_TILE, seq)
    nt = seq // tm
    n = batch * seq
    row = lambda width: pl.BlockSpec((tm, width), lambda b, t: (b * nt + t, 0))
    tile = lambda gain: jnp.tile(gain, GROUP // DA_QK).reshape(1, GROUP)
    return pl.pallas_call(
        _in_proj_body,
        grid=(batch, nt),
        in_specs=[row(D_MODEL), _const_spec((1, D_MODEL)), _const_spec((D_MODEL, P_W)),
                  _const_spec((1, GROUP)), _const_spec((1, GROUP)), _const_spec((GROUP, GROUP)),
                  _const_spec(kpos.shape)],
        out_specs=[pl.BlockSpec((1, 2, HEADS * ATT_KA, tm), lambda b, t: (b, 0, 0, t)),
                   pl.BlockSpec((1, HEADS, tm, ATT_KA), lambda b, t: (b, 0, t, 0)),
                   pl.BlockSpec((1, HEADS, tm // blk, ATT_VA, blk), lambda b, t: (b, 0, t, 0, 0)),
                   row(B_W), row(C_W), row(D_W)],
        out_shape=[jax.ShapeDtypeStruct((batch, 2, HEADS * ATT_KA, seq), BF16),
                   jax.ShapeDtypeStruct((batch, HEADS, seq, ATT_KA), BF16),
                   jax.ShapeDtypeStruct((batch, HEADS, seq // blk, ATT_VA, blk), BF16)]
                  + [jax.ShapeDtypeStruct((n, wd), F32) for wd in (B_W, C_W, D_W)],
        compiler_params=_params("parallel", "parallel"),
        name="in_proj",
    )(x, g.reshape(1, D_MODEL), w, tile(gq), tile(gk), _group_mean_matrix(GROUP, DA_QK), kpos)


def _post_body(x_ref, oa_ref, ob_ref, oc_ref, od_ref, wo_ref, g_ref, wu_ref, wd_ref, out_ref):
    mixed = jnp.concatenate([r[...] for r in (oa_ref, ob_ref, oc_ref, od_ref)], axis=1)
    x1 = x_ref[...] + _dot(mixed, wo_ref[...])
    h = (x1 * lax.rsqrt(jnp.mean(x1 * x1, axis=-1, keepdims=True) + RMS_EPS) * g_ref[...]).astype(BF16)
    y = x1
    for f in range(D_FF // D_MODEL):
        cols = slice(f * D_MODEL, (f + 1) * D_MODEL)
        u = jnp.square(jnp.maximum(_dot(h, wu_ref[:, cols]), 0.0)).astype(BF16)
        y = y + _dot(u, wd_ref[cols, :])
    out_ref[...] = y


def _post(x, outs, w_out, g, w_up, w_down):
    n = x.shape[0]
    tm = min(ROW_TILE, n)
    row = lambda width: pl.BlockSpec((tm, width), lambda i: (i, 0))
    return pl.pallas_call(
        _post_body,
        grid=(n // tm,),
        in_specs=[row(D_MODEL)] + [row(GROUP)] * 4 + [
            _const_spec((D_MODEL, D_MODEL)), _const_spec((1, D_MODEL)),
            _const_spec((D_MODEL, D_FF)), _const_spec((D_FF, D_MODEL))],
        out_specs=row(D_MODEL),
        out_shape=jax.ShapeDtypeStruct((n, D_MODEL), F32),
        compiler_params=_params("parallel"),
        name="post",
    )(x, *outs, w_out, g.reshape(1, D_MODEL), w_up, w_down)


def _group_mean_matrix(width, group):
    i = jnp.arange(width) // group
    return ((i[:, None] == i[None, :]).astype(F32) / group).astype(BF16)


def _group_sum_matrix(width, group):
    i = jnp.arange(width) // group
    return (i[:, None] == i[None, :]).astype(BF16)


def _attention_layouts(pa, qm, km, gq_ref, gk_ref, kpos_ref, qt_ref, k_ref, vt_ref):
    tm = pa.shape[0]
    blk = vt_ref.shape[-1]
    q, k = pa[:, 0:GROUP], pa[:, GROUP:2 * GROUP]
    q_t = (q * lax.rsqrt(qm + RMS_EPS) * gq_ref[...] * (DA_QK ** -0.5 * LOG2E)).T
    comp = (lax.broadcasted_iota(jnp.int32, q_t.shape, 0) // DA_QK) % 2
    n_extra = ATT_KA - 2 * DA_QK
    q_extra = (lax.broadcasted_iota(jnp.int32, (n_extra, tm), 0) < 3).astype(BF16)
    for c in range(2):
        q_c = jnp.where(comp == c, q_t, 0.0).astype(BF16)
        for h in range(HEADS):
            qt_ref[0, c, h * ATT_KA:h * ATT_KA + 2 * DA_QK, :] = q_c[h * HEAD_V:(h + 1) * HEAD_V]
            qt_ref[0, c, h * ATT_KA + 2 * DA_QK:(h + 1) * ATT_KA, :] = q_extra
    kn = (k * lax.rsqrt(km + RMS_EPS) * gk_ref[...]).astype(BF16)
    v_t = pa[:, 2 * GROUP:3 * GROUP].T.astype(BF16)
    ones_row = (lax.broadcasted_iota(jnp.int32, (ATT_VA - HEAD_V, blk), 0) == 0).astype(BF16)
    for h in range(HEADS):
        k_ref[0, h, :, 0:2 * DA_QK] = kn[:, h * HEAD_V:(h + 1) * HEAD_V]
        for sb in range(tm // blk):
            k_ref[0, h, sb * blk:(sb + 1) * blk, 2 * DA_QK:ATT_KA] = kpos_ref[h]
        for sb in range(tm // blk):
            vt_ref[0, h, sb, 0:HEAD_V, :] = v_t[h * HEAD_V:(h + 1) * HEAD_V, sb * blk:(sb + 1) * blk]
            vt_ref[0, h, sb, HEAD_V:ATT_VA, :] = ones_row


def _attn_body(qt_ref, k_ref, vt_ref, td_ref, lp_ref, go_ref, o_ref, m_ref, acc_ref, s_ref, *,
               lam_init, slopes, blk):
    i = pl.program_id(1)
    chains = [(h, c) for h in range(HEADS) for c in range(2)]

    def stage(cur, cur_slot, first, nxt=None, nxt_slot=None):
        for n, (h, c) in enumerate(chains):
            if nxt is not None:
                q_hc = qt_ref[0, c, h * ATT_KA:(h + 1) * ATT_KA, :]
                s_ref[nxt_slot, n] = _dot(k_ref[0, h, nxt], q_hc)
            if cur is None:
                continue
            off = -cur.astype(F32) * (slopes[h] * LOG2E * blk)
            s = s_ref[cur_slot, n]
            if first:
                s = s + td_ref[...]
            smax = jnp.max(s, axis=0, keepdims=True) - off
            if first:
                mx = smax
            else:
                m = m_ref[n]
                mx = jnp.maximum(m, smax)
            m_ref[n] = mx
            p = jnp.exp2(s - (mx + off)).astype(BF16)
            pv = _dot(vt_ref[0, h, cur], p)
            acc_ref[n] = pv if first else jnp.exp2(m - mx) * acc_ref[n] + pv

    stage(None, None, False, nxt=i, nxt_slot=0)
    stage(i, 0, True, nxt=0, nxt_slot=1)

    def body(u, carry):
        for d in range(ATT_UNROLL):
            stage(ATT_UNROLL * u + d, (d + 1) % 2, False, nxt=ATT_UNROLL * u + d + 1, nxt_slot=d % 2)
        return carry

    trips = i // ATT_UNROLL
    lax.fori_loop(0, trips, body, 0)
    done = trips * ATT_UNROLL
    for rem in range(1, ATT_UNROLL):

        @pl.when(i - done == rem)
        def _(rem=rem):
            for d in range(rem):
                more = d + 1 < rem
                stage(done + d, (d + 1) % 2, False, nxt=done + d + 1 if more else None,
                      nxt_slot=d % 2 if more else None)

    lp = lp_ref[...]
    lam = (jnp.exp(jnp.sum(lp[0:1] * lp[1:2], axis=-1, keepdims=True))
           - jnp.exp(jnp.sum(lp[2:3] * lp[3:4], axis=-1, keepdims=True)) + lam_init)
    outs = []
    for h in range(HEADS):
        a1, a2 = acc_ref[2 * h], acc_ref[2 * h + 1]
        o = a1[0:HEAD_V] / a1[HEAD_V:HEAD_V + 1] - lam * (a2[0:HEAD_V] / a2[HEAD_V:HEAD_V + 1])
        outs.append(o * lax.rsqrt(jnp.mean(o * o, axis=0, keepdims=True) + RMS_EPS) * go_ref[...]
                    * (1.0 - lam_init))
    o_ref[...] = jnp.concatenate(outs, axis=0).T.astype(BF16)


def _alibi_slopes():
    return tuple(2.0 ** (-8.0 * (h + 1) / HEADS) for h in range(HEADS))


def _alibi_key_terms(blk):
    kpos = np.zeros((HEADS, blk, ATT_KA - 2 * DA_QK), ml_dtypes.bfloat16)
    for h, s in enumerate(_alibi_slopes()):
        r = np.float32(s * LOG2E) * np.arange(blk, dtype=np.float32)
        for part in range(3):
            kpos[h, :, part] = r.astype(ml_dtypes.bfloat16)
            r = r - kpos[h, :, part].astype(np.float32)
    return jnp.asarray(kpos)


def _diff_attention(qt, k, vt, batch, seq, blk, lq1, lk1, lq2, lk2, g_out, lam_init):
    nb = seq // blk
    slopes = _alibi_slopes()
    kh = k.reshape(batch, HEADS, nb, blk, ATT_KA)
    rel = jnp.arange(blk)[None, :] - jnp.arange(blk)[:, None]
    td = jnp.where(rel >= 0, 0.0, MASK_VALUE).astype(F32)
    lp = jnp.stack([lq1, lk1, lq2, lk2])
    go = jnp.broadcast_to(g_out[:, None], (HEAD_V, blk))
    body = functools.partial(_attn_body, lam_init=lam_init, slopes=slopes, blk=blk)
    return pl.pallas_call(
        body,
        grid=(batch, nb),
        in_specs=[
            pl.BlockSpec((1, 2, HEADS * ATT_KA, blk), lambda b, i: (b, 0, 0, i)),
            pl.BlockSpec((1, HEADS, nb, blk, ATT_KA), lambda b, i: (b, 0, 0, 0, 0)),
            pl.BlockSpec((1, HEADS, nb, ATT_VA, blk), lambda b, i: (b, 0, 0, 0, 0)),
            _const_spec((blk, blk)),
            _const_spec((4, DA_QK)), _const_spec((HEAD_V, blk))],
        out_specs=pl.BlockSpec((blk, GROUP), lambda b, i: (b * nb + i, 0)),
        out_shape=jax.ShapeDtypeStruct((batch * seq, GROUP), BF16),
        scratch_shapes=[pltpu.VMEM((2 * HEADS, 1, blk), F32), pltpu.VMEM((2 * HEADS, ATT_VA, blk), F32),
                        pltpu.VMEM((2, 2 * HEADS, blk, blk), F32)],
        compiler_params=_params("parallel", "arbitrary"),
        name="diff_attention",
    )(qt, kh, vt, td, lp, go)


def _gla_consts(tb, hk):
    c, sub, nc, kh = GLA_CHUNK, GLA_SUB, tb // GLA_CHUNK, hk // HEADS
    nsub = c // sub
    width = nsub * HEADS * sub
    t = jnp.arange(tb)
    lseg = ((t[:, None] // c == t[None, :] // c) & (t[None, :] <= t[:, None])).astype(BF16)
    r = jnp.arange(sub * hk)
    col = jnp.arange(width)
    ind = ((r[:, None] // hk == col[None, :] % sub)
           & ((r[:, None] % hk) // kh == (col[None, :] // sub) % HEADS)).astype(BF16)
    submask = ((t[:, None] % c) // sub == col[None, :] // (HEADS * sub)).astype(F32)
    vr = jnp.arange(nc * width)
    row_h = (vr[:, None] // sub) % HEADS
    vmask = (row_h == jnp.arange(GROUP)[None, :] // HEAD_V).astype(BF16)
    kmask = (((vr[:, None] % width) // (HEADS * sub) == 0)
             & (row_h == jnp.arange(hk)[None, :] // kh)).astype(BF16)
    bdmask = (jnp.arange(GROUP)[:, None] // HEAD_V == jnp.arange(hk)[None, :] // kh).astype(F32)
    return lseg, ind, submask, vmask, kmask, bdmask


class _GlaStream:
    def __init__(self, q, k, lf, v, consts, scratch):
        self.q, self.k, self.lf, self.v = q, k, lf, v
        self.lseg, self.ind, self.submask, self.vmask, self.kmask, self.bdmask = consts
        self.st, self.tcat, self.bk, self.stack = scratch


def _gla_core(streams):
    c, sub = GLA_CHUNK, GLA_SUB
    assert c == 2 * sub
    tb = streams[0].q.shape[0]
    nc = tb // c
    width = 2 * HEADS * sub
    chunk = lambda x, ci, n=c: x[ci * n:(ci + 1) * n]
    for s in streams:
        s.b = _dot_exact_x(s.lseg[...], s.lf, 3) * LOG2E
    yield
    for s in streams:
        hk = s.q.shape[1]
        s.bk[0] = s.b
        s.bk[1] = s.k
        s.row = lax.broadcasted_iota(jnp.int32, (tb, hk), 0)
        rowmod = s.row % sub
        for jj in range(sub):
            bj = _chunk_row_bcast(s.bk, 0, jj, sub)
            kj = _chunk_row_bcast(s.bk, 1, jj, sub)
            t = s.q * kj * jnp.exp2(s.b - bj)
            s.tcat[:, jj * hk:(jj + 1) * hk] = jnp.where(rowmod >= jj, t, 0.0).astype(BF16)
            if jj % 4 == 3:
                yield
    yield
    for s in streams:
        s.a = _dot(s.tcat[...], s.ind[...]) * s.submask[...]
    yield
    for s in streams:
        bref = _chunk_row_bcast(s.bk, 0, sub - 1, c)
        second = s.row % c >= sub
        s.qm = jnp.where(second, s.q * jnp.exp2(s.b - bref), 0.0).astype(BF16)
        kt = jnp.where(second, 0.0, s.k * jnp.exp2(bref - s.b)).astype(BF16)
        s.k_exp = jnp.concatenate([kt[ci * c:ci * c + sub] for ci in range(nc) for _ in range(2 * HEADS)],
                                  axis=0) * s.kmask[...]
    yield
    for s in streams:
        s.a_off = [_dot_nt(chunk(s.qm, ci), chunk(s.k_exp, ci, width)) for ci in range(nc)]
    yield
    for s in streams:
        s.a_c = [(chunk(s.a, ci) + s.a_off[ci]).astype(BF16) for ci in range(nc)]
        s.vb = s.v.astype(BF16)
        s.v_bd = jnp.concatenate([s.vb[ci * c + h * sub:ci * c + (h + 1) * sub]
                                  for ci in range(nc) for h in range(2) for _ in range(HEADS)],
                                 axis=0) * s.vmask[...]
    yield
    for s in streams:
        s.o = [_dot(s.a_c[ci], chunk(s.v_bd, ci, width)) for ci in range(nc)]
    yield
    for s in streams:
        blast = _chunk_row_bcast(s.bk, 0, c - 1, c)
        s.qe = (s.q * jnp.exp2(s.b)).astype(BF16)
        s.ke = (s.k * jnp.exp2(blast - s.b)).astype(BF16)
    yield
    for s in streams:
        s.pt = [_dot_tn(s.vb[ci * c:(ci + 1) * c], s.ke[ci * c:(ci + 1) * c]) * s.bdmask[...]
                for ci in range(nc)]
    yield
    for s in streams:
        hk = s.q.shape[1]
        st = s.st[...]
        for ci in range(nc):
            s.stack[:, ci * hk:(ci + 1) * hk] = st.astype(BF16)
            st = st * jnp.exp2(s.bk[0, ci * c + c - 1:ci * c + c, :]) + s.pt[ci]
        s.st[...] = st
    yield
    return [jnp.concatenate([s.o[ci] + _dot_nt(chunk(s.qe, ci), s.stack[:, ci * s.q.shape[1]:(ci + 1) * s.q.shape[1]])
                             for ci in range(nc)], axis=0) for s in streams]


def _gla_scratch(tb, hk):
    nc = tb // GLA_CHUNK
    return [pltpu.VMEM((GROUP, hk), F32), pltpu.VMEM((tb, GLA_SUB * hk), BF16),
            pltpu.VMEM((2, tb, hk), F32), pltpu.VMEM((GROUP, nc * hk), BF16)]


N_GLA_CONSTS = 6
N_GLA_SCRATCH = 4


def _gla_hgrn_body(pb_ref, pc_ref, gup_ref, gb_ref, gout_b_ref, lg_ref, gout_c_ref, g64_ref, *rest, layer):
    consts_b, consts_c = rest[:N_GLA_CONSTS], rest[N_GLA_CONSTS:2 * N_GLA_CONSTS]
    ob_ref, oc_ref = rest[2 * N_GLA_CONSTS:2 * N_GLA_CONSTS + 2]
    scratch = rest[2 * N_GLA_CONSTS + 2:]
    scratch_b, scratch_c = scratch[:N_GLA_SCRATCH], scratch[N_GLA_SCRATCH:]
    hk = HEADS * GLA_K
    gate_lo, og_lo = 2 * hk + GROUP, 2 * hk + GROUP + LANE
    logit = _dot(pb_ref[:, gate_lo:og_lo].astype(BF16), gup_ref[...]) + gb_ref[...]
    lf_b = (jnp.minimum(logit, 0.0) - jnp.log(1.0 + jnp.exp(-jnp.abs(logit)))) * (1.0 / GLA_NORMALIZER)
    gla = _GlaStream(pb_ref[:, 0:hk] * (GLA_K ** -0.5), pb_ref[:, hk:2 * hk], lf_b,
                     pb_ref[:, 2 * hk:2 * hk + GROUP], consts_b, scratch_b)
    lg = lg_ref[...]
    e = jnp.exp(lg - jnp.max(lg, axis=0, keepdims=True))
    p = e / jnp.sum(e, axis=0, keepdims=True)
    cs = p[0:1]
    for i in range(1, layer + 1):
        cs = cs + p[i:i + 1]
    lb = cs - p[0:1]
    z = pc_ref[:, GROUP:2 * GROUP]
    forget = lb + (1.0 - lb) * _sigmoid(z)
    hgrn = _GlaStream(_silu(pc_ref[:, 0:GROUP]), (1.0 - lb) * _sigmoid(-z),
                      jnp.log(jnp.maximum(forget, TINY)), pc_ref[:, 2 * GROUP:3 * GROUP], consts_c, scratch_c)
    o_b, o_c = yield from _gla_core([gla, hgrn])
    for o, og, gout_ref, o_ref in ((o_b, pb_ref[:, og_lo:og_lo + GROUP], gout_b_ref, ob_ref),
                                   (o_c, pc_ref[:, 3 * GROUP:4 * GROUP], gout_c_ref, oc_ref)):
        ms = _dot_x_exact(o * o, g64_ref[...], 2)
        o_ref[...] = (o * lax.rsqrt(ms + RMS_EPS) * gout_ref[...] * _silu(og)).astype(BF16)


def _gla_hgrn_operands(tb, gate_up, gate_b, g_out_b, lb_logits, g_out_c):
    hk_b, hk_c = HEADS * GLA_K, HEADS * HG_K
    gup = jnp.zeros((LANE, hk_b), F32).at[:GLA_RANK].set(gate_up).astype(BF16)
    tile = lambda g: jnp.tile(g, HEADS).reshape(1, GROUP)
    extra = [gup, gate_b.reshape(1, hk_b), tile(g_out_b), lb_logits, tile(g_out_c),
             _group_mean_matrix(GROUP, HEAD_V), *_gla_consts(tb, hk_b), *_gla_consts(tb, hk_c)]
    return extra, _gla_scratch(tb, hk_b) + _gla_scratch(tb, hk_c)


def _rwkv_consts(tb):
    c, nch = RW_CHUNK, tb // RW_CHUNK
    t = jnp.arange(tb)
    same = t[:, None] // c == t[None, :] // c
    lseg = (same & (t[None, :] <= t[:, None])).astype(BF16)
    strict = (same & (t[None, :] < t[:, None])).astype(F32)
    incl = (same & (t[None, :] <= t[:, None])).astype(F32)
    eye = jnp.eye(tb, dtype=F32)
    hmask = (jnp.arange(HEADS)[:, None] == jnp.arange(GROUP)[None, :] // HEAD_V).astype(F32)
    bd = (jnp.arange(GROUP)[:, None] // HEAD_V == jnp.arange(GROUP)[None, :] // HEAD_V).astype(F32)
    return lseg, strict, incl, eye, hmask, bd


def _rwkv_core(r, logw, k2, v, kk, a, consts, mt_ref, gam_ref, stack_ref):
    lseg_ref, strict_ref, incl_ref, eye_ref, hmask_ref, bd_ref = consts
    tb = r.shape[0]
    c = RW_CHUNK
    nch = tb // c
    gam = _dot_exact_x(lseg_ref[...], logw, 3)
    yield
    gam_ref[0] = gam
    glast = _chunk_row_bcast(gam_ref, 0, c - 1, c)
    eng = jnp.exp(-gam)
    ecl = jnp.exp(glast - gam)
    beta = a * kk
    abar = kk * jnp.exp(gam - logw)
    rbar = r * jnp.exp(gam)
    yb = jnp.concatenate([beta * eng, k2 * eng], axis=0).astype(BF16)
    bhat = beta * ecl
    khat = k2 * ecl
    strict, incl = strict_ref[...] > 0.0, incl_ref[...] > 0.0
    eye = eye_ref[...]
    heads = range(HEADS)
    hms = [hmask_ref[h:h + 1, :] for h in heads]
    abar_h = [abar * hms[h] for h in heads]
    vh = [(v * hms[h]).astype(BF16) for h in heads]
    pw, a_rb, a_rk, z = [], [], [], []
    for h in heads:
        xh = jnp.concatenate([abar_h[h], rbar * hms[h]], axis=0).astype(BF16)
        s = _dot_nt(xh, yb)
        pw.append(jnp.where(strict, -s[:tb, :tb], 0.0))
        a_ak = jnp.where(strict, s[:tb, tb:], 0.0).astype(BF16)
        a_rb.append(jnp.where(incl, s[tb:, :tb], 0.0).astype(BF16))
        a_rk.append(jnp.where(incl, s[tb:, tb:], 0.0).astype(BF16))
        z.append(jnp.concatenate([abar_h[h], _dot(a_ak, vh[h])], axis=1).astype(BF16))
        yield
    tinv = [eye + pw[h] for h in heads]
    for _ in range(int(math.log2(c)) - 1):
        pwb = [pw[h].astype(BF16) for h in heads]
        pw = [_dot(pwb[h], pwb[h]) for h in heads]
        yield
        tinv = [tinv[h] + _dot(tinv[h].astype(BF16), pw[h].astype(BF16)) for h in heads]
        yield
    wu_h = [_dot(tinv[h].astype(BF16), z[h]) for h in heads]
    yield
    om_h = [_dot(a_rb[h], wu_h[h].astype(BF16)) for h in heads]
    arkv_h = [_dot(a_rk[h], vh[h]) for h in heads]
    yield
    wu, om, arkv = sum(wu_h[1:], wu_h[0]), sum(om_h[1:], om_h[0]), sum(arkv_h[1:], arkv_h[0])
    wbar, u0 = wu[:, :GROUP], wu[:, GROUP:]
    omega = rbar - om[:, :GROUP]
    y0 = arkv - om[:, GROUP:]
    wb, bhb = wbar.astype(BF16), bhat.astype(BF16)
    vb, nub, khb = v.astype(BF16), (-u0).astype(BF16), khat.astype(BF16)
    x_c, psi = [], []
    for ci in range(nch):
        rows = slice(ci * c, (ci + 1) * c)
        x_c.append((_dot_tn(wb[rows], bhb[rows]) * bd_ref[...]).astype(BF16))
        psi.append(_dot_tn(jnp.concatenate([vb[rows], nub[rows]], axis=0),
                           jnp.concatenate([khb[rows], bhb[rows]], axis=0)) * bd_ref[...])
    yield
    mt = mt_ref[...]
    for ci in range(nch):
        stack_ref[:, ci * GROUP:(ci + 1) * GROUP] = mt.astype(BF16)
        gl = jnp.exp(gam_ref[0, ci * c + c - 1:ci * c + c, :])
        mt = mt * gl - _dot(mt.astype(BF16), x_c[ci]) + psi[ci]
        yield
    mt_ref[...] = mt
    omb = omega.astype(BF16)
    return y0 + jnp.concatenate([_dot_nt(omb[ci * c:(ci + 1) * c], stack_ref[:, ci * GROUP:(ci + 1) * GROUP])
                                 for ci in range(nch)], axis=0)


def _rwkv_body(pd_ref, mu_ref, w0_ref, wup_ref, a0_ref, aup_ref, gup_ref, kk_ref, ka_ref, rk_ref,
               lng_ref, lnb_ref, g64_ref, s64_ref, *rest):
    consts, (o_ref, mt_ref, carry_ref, gam_ref, stack_ref) = rest[:6], rest[6:]
    tb = pd_ref.shape[0]
    p = pd_ref[...]
    rowid = lax.broadcasted_iota(jnp.int32, p.shape, 0)
    prev = jnp.where(rowid == 0, carry_ref[...], pltpu.roll(p, 1, axis=0))
    carry_ref[...] = p[tb - 1:tb, :]
    p = p + mu_ref[...] * (prev - p)
    r, k, v = p[:, 0:GROUP], p[:, GROUP:2 * GROUP], p[:, 2 * GROUP:3 * GROUP]
    low = p[:, 3 * GROUP:3 * GROUP + LANE]
    logw = -RW_DECAY_SCALE * _sigmoid(w0_ref[...] + _dot(jnp.tanh(low).astype(BF16), wup_ref[...]))
    a = _sigmoid(a0_ref[...] + _dot(low.astype(BF16), aup_ref[...]))
    g = _dot(_sigmoid(low).astype(BF16), gup_ref[...])
    kk = k * kk_ref[...]
    kk = kk * lax.rsqrt(jnp.maximum(_dot_x_exact(kk * kk, s64_ref[...], 2), 1e-24))
    k2 = k * (1.0 + (a - 1.0) * ka_ref[...])
    yield
    y = yield from _rwkv_core(r, logw, k2, v, kk, a, consts, mt_ref, gam_ref, stack_ref)
    yield
    mean = _dot_x_exact(y, g64_ref[...], 2)
    d = y - mean
    yield
    var = _dot_x_exact(d * d, g64_ref[...], 2)
    yn = d * lax.rsqrt(var + RW_LN_EPS) * lng_ref[...] + lnb_ref[...]
    bonus = _dot_x_exact(r * k2 * rk_ref[...], s64_ref[...], 2) * v
    o_ref[...] = ((yn + bonus) * g).astype(BF16)


def _rwkv_operands(tb, mu, w0, w_up, a0, a_up, g_up, k_k, k_a, r_k, ln_g, ln_b):
    nch = tb // RW_CHUNK
    row = lambda z: z.reshape(1, -1)
    low = lambda w, start: jnp.zeros((LANE, GROUP), F32).at[start:start + w.shape[0]].set(w).astype(BF16)
    extra = [row(mu), row(w0), low(w_up, 0), row(a0), low(a_up, 32), low(g_up, 64), row(k_k), row(k_a),
             row(r_k), row(ln_g), row(ln_b), _group_mean_matrix(GROUP, HEAD_V),
             _group_sum_matrix(GROUP, HEAD_V), *_rwkv_consts(tb)]
    scratch = [pltpu.VMEM((GROUP, GROUP), F32), pltpu.VMEM((1, D_W), F32),
               pltpu.VMEM((1, tb, GROUP), F32), pltpu.VMEM((GROUP, nch * GROUP), BF16)]
    return extra, scratch


def _recurrent_body(*refs, n_rw, n_gh, n_rw_scratch, layer):
    pd_ref, pb_ref, pc_ref = refs[:3]
    rw_extra = refs[3:3 + n_rw]
    gh_extra = refs[3 + n_rw:3 + n_rw + n_gh]
    od_ref, ob_ref, oc_ref = refs[3 + n_rw + n_gh:6 + n_rw + n_gh]
    scratch = refs[6 + n_rw + n_gh:]
    gh_scratch = scratch[n_rw_scratch:]

    @pl.when(pl.program_id(1) == 0)
    def _():
        for ref in (scratch[0], scratch[1], gh_scratch[0], gh_scratch[N_GLA_SCRATCH]):
            ref[...] = jnp.zeros_like(ref)

    live = [_rwkv_body(pd_ref, *rw_extra, od_ref, *scratch[:n_rw_scratch]),
            _gla_hgrn_body(pb_ref, pc_ref, *gh_extra, ob_ref, oc_ref, *gh_scratch, layer=layer)]
    while live:
        for gen in list(live):
            if next(gen, StopIteration) is StopIteration:
                live.remove(gen)


def _recurrent_mixers(pb, pc, pd, batch, seq, gla_hgrn_params, rwkv_params, layer):
    tb = min(TIME_BLOCK, seq)
    nt = seq // tb
    rw_extra, rw_scratch = _rwkv_operands(tb, *rwkv_params)
    gh_extra, gh_scratch = _gla_hgrn_operands(tb, *gla_hgrn_params)
    row = lambda width: pl.BlockSpec((tb, width), lambda b, t: (b * nt + t, 0))
    body = functools.partial(_recurrent_body, n_rw=len(rw_extra), n_gh=len(gh_extra),
                             n_rw_scratch=len(rw_scratch), layer=layer)
    o_d, o_b, o_c = pl.pallas_call(
        body,
        grid=(batch, nt),
        in_specs=[row(D_W), row(B_W), row(C_W)] + [_const_spec(e.shape) for e in rw_extra + gh_extra],
        out_specs=[row(GROUP)] * 3,
        out_shape=[jax.ShapeDtypeStruct((batch * seq, GROUP), BF16)] * 3,
        scratch_shapes=rw_scratch + gh_scratch,
        compiler_params=_params("parallel", "arbitrary"),
        name="recurrent_mixers",
    )(pd, pb, pc, *rw_extra, *gh_extra)
    return o_b, o_c, o_d


def _relayout_w_in(w_in):
    a_end, b_end, c_end = 768, 768 + 784, 768 + 784 + 1024
    wb = w_in[:, a_end:b_end]
    pad = jnp.zeros((w_in.shape[0], LANE - GLA_RANK), w_in.dtype)
    wb = jnp.concatenate([wb[:, :528], pad, wb[:, 528:]], axis=1)
    return jnp.concatenate([w_in[:, :a_end], wb, w_in[:, b_end:c_end], w_in[:, c_end:]], axis=1).astype(BF16)


def kernel(x, norm_mix_g, w_in, da_q_norm_g, da_k_norm_g, da_lambda_q1, da_lambda_k1, da_lambda_q2, da_lambda_k2, da_out_norm_g, gla_gate_up, gla_gate_b, gla_out_norm_g, hgrn_lb_logits, hgrn_out_norm_g, rw_shift_mu, rw_w0, rw_w_up, rw_a0, rw_a_up, rw_g_up, rw_k_k, rw_k_a, rw_r_k, rw_ln_g, rw_ln_b, w_out, norm_mlp_g, w_mlp_up, w_mlp_down):
    batch, seq, _ = x.shape
    xf = x.reshape(batch * seq, D_MODEL)
    blk = min(ATT_BLOCK, seq)
    kpos = _alibi_key_terms(blk)
    for l in range(DEPTH):
        lam_init = 0.8 - 0.6 * math.exp(-0.3 * l)
        qt, k, vt, pb, pc, pd = _in_proj(xf, norm_mix_g[l], _relayout_w_in(w_in[l]), batch, seq, blk,
                                          da_q_norm_g[l], da_k_norm_g[l], kpos)
        o_a = _diff_attention(qt, k, vt, batch, seq, blk, da_lambda_q1[l], da_lambda_k1[l], da_lambda_q2[l],
                              da_lambda_k2[l], da_out_norm_g[l], lam_init)
        o_b, o_c, o_d = _recurrent_mixers(
            pb, pc, pd, batch, seq,
            (gla_gate_up[l], gla_gate_b[l], gla_out_norm_g[l], hgrn_lb_logits, hgrn_out_norm_g[l]),
            (rw_shift_mu[l], rw_w0[l], rw_w_up[l], rw_a0[l], rw_a_up[l], rw_g_up[l], rw_k_k[l], rw_k_a[l],
             rw_r_k[l].reshape(-1), rw_ln_g[l], rw_ln_b[l]), l)
        xf = _post(xf, (o_a, o_b, o_c, o_d), w_out[l].astype(BF16), norm_mlp_g[l],
                   w_mlp_up[l].astype(BF16), w_mlp_down[l].astype(BF16))
    return xf.reshape(batch, seq, D_MODEL)
```

```python
import functools
import math

import jax
import jax.numpy as jnp
import ml_dtypes
import numpy as np
from jax import lax
from jax.experimental import pallas as pl
from jax.experimental.pallas import tpu as pltpu

F32, BF16 = jnp.float32, jnp.bfloat16

D_MODEL = 1024
GROUP = 256
HEADS = 4
HEAD_V = 64
DA_QK = 32
GLA_K = 32
GLA_RANK = 16
GLA_NORMALIZER = 16.0
HG_K = 64
RW_DECAY_SCALE = 0.606531
RW_LN_EPS = 64e-5
D_FF = 4 * D_MODEL
RMS_EPS = 1e-6
MASK_VALUE = -1e30
TINY = 1e-30
DEPTH = 4

LANE = 128
A_W, B_W, C_W, D_W = 768, 896, 1024, 896
P_W = A_W + B_W + C_W + D_W

ROW_TILE = 512
TIME_BLOCK = 256
REC_ROWS = 2
ATT_BLOCK = 256
ATT_UNROLL = 4
ATT_VA = 80
ATT_KA = 80
LOG2E = 1.4426950408889634
GLA_CHUNK = 32
GLA_SUB = 16
RW_CHUNK = 64
VMEM_LIMIT = 56 * 1024 * 1024


def _dot(a, b):
    return jnp.dot(a, b, preferred_element_type=F32)


def _dot_nt(a, b):
    return lax.dot_general(a, b, (((1,), (1,)), ((), ())), preferred_element_type=F32)


def _dot_tn(a, b):
    return lax.dot_general(a, b, (((0,), (0,)), ((), ())), preferred_element_type=F32)


def _split(x, parts):
    out, r = [], x
    for i in range(parts):
        h = r.astype(BF16)
        out.append(h)
        if i + 1 < parts:
            r = r - h.astype(F32)
    return out


def _dot_x_exact(x, m, parts):
    acc = None
    for h in _split(x, parts):
        d = _dot(h, m)
        acc = d if acc is None else acc + d
    return acc


def _dot_exact_x(m, x, parts):
    acc = None
    for h in _split(x, parts):
        d = _dot(m, h)
        acc = d if acc is None else acc + d
    return acc


def _sigmoid(x):
    return 1.0 / (1.0 + jnp.exp(-x))


def _silu(x):
    return x * _sigmoid(x)


def _const_spec(shape):
    zeros = (0,) * len(shape)
    return pl.BlockSpec(shape, lambda *_: zeros)


def _chunk_row_bcast(ref, idx, row, chunk):
    n_rows, width = ref.shape[1], ref.shape[2]
    pieces = []
    for c in range(n_rows // chunk):
        r = c * chunk + row
        pieces.append(jnp.broadcast_to(ref[idx, r:r + 1, :], (chunk, width)))
    return jnp.concatenate(pieces, axis=0)


def _params(*sem):
    return pltpu.CompilerParams(dimension_semantics=sem, vmem_limit_bytes=VMEM_LIMIT)


def _in_proj_body(x_ref, g_ref, w_ref, gq_ref, gk_ref, g32_ref, kpos_ref,
                  qt_ref, k_ref, vt_ref, pb_ref, pc_ref, pd_ref):
    x = x_ref[...]
    h = (x * lax.rsqrt(jnp.mean(x * x, axis=-1, keepdims=True) + RMS_EPS) * g_ref[...]).astype(BF16)
    pa = _dot(h, w_ref[:, 0:A_W])
    q, k = pa[:, 0:GROUP], pa[:, GROUP:2 * GROUP]
    qm = _dot_x_exact(q * q, g32_ref[...], 2)
    km = _dot_x_exact(k * k, g32_ref[...], 2)
    off = A_W
    for o_ref in (pb_ref, pc_ref, pd_ref):
        n = o_ref.shape[1]
        o_ref[...] = _dot(h, w_ref[:, off:off + n])
        off += n
    _attention_layouts(pa, qm, km, gq_ref, gk_ref, kpos_ref, qt_ref, k_ref, vt_ref)


def _in_proj(x, g, w, batch, seq, blk, gq, gk, kpos):
    tm = min(ROW_TILE, seq)
    nt = seq // tm
    n = batch * seq
    row = lambda width: pl.BlockSpec((tm, width), lambda b, t: (b * nt + t, 0))
    tile = lambda gain: jnp.tile(gain, GROUP // DA_QK).reshape(1, GROUP)
    return pl.pallas_call(
        _in_proj_body,
        grid=(batch, nt),
        in_specs=[row(D_MODEL), _const_spec((1, D_MODEL)), _const_spec((D_MODEL, P_W)),
                  _const_spec((1, GROUP)), _const_spec((1, GROUP)), _const_spec((GROUP, GROUP)),
                  _const_spec(kpos.shape)],
        out_specs=[pl.BlockSpec((1, 2, HEADS * ATT_KA, tm), lambda b, t: (b, 0, 0, t)),
                   pl.BlockSpec((1, HEADS, tm, ATT_KA), lambda b, t: (b, 0, t, 0)),
                   pl.BlockSpec((1, HEADS, tm // blk, ATT_VA, blk), lambda b, t: (b, 0, t, 0, 0)),
                   row(B_W), row(C_W), row(D_W)],
        out_shape=[jax.ShapeDtypeStruct((batch, 2, HEADS * ATT_KA, seq), BF16),
                   jax.ShapeDtypeStruct((batch, HEADS, seq, ATT_KA), BF16),
                   jax.ShapeDtypeStruct((batch, HEADS, seq // blk, ATT_VA, blk), BF16)]
                  + [jax.ShapeDtypeStruct((n, wd), F32) for wd in (B_W, C_W, D_W)],
        compiler_params=_params("parallel", "parallel"),
        name="in_proj",
    )(x, g.reshape(1, D_MODEL), w, tile(gq), tile(gk), _group_mean_matrix(GROUP, DA_QK), kpos)


def _post_body(x_ref, oa_ref, ob_ref, oc_ref, od_ref, wo_ref, g_ref, wu_ref, wd_ref, out_ref):
    mixed = jnp.concatenate([r[...] for r in (oa_ref, ob_ref, oc_ref, od_ref)], axis=1)
    x1 = x_ref[...] + _dot(mixed, wo_ref[...])
    h = (x1 * lax.rsqrt(jnp.mean(x1 * x1, axis=-1, keepdims=True) + RMS_EPS) * g_ref[...]).astype(BF16)
    y = x1
    for f in range(D_FF // D_MODEL):
        cols = slice(f * D_MODEL, (f + 1) * D_MODEL)
        u = jnp.square(jnp.maximum(_dot(h, wu_ref[:, cols]), 0.0)).astype(BF16)
        y = y + _dot(u, wd_ref[cols, :])
    out_ref[...] = y


def _post(x, outs, w_out, g, w_up, w_down):
    n = x.shape[0]
    tm = min(ROW_TILE, n)
    row = lambda width: pl.BlockSpec((tm, width), lambda i: (i, 0))
    return pl.pallas_call(
        _post_body,
        grid=(n // tm,),
        in_specs=[row(D_MODEL)] + [row(GROUP)] * 4 + [
            _const_spec((D_MODEL, D_MODEL)), _const_spec((1, D_MODEL)),
            _const_spec((D_MODEL, D_FF)), _const_spec((D_FF, D_MODEL))],
        out_specs=row(D_MODEL),
        out_shape=jax.ShapeDtypeStruct((n, D_MODEL), F32),
        compiler_params=_params("parallel"),
        name="post",
    )(x, *outs, w_out, g.reshape(1, D_MODEL), w_up, w_down)


def _group_mean_matrix(width, group):
    i = jnp.arange(width) // group
    return ((i[:, None] == i[None, :]).astype(F32) / group).astype(BF16)


def _group_sum_matrix(width, group):
    i = jnp.arange(width) // group
    return (i[:, None] == i[None, :]).astype(BF16)


def _attention_layouts(pa, qm, km, gq_ref, gk_ref, kpos_ref, qt_ref, k_ref, vt_ref):
    tm = pa.shape[0]
    blk = vt_ref.shape[-1]
    q, k = pa[:, 0:GROUP], pa[:, GROUP:2 * GROUP]
    q_t = (q * lax.rsqrt(qm + RMS_EPS) * gq_ref[...] * (DA_QK ** -0.5 * LOG2E)).T
    comp = (lax.broadcasted_iota(jnp.int32, q_t.shape, 0) // DA_QK) % 2
    n_extra = ATT_KA - 2 * DA_QK
    q_extra = (lax.broadcasted_iota(jnp.int32, (n_extra, tm), 0) < 3).astype(BF16)
    for c in range(2):
        q_c = jnp.where(comp == c, q_t, 0.0).astype(BF16)
        for h in range(HEADS):
            qt_ref[0, c, h * ATT_KA:h * ATT_KA + 2 * DA_QK, :] = q_c[h * HEAD_V:(h + 1) * HEAD_V]
            qt_ref[0, c, h * ATT_KA + 2 * DA_QK:(h + 1) * ATT_KA, :] = q_extra
    kn = (k * lax.rsqrt(km + RMS_EPS) * gk_ref[...]).astype(BF16)
    v_t = pa[:, 2 * GROUP:3 * GROUP].T.astype(BF16)
    ones_row = (lax.broadcasted_iota(jnp.int32, (ATT_VA - HEAD_V, blk), 0) == 0).astype(BF16)
    for h in range(HEADS):
        k_ref[0, h, :, 0:2 * DA_QK] = kn[:, h * HEAD_V:(h + 1) * HEAD_V]
        for sb in range(tm // blk):
            k_ref[0, h, sb * blk:(sb + 1) * blk, 2 * DA_QK:ATT_KA] = kpos_ref[h]
        for sb in range(tm // blk):
            vt_ref[0, h, sb, 0:HEAD_V, :] = v_t[h * HEAD_V:(h + 1) * HEAD_V, sb * blk:(sb + 1) * blk]
            vt_ref[0, h, sb, HEAD_V:ATT_VA, :] = ones_row


def _attn_body(qt_ref, k_ref, vt_ref, td_ref, lp_ref, go_ref, o_ref, m_ref, acc_ref, s_ref, *,
               lam_init, slopes, blk):
    i = pl.program_id(1)
    chains = [(h, c) for h in range(HEADS) for c in range(2)]

    def stage(cur, cur_slot, first, nxt=None, nxt_slot=None):
        for n, (h, c) in enumerate(chains):
            if nxt is not None:
                q_hc = qt_ref[0, c, h * ATT_KA:(h + 1) * ATT_KA, :]
                s_ref[nxt_slot, n] = _dot(k_ref[0, h, nxt], q_hc)
            if cur is None:
                continue
            off = -cur.astype(F32) * (slopes[h] * LOG2E * blk)
            s = s_ref[cur_slot, n]
            if first:
                s = s + td_ref[...]
            smax = jnp.max(s, axis=0, keepdims=True) - off
            if first:
                mx = smax
            else:
                m = m_ref[n]
                mx = jnp.maximum(m, smax)
            m_ref[n] = mx
            p = jnp.exp2(s - (mx + off)).astype(BF16)
            pv = _dot(vt_ref[0, h, cur], p)
            acc_ref[n] = pv if first else jnp.exp2(m - mx) * acc_ref[n] + pv

    stage(None, None, False, nxt=i, nxt_slot=0)
    stage(i, 0, True, nxt=0, nxt_slot=1)

    def body(u, carry):
        for d in range(ATT_UNROLL):
            stage(ATT_UNROLL * u + d, (d + 1) % 2, False, nxt=ATT_UNROLL * u + d + 1, nxt_slot=d % 2)
        return carry

    trips = i // ATT_UNROLL
    lax.fori_loop(0, trips, body, 0)
    done = trips * ATT_UNROLL
    for rem in range(1, ATT_UNROLL):

        @pl.when(i - done == rem)
        def _(rem=rem):
            for d in range(rem):
                more = d + 1 < rem
                stage(done + d, (d + 1) % 2, False, nxt=done + d + 1 if more else None,
                      nxt_slot=d % 2 if more else None)

    lp = lp_ref[...]
    lam = (jnp.exp(jnp.sum(lp[0:1] * lp[1:2], axis=-1, keepdims=True))
           - jnp.exp(jnp.sum(lp[2:3] * lp[3:4], axis=-1, keepdims=True)) + lam_init)
    outs = []
    for h in range(HEADS):
        a1, a2 = acc_ref[2 * h], acc_ref[2 * h + 1]
        o = a1[0:HEAD_V] / a1[HEAD_V:HEAD_V + 1] - lam * (a2[0:HEAD_V] / a2[HEAD_V:HEAD_V + 1])
        outs.append(o * lax.rsqrt(jnp.mean(o * o, axis=0, keepdims=True) + RMS_EPS) * go_ref[...]
                    * (1.0 - lam_init))
    o_ref[...] = jnp.concatenate(outs, axis=0).T.astype(BF16)


def _alibi_slopes():
    return tuple(2.0 ** (-8.0 * (h + 1) / HEADS) for h in range(HEADS))


def _alibi_key_terms(blk):
    kpos = np.zeros((HEADS, blk, ATT_KA - 2 * DA_QK), ml_dtypes.bfloat16)
    for h, s in enumerate(_alibi_slopes()):
        r = np.float32(s * LOG2E) * np.arange(blk, dtype=np.float32)
        for part in range(3):
            kpos[h, :, part] = r.astype(ml_dtypes.bfloat16)
            r = r - kpos[h, :, part].astype(np.float32)
    return jnp.asarray(kpos)


def _diff_attention(qt, k, vt, batch, seq, blk, lq1, lk1, lq2, lk2, g_out, lam_init):
    nb = seq // blk
    slopes = _alibi_slopes()
    kh = k.reshape(batch, HEADS, nb, blk, ATT_KA)
    rel = jnp.arange(blk)[None, :] - jnp.arange(blk)[:, None]
    td = jnp.where(rel >= 0, 0.0, MASK_VALUE).astype(F32)
    lp = jnp.stack([lq1, lk1, lq2, lk2])
    go = jnp.broadcast_to(g_out[:, None], (HEAD_V, blk))
    body = functools.partial(_attn_body, lam_init=lam_init, slopes=slopes, blk=blk)
    return pl.pallas_call(
        body,
        grid=(batch, nb),
        in_specs=[
            pl.BlockSpec((1, 2, HEADS * ATT_KA, blk), lambda b, i: (b, 0, 0, i)),
            pl.BlockSpec((1, HEADS, nb, blk, ATT_KA), lambda b, i: (b, 0, 0, 0, 0)),
            pl.BlockSpec((1, HEADS, nb, ATT_VA, blk), lambda b, i: (b, 0, 0, 0, 0)),
            _const_spec((blk, blk)),
            _const_spec((4, DA_QK)), _const_spec((HEAD_V, blk))],
        out_specs=pl.BlockSpec((blk, GROUP), lambda b, i: (b * nb + i, 0)),
        out_shape=jax.ShapeDtypeStruct((batch * seq, GROUP), BF16),
        scratch_shapes=[pltpu.VMEM((2 * HEADS, 1, blk), F32), pltpu.VMEM((2 * HEADS, ATT_VA, blk), F32),
                        pltpu.VMEM((2, 2 * HEADS, blk, blk), F32)],
        compiler_params=_params("parallel", "arbitrary"),
        name="diff_attention",
    )(qt, kh, vt, td, lp, go)


def _gla_consts(tb, hk):
    c, sub, nc, kh = GLA_CHUNK, GLA_SUB, tb // GLA_CHUNK, hk // HEADS
    nsub = c // sub
    width = nsub * HEADS * sub
    t = jnp.arange(tb)
    lseg = ((t[:, None] // c == t[None, :] // c) & (t[None, :] <= t[:, None])).astype(BF16)
    r = jnp.arange(sub * hk)
    col = jnp.arange(width)
    ind = ((r[:, None] // hk == col[None, :] % sub)
           & ((r[:, None] % hk) // kh == (col[None, :] // sub) % HEADS)).astype(BF16)
    submask = ((t[:, None] % c) // sub == col[None, :] // (HEADS * sub)).astype(F32)
    vr = jnp.arange(nc * width)
    row_h = (vr[:, None] // sub) % HEADS
    vmask = (row_h == jnp.arange(GROUP)[None, :] // HEAD_V).astype(BF16)
    kmask = (((vr[:, None] % width) // (HEADS * sub) == 0)
             & (row_h == jnp.arange(hk)[None, :] // kh)).astype(BF16)
    bdmask = (jnp.arange(GROUP)[:, None] // HEAD_V == jnp.arange(hk)[None, :] // kh).astype(F32)
    return lseg, ind, submask, vmask, kmask, bdmask


class _GlaStream:
    def __init__(self, q, k, lf, v, consts, scratch):
        self.q, self.k, self.lf, self.v = q, k, lf, v
        self.lseg, self.ind, self.submask, self.vmask, self.kmask, self.bdmask = consts
        self.st, self.tcat, self.bk, self.stack = scratch


def _gla_core(streams):
    c, sub = GLA_CHUNK, GLA_SUB
    assert c == 2 * sub
    tb = streams[0].q.shape[0]
    nc = tb // c
    width = 2 * HEADS * sub
    chunk = lambda x, ci, n=c: x[ci * n:(ci + 1) * n]
    for s in streams:
        s.b = _dot_exact_x(s.lseg[...], s.lf, 3) * LOG2E
    yield
    for s in streams:
        hk = s.q.shape[1]
        s.bk[0] = s.b
        s.bk[1] = s.k
        s.row = lax.broadcasted_iota(jnp.int32, (tb, hk), 0)
        rowmod = s.row % sub
        for jj in range(sub):
            bj = _chunk_row_bcast(s.bk, 0, jj, sub)
            kj = _chunk_row_bcast(s.bk, 1, jj, sub)
            t = s.q * kj * jnp.exp2(s.b - bj)
            s.tcat[:, jj * hk:(jj + 1) * hk] = jnp.where(rowmod >= jj, t, 0.0).astype(BF16)
            if jj % 4 == 3:
                yield
    yield
    for s in streams:
        s.a = _dot(s.tcat[...], s.ind[...]) * s.submask[...]
    yield
    for s in streams:
        bref = _chunk_row_bcast(s.bk, 0, sub - 1, c)
        second = s.row % c >= sub
        s.qm = jnp.where(second, s.q * jnp.exp2(s.b - bref), 0.0).astype(BF16)
        kt = jnp.where(second, 0.0, s.k * jnp.exp2(bref - s.b)).astype(BF16)
        s.k_exp = jnp.concatenate([kt[ci * c:ci * c + sub] for ci in range(nc) for _ in range(2 * HEADS)],
                                  axis=0) * s.kmask[...]
    yield
    for s in streams:
        s.a_off = [_dot_nt(chunk(s.qm, ci), chunk(s.k_exp, ci, width)) for ci in range(nc)]
    yield
    for s in streams:
        s.a_c = [(chunk(s.a, ci) + s.a_off[ci]).astype(BF16) for ci in range(nc)]
        s.vb = s.v.astype(BF16)
        s.v_bd = jnp.concatenate([s.vb[ci * c + h * sub:ci * c + (h + 1) * sub]
                                  for ci in range(nc) for h in range(2) for _ in range(HEADS)],
                                 axis=0) * s.vmask[...]
    yield
    for s in streams:
        s.o = [_dot(s.a_c[ci], chunk(s.v_bd, ci, width)) for ci in range(nc)]
    yield
    for s in streams:
        blast = _chunk_row_bcast(s.bk, 0, c - 1, c)
        s.qe = (s.q * jnp.exp2(s.b)).astype(BF16)
        s.ke = (s.k * jnp.exp2(blast - s.b)).astype(BF16)
    yield
    for s in streams:
        s.pt = [_dot_tn(s.vb[ci * c:(ci + 1) * c], s.ke[ci * c:(ci + 1) * c]) * s.bdmask[...]
                for ci in range(nc)]
    yield
    for s in streams:
        hk = s.q.shape[1]
        st = s.st[...]
        for ci in range(nc):
            s.stack[:, ci * hk:(ci + 1) * hk] = st.astype(BF16)
            st = st * jnp.exp2(s.bk[0, ci * c + c - 1:ci * c + c, :]) + s.pt[ci]
        s.st[...] = st
    yield
    return [jnp.concatenate([s.o[ci] + _dot_nt(chunk(s.qe, ci), s.stack[:, ci * s.q.shape[1]:(ci + 1) * s.q.shape[1]])
                             for ci in range(nc)], axis=0) for s in streams]


def _gla_scratch(tb, hk):
    nc = tb // GLA_CHUNK
    return [pltpu.VMEM((GROUP, hk), F32), pltpu.VMEM((tb, GLA_SUB * hk), BF16),
            pltpu.VMEM((2, tb, hk), F32), pltpu.VMEM((GROUP, nc * hk), BF16)]


N_GLA_CONSTS = 6
N_GLA_SCRATCH = 4


def _gla_hgrn_body(pb_ref, pc_ref, gup_ref, gb_ref, gout_b_ref, lg_ref, gout_c_ref, g64_ref, *rest, layer):
    consts_b, consts_c = rest[:N_GLA_CONSTS], rest[N_GLA_CONSTS:2 * N_GLA_CONSTS]
    ob_ref, oc_ref = rest[2 * N_GLA_CONSTS:2 * N_GLA_CONSTS + 2]
    scratch = rest[2 * N_GLA_CONSTS + 2:]
    scratch_b, scratch_c = scratch[:N_GLA_SCRATCH], scratch[N_GLA_SCRATCH:]
    hk = HEADS * GLA_K
    gate_lo, og_lo = 2 * hk + GROUP, 2 * hk + GROUP + LANE
    logit = _dot(pb_ref[:, gate_lo:og_lo].astype(BF16), gup_ref[...]) + gb_ref[...]
    lf_b = (jnp.minimum(logit, 0.0) - jnp.log(1.0 + jnp.exp(-jnp.abs(logit)))) * (1.0 / GLA_NORMALIZER)
    gla = _GlaStream(pb_ref[:, 0:hk] * (GLA_K ** -0.5), pb_ref[:, hk:2 * hk], lf_b,
                     pb_ref[:, 2 * hk:2 * hk + GROUP], consts_b, scratch_b)
    lg = lg_ref[...]
    e = jnp.exp(lg - jnp.max(lg, axis=0, keepdims=True))
    p = e / jnp.sum(e, axis=0, keepdims=True)
    cs = p[0:1]
    for i in range(1, layer + 1):
        cs = cs + p[i:i + 1]
    lb = cs - p[0:1]
    z = pc_ref[:, GROUP:2 * GROUP]
    forget = lb + (1.0 - lb) * _sigmoid(z)
    hgrn = _GlaStream(_silu(pc_ref[:, 0:GROUP]), (1.0 - lb) * _sigmoid(-z),
                      jnp.log(jnp.maximum(forget, TINY)), pc_ref[:, 2 * GROUP:3 * GROUP], consts_c, scratch_c)
    o_b, o_c = yield from _gla_core([gla, hgrn])
    for o, og, gout_ref, o_ref in ((o_b, pb_ref[:, og_lo:og_lo + GROUP], gout_b_ref, ob_ref),
                                   (o_c, pc_ref[:, 3 * GROUP:4 * GROUP], gout_c_ref, oc_ref)):
        ms = _dot_x_exact(o * o, g64_ref[...], 2)
        o_ref[...] = (o * lax.rsqrt(ms + RMS_EPS) * gout_ref[...] * _silu(og)).astype(BF16)


def _gla_hgrn_operands(tb, gate_up, gate_b, g_out_b, lb_logits, g_out_c):
    hk_b, hk_c = HEADS * GLA_K, HEADS * HG_K
    gup = jnp.zeros((LANE, hk_b), F32).at[:GLA_RANK].set(gate_up).astype(BF16)
    tile = lambda g: jnp.tile(g, HEADS).reshape(1, GROUP)
    extra = [gup, gate_b.reshape(1, hk_b), tile(g_out_b), lb_logits, tile(g_out_c),
             _group_mean_matrix(GROUP, HEAD_V), *_gla_consts(tb, hk_b), *_gla_consts(tb, hk_c)]
    return extra, _gla_scratch(tb, hk_b) + _gla_scratch(tb, hk_c)


def _rwkv_consts(tb):
    c, nch = RW_CHUNK, tb // RW_CHUNK
    t = jnp.arange(tb)
    same = t[:, None] // c == t[None, :] // c
    lseg = (same & (t[None, :] <= t[:, None])).astype(BF16)
    strict = (same & (t[None, :] < t[:, None])).astype(F32)
    incl = (same & (t[None, :] <= t[:, None])).astype(F32)
    eye = jnp.eye(tb, dtype=F32)
    hmask = (jnp.arange(HEADS)[:, None] == jnp.arange(GROUP)[None, :] // HEAD_V).astype(F32)
    bd = (jnp.arange(GROUP)[:, None] // HEAD_V == jnp.arange(GROUP)[None, :] // HEAD_V).astype(F32)
    return lseg, strict, incl, eye, hmask, bd


def _rwkv_core(r, logw, k2, v, kk, a, consts, mt_ref, gam_ref, stack_ref):
    lseg_ref, strict_ref, incl_ref, eye_ref, hmask_ref, bd_ref = consts
    tb = r.shape[0]
    c = RW_CHUNK
    nch = tb // c
    gam = _dot_exact_x(lseg_ref[...], logw, 3)
    yield
    gam_ref[0] = gam
    glast = _chunk_row_bcast(gam_ref, 0, c - 1, c)
    eng = jnp.exp(-gam)
    ecl = jnp.exp(glast - gam)
    beta = a * kk
    abar = kk * jnp.exp(gam - logw)
    rbar = r * jnp.exp(gam)
    yb = jnp.concatenate([beta * eng, k2 * eng], axis=0).astype(BF16)
    bhat = beta * ecl
    khat = k2 * ecl
    strict, incl = strict_ref[...] > 0.0, incl_ref[...] > 0.0
    eye = eye_ref[...]
    heads = range(HEADS)
    hms = [hmask_ref[h:h + 1, :] for h in heads]
    abar_h = [abar * hms[h] for h in heads]
    vh = [(v * hms[h]).astype(BF16) for h in heads]
    pw, a_rb, a_rk, z = [], [], [], []
    for h in heads:
        xh = jnp.concatenate([abar_h[h], rbar * hms[h]], axis=0).astype(BF16)
        s = _dot_nt(xh, yb)
        pw.append(jnp.where(strict, -s[:tb, :tb], 0.0))
        a_ak = jnp.where(strict, s[:tb, tb:], 0.0).astype(BF16)
        a_rb.append(jnp.where(incl, s[tb:, :tb], 0.0).astype(BF16))
        a_rk.append(jnp.where(incl, s[tb:, tb:], 0.0).astype(BF16))
        z.append(jnp.concatenate([abar_h[h], _dot(a_ak, vh[h])], axis=1).astype(BF16))
        yield
    tinv = [eye + pw[h] for h in heads]
    for _ in range(int(math.log2(c)) - 1):
        pwb = [pw[h].astype(BF16) for h in heads]
        pw = [_dot(pwb[h], pwb[h]) for h in heads]
        yield
        tinv = [tinv[h] + _dot(tinv[h].astype(BF16), pw[h].astype(BF16)) for h in heads]
        yield
    wu_h = [_dot(tinv[h].astype(BF16), z[h]) for h in heads]
    yield
    om_h = [_dot(a_rb[h], wu_h[h].astype(BF16)) for h in heads]
    arkv_h = [_dot(a_rk[h], vh[h]) for h in heads]
    yield
    wu, om, arkv = sum(wu_h[1:], wu_h[0]), sum(om_h[1:], om_h[0]), sum(arkv_h[1:], arkv_h[0])
    wbar, u0 = wu[:, :GROUP], wu[:, GROUP:]
    omega = rbar - om[:, :GROUP]
    y0 = arkv - om[:, GROUP:]
    wb, bhb = wbar.astype(BF16), bhat.astype(BF16)
    vb, nub, khb = v.astype(BF16), (-u0).astype(BF16), khat.astype(BF16)
    x_c, psi = [], []
    for ci in range(nch):
        rows = slice(ci * c, (ci + 1) * c)
        x_c.append((_dot_tn(wb[rows], bhb[rows]) * bd_ref[...]).astype(BF16))
        psi.append(_dot_tn(jnp.concatenate([vb[rows], nub[rows]], axis=0),
                           jnp.concatenate([khb[rows], bhb[rows]], axis=0)) * bd_ref[...])
    yield
    mt = mt_ref[...]
    for ci in range(nch):
        stack_ref[:, ci * GROUP:(ci + 1) * GROUP] = mt.astype(BF16)
        gl = jnp.exp(gam_ref[0, ci * c + c - 1:ci * c + c, :])
        mt = mt * gl - _dot(mt.astype(BF16), x_c[ci]) + psi[ci]
        yield
    mt_ref[...] = mt
    omb = omega.astype(BF16)
    return y0 + jnp.concatenate([_dot_nt(omb[ci * c:(ci + 1) * c], stack_ref[:, ci * GROUP:(ci + 1) * GROUP])
                                 for ci in range(nch)], axis=0)


def _rwkv_body(pd_ref, mu_ref, w0_ref, wup_ref, a0_ref, aup_ref, gup_ref, kk_ref, ka_ref, rk_ref,
               lng_ref, lnb_ref, g64_ref, s64_ref, *rest):
    consts, (o_ref, mt_ref, carry_ref, gam_ref, stack_ref) = rest[:6], rest[6:]
    tb = pd_ref.shape[0]
    p = pd_ref[...]
    rowid = lax.broadcasted_iota(jnp.int32, p.shape, 0)
    prev = jnp.where(rowid == 0, carry_ref[...], pltpu.roll(p, 1, axis=0))
    carry_ref[...] = p[tb - 1:tb, :]
    p = p + mu_ref[...] * (prev - p)
    r, k, v = p[:, 0:GROUP], p[:, GROUP:2 * GROUP], p[:, 2 * GROUP:3 * GROUP]
    low = p[:, 3 * GROUP:3 * GROUP + LANE]
    logw = -RW_DECAY_SCALE * _sigmoid(w0_ref[...] + _dot(jnp.tanh(low).astype(BF16), wup_ref[...]))
    a = _sigmoid(a0_ref[...] + _dot(low.astype(BF16), aup_ref[...]))
    g = _dot(_sigmoid(low).astype(BF16), gup_ref[...])
    kk = k * kk_ref[...]
    kk = kk * lax.rsqrt(jnp.maximum(_dot_x_exact(kk * kk, s64_ref[...], 2), 1e-24))
    k2 = k * (1.0 + (a - 1.0) * ka_ref[...])
    yield
    y = yield from _rwkv_core(r, logw, k2, v, kk, a, consts, mt_ref, gam_ref, stack_ref)
    yield
    mean = _dot_x_exact(y, g64_ref[...], 2)
    d = y - mean
    yield
    var = _dot_x_exact(d * d, g64_ref[...], 2)
    yn = d * lax.rsqrt(var + RW_LN_EPS) * lng_ref[...] + lnb_ref[...]
    bonus = _dot_x_exact(r * k2 * rk_ref[...], s64_ref[...], 2) * v
    o_ref[...] = ((yn + bonus) * g).astype(BF16)


def _rwkv_operands(tb, mu, w0, w_up, a0, a_up, g_up, k_k, k_a, r_k, ln_g, ln_b):
    nch = tb // RW_CHUNK
    row = lambda z: z.reshape(1, -1)
    low = lambda w, start: jnp.zeros((LANE, GROUP), F32).at[start:start + w.shape[0]].set(w).astype(BF16)
    extra = [row(mu), row(w0), low(w_up, 0), row(a0), low(a_up, 32), low(g_up, 64), row(k_k), row(k_a),
             row(r_k), row(ln_g), row(ln_b), _group_mean_matrix(GROUP, HEAD_V),
             _group_sum_matrix(GROUP, HEAD_V), *_rwkv_consts(tb)]
    scratch = [pltpu.VMEM((GROUP, GROUP), F32), pltpu.VMEM((1, D_W), F32),
               pltpu.VMEM((1, tb, GROUP), F32), pltpu.VMEM((GROUP, nch * GROUP), BF16)]
    return extra, scratch


def _recurrent_body(*refs, n_rw, n_gh, n_rw_scratch, n_gh_scratch, rows, layer):
    pd_ref, pb_ref, pc_ref = refs[:3]
    rw_extra = refs[3:3 + n_rw]
    gh_extra = refs[3 + n_rw:3 + n_rw + n_gh]
    od_ref, ob_ref, oc_ref = refs[3 + n_rw + n_gh:6 + n_rw + n_gh]
    scratch = refs[6 + n_rw + n_gh:]
    per_row = n_rw_scratch + n_gh_scratch
    rw_scratch = [scratch[r * per_row:r * per_row + n_rw_scratch] for r in range(rows)]
    gh_scratch = [scratch[r * per_row + n_rw_scratch:(r + 1) * per_row] for r in range(rows)]

    @pl.when(pl.program_id(1) == 0)
    def _():
        for r in range(rows):
            for ref in (rw_scratch[r][0], rw_scratch[r][1], gh_scratch[r][0], gh_scratch[r][N_GLA_SCRATCH]):
                ref[...] = jnp.zeros_like(ref)

    live = [_rwkv_body(pd_ref.at[r], *rw_extra, od_ref.at[r], *rw_scratch[r]) for r in range(rows)]
    live += [_gla_hgrn_body(pb_ref.at[r], pc_ref.at[r], *gh_extra, ob_ref.at[r], oc_ref.at[r], *gh_scratch[r],
                            layer=layer) for r in range(rows)]
    while live:
        for gen in list(live):
            if next(gen, StopIteration) is StopIteration:
                live.remove(gen)


def _recurrent_mixers(pb, pc, pd, batch, seq, gla_hgrn_params, rwkv_params, layer):
    tb = min(TIME_BLOCK, seq)
    nt = seq // tb
    rows = REC_ROWS if batch % REC_ROWS == 0 else 1
    rw_extra, rw_scratch = _rwkv_operands(tb, *rwkv_params)
    gh_extra, gh_scratch = _gla_hgrn_operands(tb, *gla_hgrn_params)
    blk3 = lambda width: pl.BlockSpec((rows, tb, width), lambda b, t: (b, t, 0))
    body = functools.partial(_recurrent_body, n_rw=len(rw_extra), n_gh=len(gh_extra),
                             n_rw_scratch=len(rw_scratch), n_gh_scratch=len(gh_scratch), rows=rows, layer=layer)
    o_d, o_b, o_c = pl.pallas_call(
        body,
        grid=(batch // rows, nt),
        in_specs=[blk3(D_W), blk3(B_W), blk3(C_W)] + [_const_spec(e.shape) for e in rw_extra + gh_extra],
        out_specs=[blk3(GROUP)] * 3,
        out_shape=[jax.ShapeDtypeStruct((batch, seq, GROUP), BF16)] * 3,
        scratch_shapes=(rw_scratch + gh_scratch) * rows,
        compiler_params=_params("parallel", "arbitrary"),
        name="recurrent_mixers",
    )(pd.reshape(batch, seq, D_W), pb.reshape(batch, seq, B_W), pc.reshape(batch, seq, C_W),
      *rw_extra, *gh_extra)
    flat = lambda o: o.reshape(batch * seq, GROUP)
    return flat(o_b), flat(o_c), flat(o_d)


def _relayout_w_in(w_in):
    a_end, b_end, c_end = 768, 768 + 784, 768 + 784 + 1024
    wb = w_in[:, a_end:b_end]
    pad = jnp.zeros((w_in.shape[0], LANE - GLA_RANK), w_in.dtype)
    wb = jnp.concatenate([wb[:, :528], pad, wb[:, 528:]], axis=1)
    return jnp.concatenate([w_in[:, :a_end], wb, w_in[:, b_end:c_end], w_in[:, c_end:]], axis=1).astype(BF16)


def kernel(x, norm_mix_g, w_in, da_q_norm_g, da_k_norm_g, da_lambda_q1, da_lambda_k1, da_lambda_q2, da_lambda_k2, da_out_norm_g, gla_gate_up, gla_gate_b, gla_out_norm_g, hgrn_lb_logits, hgrn_out_norm_g, rw_shift_mu, rw_w0, rw_w_up, rw_a0, rw_a_up, rw_g_up, rw_k_k, rw_k_a, rw_r_k, rw_ln_g, rw_ln_b, w_out, norm_mlp_g, w_mlp_up, w_mlp_down):
    batch, seq, _ = x.shape
    xf = x.reshape(batch * seq, D_MODEL)
    blk = min(ATT_BLOCK, seq)
    kpos = _alibi_key_terms(blk)
    for l in range(DEPTH):
        lam_init = 0.8 - 0.6 * math.exp(-0.3 * l)
        qt, k, vt, pb, pc, pd = _in_proj(xf, norm_mix_g[l], _relayout_w_in(w_in[l]), batch, seq, blk,
                                          da_q_norm_g[l], da_k_norm_g[l], kpos)
        o_a = _diff_attention(qt, k, vt, batch, seq, blk, da_lambda_q1[l], da_lambda_k1[l], da_lambda_q2[l],
                              da_lambda_k2[l], da_out_norm_g[l], lam_init)
        o_b, o_c, o_d = _recurrent_mixers(
            pb, pc, pd, batch, seq,
            (gla_gate_up[l], gla_gate_b[l], gla_out_norm_g[l], hgrn_lb_logits, hgrn_out_norm_g[l]),
            (rw_shift_mu[l], rw_w0[l], rw_w_up[l], rw_a0[l], rw_a_up[l], rw_g_up[l], rw_k_k[l], rw_k_a[l],
             rw_r_k[l].reshape(-1), rw_ln_g[l], rw_ln_b[l]), l)
        xf = _post(xf, (o_a, o_b, o_c, o_d), w_out[l].astype(BF16), norm_mlp_g[l],
                   w_mlp_up[l].astype(BF16), w_mlp_down[l].astype(BF16))
    return xf.reshape(batch, seq, D_MODEL)
```
